```python
import jax, jax.numpy as jnp
from jax import lax
import numpy as np

D_MODEL = 1024
BATCH = 8
SEQ = 4096
DEPTH = 1

HEAD_DIM = 64
N_HEADS = D_MODEL // HEAD_DIM
NSA_HEADS = N_HEADS // 2
SWA_HEADS = N_HEADS - NSA_HEADS
NSA_KV = 2
NSA_GROUP = NSA_HEADS // NSA_KV
SWA_KV = 2
SWA_GROUP = SWA_HEADS // SWA_KV
MIX_WIDTH = (NSA_HEADS + SWA_HEADS) * HEAD_DIM
SWA_WINDOW = 128
NSA_WINDOW = 512
CMP_LEN = 32
CMP_STRIDE = 16
CMP_HIDDEN = 256
SLC_BLOCK = 64
SLC_TOPK = 16
NSA_BRANCHES = 3
BAND_QBLOCK = 128
SLC_QCHUNK = 64
D_FF = 2816
CONV_WIDTH = 3
ROPE_THETA = 10000.0
RMS_EPS = 1e-6
NEG = -1e30
BIG = 1e9

IN_SIZES = [
    NSA_HEADS * HEAD_DIM,
    NSA_KV * HEAD_DIM, NSA_KV * HEAD_DIM,
    NSA_KV * HEAD_DIM, NSA_KV * HEAD_DIM,
    NSA_KV * HEAD_DIM, NSA_KV * HEAD_DIM,
    NSA_HEADS * NSA_BRANCHES,
    SWA_HEADS * HEAD_DIM,
    SWA_KV * HEAD_DIM, SWA_KV * HEAD_DIM,
]
IN_WIDTH = int(sum(IN_SIZES))
IN_SPLITS = [int(v) for v in np.cumsum(IN_SIZES)[:-1]]

kernel_name = "hybrid_nsa_swasink_convffn"


def rms_norm(x, g):
    xf = x.astype(jnp.float32)
    y = xf * lax.rsqrt(jnp.mean(xf * xf, axis=-1, keepdims=True) + RMS_EPS)
    return (y * g.astype(jnp.float32)).astype(x.dtype)


def rope(x, pos):
    half = HEAD_DIM // 2
    inv = ROPE_THETA ** (-jnp.arange(half, dtype=jnp.float32) / half)
    ang = pos.astype(jnp.float32)[:, None] * inv[None, :]
    cos = jnp.cos(ang)[None, :, None, :]
    sin = jnp.sin(ang)[None, :, None, :]
    xf = x.astype(jnp.float32)
    x1, x2 = xf[..., :half], xf[..., half:]
    return jnp.concatenate([x1 * cos - x2 * sin, x2 * cos + x1 * sin], axis=-1).astype(x.dtype)


def compress(kv, pe, w1, w2):
    B, S, H, D = kv.shape
    ch = kv.reshape(B, S // CMP_STRIDE, CMP_STRIDE, H, D)
    blocks = jnp.concatenate([ch[:, :-1], ch[:, 1:]], axis=2)
    blocks = blocks + pe[None, None, :, None, :]
    nc = blocks.shape[1]
    flat = blocks.transpose(0, 1, 3, 2, 4).reshape(B, nc, H, CMP_LEN * D)
    return jax.nn.gelu(flat @ w1) @ w2


def compressed_attention(q, kc, vc, t):
    nc = kc.shape[1]
    s = jnp.einsum('bthgd,bchd->bhgtc', q, kc).astype(jnp.float32) * (HEAD_DIM ** -0.5)
    ends = jnp.arange(nc) * CMP_STRIDE + CMP_LEN - 1
    valid = ends[None, :] <= t[:, None]
    s = jnp.where(valid, s, NEG)
    m = jnp.max(s, axis=-1, keepdims=True)
    e = jnp.where(valid, jnp.exp(s - m), 0.0)
    p = e / jnp.maximum(jnp.sum(e, axis=-1, keepdims=True), 1e-30)
    o = jnp.einsum('bhgtc,bchd->bthgd', p.astype(vc.dtype), vc)
    return o, p


def overlap_matrix(nc, nsb):
    i = np.arange(nc)[:, None]
    j = np.arange(nsb)[None, :]
    cs, ce = i * CMP_STRIDE, i * CMP_STRIDE + CMP_LEN - 1
    bs, be = j * SLC_BLOCK, j * SLC_BLOCK + SLC_BLOCK - 1
    return jnp.asarray(((cs <= be) & (ce >= bs)).astype(np.float32))


def select_blocks(p_cmp, t, nsb):
    p_grp = jnp.sum(p_cmp, axis=2)
    imp = jnp.einsum('bhtc,cj->bhtj', p_grp, overlap_matrix(p_cmp.shape[-1], nsb))
    jb = jnp.arange(nsb)[None, :]
    tb = (t // SLC_BLOCK)[:, None]
    forced = (jb == 0) | (jb == tb) | (jb == tb - 1)
    causal = jb * SLC_BLOCK <= t[:, None]
    score = jnp.where(forced, BIG, jnp.where(causal, imp, NEG))
    _, idx = lax.top_k(score, min(SLC_TOPK, nsb))
    return idx


def selected_attention(q, k, v, idx, t):
    B, S, Hkv, G, D = q.shape
    nsb = S // SLC_BLOCK
    K = idx.shape[-1]
    nqc = S // SLC_QCHUNK
    kb = k.reshape(B, nsb, SLC_BLOCK, Hkv, D).transpose(0, 3, 1, 2, 4)
    vb = v.reshape(B, nsb, SLC_BLOCK, Hkv, D).transpose(0, 3, 1, 2, 4)
    qc = q.reshape(B, nqc, SLC_QCHUNK, Hkv, G, D).transpose(1, 0, 2, 3, 4, 5)
    ic = idx.reshape(B, Hkv, nqc, SLC_QCHUNK, K).transpose(2, 0, 1, 3, 4)
    tc = t.reshape(nqc, SLC_QCHUNK)
    gather = jax.vmap(jax.vmap(lambda blk, ix: blk[ix]))
    offs = jnp.arange(SLC_BLOCK)

    def chunk(args):
        qx, ix, tx = args
        kg = gather(kb, ix)
        vg = gather(vb, ix)
        s = jnp.einsum('bqhgd,bhqnkd->bhgqnk', qx, kg).astype(jnp.float32) * (HEAD_DIM ** -0.5)
        pos = ix[..., None] * SLC_BLOCK + offs
        mask = pos <= tx[None, None, :, None, None]
        s = jnp.where(mask[:, :, None], s, NEG)
        p = jax.nn.softmax(s.reshape(B, Hkv, G, SLC_QCHUNK, K * SLC_BLOCK), axis=-1)
        p = p.reshape(B, Hkv, G, SLC_QCHUNK, K, SLC_BLOCK).astype(vg.dtype)
        return jnp.einsum('bhgqnk,bhqnkd->bqhgd', p, vg)

    o = lax.map(chunk, (qc, ic, tc))
    return o.transpose(1, 0, 2, 3, 4, 5).reshape(B, S, Hkv, G, D)


def banded_attention(q, k, v, window, sinks):
    B, S, Hkv, G, D = q.shape
    QB = BAND_QBLOCK
    span = window + QB
    kp = jnp.pad(k, ((0, 0), (window, 0), (0, 0), (0, 0)))
    vp = jnp.pad(v, ((0, 0), (window, 0), (0, 0), (0, 0)))
    a = jnp.arange(QB)[:, None]
    j = jnp.arange(span)[None, :]
    band = (j > a) & (j <= a + window)
    scale = HEAD_DIM ** -0.5

    def block(i):
        start = i * QB
        qb = lax.dynamic_slice_in_dim(q, start, QB, axis=1)
        kb = lax.dynamic_slice_in_dim(kp, start, span, axis=1)
        vb = lax.dynamic_slice_in_dim(vp, start, span, axis=1)
        s = jnp.einsum('bqhgd,bkhd->bhgqk', qb, kb).astype(jnp.float32) * scale
        mask = band & (start + j - window >= 0)
        s = jnp.where(mask, s, NEG)
        m = jnp.max(s, axis=-1, keepdims=True)
        if sinks is not None:
            sk = sinks.astype(jnp.float32)[None, :, :, None, None]
            m = jnp.maximum(m, sk)
            e = jnp.exp(s - m)
            den = jnp.sum(e, axis=-1, keepdims=True) + jnp.exp(sk - m)
        else:
            e = jnp.exp(s - m)
            den = jnp.sum(e, axis=-1, keepdims=True)
        p = (e / den).astype(vb.dtype)
        return jnp.einsum('bhgqk,bkhd->bqhgd', p, vb)

    o = lax.map(block, jnp.arange(S // QB))
    return o.transpose(1, 0, 2, 3, 4, 5).reshape(B, S, Hkv, G, D)


def causal_dwconv(x, w, b):
    C = x.shape[-1]
    y = lax.conv_general_dilated(
        x, w[:, None, :].astype(x.dtype), window_strides=(1,),
        padding=[(CONV_WIDTH - 1, 0)], dimension_numbers=('NWC', 'WIO', 'NWC'),
        feature_group_count=C)
    return y + b


def hybrid_layer(h, w_in, w_out, attn_pre_norm, attn_post_norm, nsa_out_norm, swa_out_norm,
                 cmp_pos, cmp_w1, cmp_w2, swa_sinks, ffn_pre_norm, ffn_post_norm,
                 w_up, conv_w, conv_b, w_down):
    B, S, _ = h.shape
    t = jnp.arange(S)
    hn = rms_norm(h, attn_pre_norm)
    proj = hn @ w_in
    (q_n, k_c, v_c, k_s, v_s, k_w, v_w, g_n, q_w, k_sw, v_sw) = jnp.split(proj, IN_SPLITS, axis=-1)
    kvh = lambda z, n: z.reshape(B, S, n, HEAD_DIM)

    qn = rope(kvh(q_n, NSA_HEADS), t).reshape(B, S, NSA_KV, NSA_GROUP, HEAD_DIM)
    k_c, k_s, k_w = (rope(kvh(z, NSA_KV), t) for z in (k_c, k_s, k_w))
    v_c, v_s, v_w = (kvh(z, NSA_KV) for z in (v_c, v_s, v_w))
    ckv = jax.vmap(compress)(jnp.stack([k_c, v_c]), cmp_pos, cmp_w1, cmp_w2)
    o_cmp, p_cmp = compressed_attention(qn, ckv[0], ckv[1], t)
    idx = select_blocks(p_cmp, t, S // SLC_BLOCK)
    o_slc = selected_attention(qn, k_s, v_s, idx, t)
    o_win = banded_attention(qn, k_w, v_w, NSA_WINDOW, None)
    gates = jax.nn.sigmoid(g_n).reshape(B, S, NSA_KV, NSA_GROUP, NSA_BRANCHES)
    o_nsa = gates[..., 0:1] * o_cmp + gates[..., 1:2] * o_slc + gates[..., 2:3] * o_win
    o_nsa = rms_norm(o_nsa.reshape(B, S, NSA_HEADS * HEAD_DIM), nsa_out_norm)

    qw = rope(kvh(q_w, SWA_HEADS), t).reshape(B, S, SWA_KV, SWA_GROUP, HEAD_DIM)
    kw = rope(kvh(k_sw, SWA_KV), t)
    o_swa = banded_attention(qw, kw, kvh(v_sw, SWA_KV), SWA_WINDOW,
                             swa_sinks.reshape(SWA_KV, SWA_GROUP))
    o_swa = rms_norm(o_swa.reshape(B, S, SWA_HEADS * HEAD_DIM), swa_out_norm)

    mixed = jnp.concatenate([o_nsa, o_swa], axis=-1) @ w_out
    h = h + rms_norm(mixed, attn_post_norm)

    hn = rms_norm(h, ffn_pre_norm)
    u = causal_dwconv(hn @ w_up, conv_w, conv_b)
    gate, up = jnp.split(u, 2, axis=-1)
    y = (jax.nn.gelu(gate, approximate=True) * up) @ w_down
    return h + rms_norm(y, ffn_post_norm)


def setup_inputs(seed: int = 0) -> dict:
    key = jax.random.key(seed)
    ks = jax.random.split(key, 17)
    f = jnp.float32
    L = DEPTH
    nrm = lambda k, shape, sc: jax.random.normal(k, shape, f) * sc
    gain = lambda k, shape: 1.0 + 0.05 * jax.random.normal(k, shape, f)
    return {
        "x": nrm(ks[0], (BATCH, SEQ, D_MODEL), 1.0),
        "w_in": nrm(ks[1], (L, D_MODEL, IN_WIDTH), D_MODEL ** -0.5),
        "w_out": nrm(ks[2], (L, MIX_WIDTH, D_MODEL), MIX_WIDTH ** -0.5),
        "attn_pre_norm": gain(ks[3], (L, D_MODEL)),
        "attn_post_norm": gain(ks[4], (L, D_MODEL)),
        "nsa_out_norm": gain(ks[5], (L, NSA_HEADS * HEAD_DIM)),
        "swa_out_norm": gain(ks[6], (L, SWA_HEADS * HEAD_DIM)),
        "cmp_pos": nrm(ks[7], (L, 2, CMP_LEN, HEAD_DIM), 0.1),
        "cmp_w1": nrm(ks[8], (L, 2, CMP_LEN * HEAD_DIM, CMP_HIDDEN), (CMP_LEN * HEAD_DIM) ** -0.5),
        "cmp_w2": nrm(ks[9], (L, 2, CMP_HIDDEN, HEAD_DIM), CMP_HIDDEN ** -0.5),
        "swa_sinks": nrm(ks[10], (L, SWA_HEADS), 0.5),
        "ffn_pre_norm": gain(ks[11], (L, D_MODEL)),
        "ffn_post_norm": gain(ks[12], (L, D_MODEL)),
        "w_up": nrm(ks[13], (L, D_MODEL, 2 * D_FF), D_MODEL ** -0.5),
        "conv_w": nrm(ks[14], (L, CONV_WIDTH, 2 * D_FF), CONV_WIDTH ** -0.5),
        "conv_b": nrm(ks[15], (L, 2 * D_FF), 0.02),
        "w_down": nrm(ks[16], (L, D_FF, D_MODEL), D_FF ** -0.5),
    }


def reference(x, w_in, w_out, attn_pre_norm, attn_post_norm, nsa_out_norm, swa_out_norm,
              cmp_pos, cmp_w1, cmp_w2, swa_sinks, ffn_pre_norm, ffn_post_norm,
              w_up, conv_w, conv_b, w_down):
    h = x
    for l in range(DEPTH):
        h = hybrid_layer(h, w_in[l], w_out[l], attn_pre_norm[l], attn_post_norm[l],
                         nsa_out_norm[l], swa_out_norm[l], cmp_pos[l], cmp_w1[l], cmp_w2[l],
                         swa_sinks[l], ffn_pre_norm[l], ffn_post_norm[l],
                         w_up[l], conv_w[l], conv_b[l], w_down[l])
    return h
```

```python
import functools

import numpy as np
import jax
import jax.numpy as jnp
from jax import lax
from jax.experimental import pallas as pl
from jax.experimental.pallas import tpu as pltpu

HEAD_DIM = 64
KV_HEADS = 2
GROUP = 4
Q_WIDTH = KV_HEADS * GROUP * HEAD_DIM
KV_WIDTH = KV_HEADS * HEAD_DIM
SWA_WINDOW = 128
NSA_WINDOW = 512
CMP_LEN = 32
CMP_STRIDE = 16
CMP_HIDDEN = 256
SLC_BLOCK = 64
SLC_TOPK = 16
NSA_BRANCHES = 3
D_FF = 2816
ROPE_THETA = 10000.0
RMS_EPS = 1e-6
NEG = -1e30
BIG = 1e9

LANES = 128
SUBLANES = 8
HALO = 8
MXU_DTYPE = jnp.bfloat16
VMEM_LIMIT = 56 * 1024 * 1024

PROJ_ROWS = 512
CMP_Q = 256
SLC_Q = 256
SLC_K = 256
BAND_Q = 128
MIX_ROWS = 512
FFN_ROWS = 512
FFN_CHUNK = 256

_N_ROPE_SLABS = 12
_N_QSLABS = 8
PROJ_WIDTH = 2 * Q_WIDTH + 9 * LANES


def _q_perm():
    n = np.arange(Q_WIDTH)
    g, h, d = n // 128, (n % 128) // 64, n % 64
    return (h * GROUP + g) * HEAD_DIM + d


def _params(*sem):
    return pltpu.CompilerParams(dimension_semantics=sem, vmem_limit_bytes=VMEM_LIMIT)


def _rms(xf, g):
    return xf * lax.rsqrt(jnp.mean(xf * xf, axis=-1, keepdims=True) + RMS_EPS) * g


def _gelu_tanh(x):
    return 0.5 * x * (1.0 + jnp.tanh(np.sqrt(2.0 / np.pi).astype(np.float32) * (x + 0.044715 * (x * x * x))))


def _dot_nt(a, b):
    return lax.dot_general(a, b, (((1,), (1,)), ((), ())), preferred_element_type=jnp.float32)


def _dot_tn(a, b):
    return lax.dot_general(a, b, (((0,), (0,)), ((), ())), preferred_element_type=jnp.float32)


def _split_heads(q):
    qs = jnp.concatenate([q[:, g * LANES:(g + 1) * LANES] for g in range(GROUP)], axis=0)
    lane = lax.broadcasted_iota(jnp.int32, qs.shape, 1)
    zero = jnp.zeros_like(qs)
    return [jnp.where((lane >= h * HEAD_DIM) & (lane < (h + 1) * HEAD_DIM), qs, zero) for h in range(KV_HEADS)]


def _merge_heads(o_t, tq):
    sub = lax.broadcasted_iota(jnp.int32, o_t[0].shape, 0)
    both = jnp.where(sub < HEAD_DIM, o_t[0], o_t[1])
    return jnp.concatenate([both[:, g * tq:(g + 1) * tq].T for g in range(GROUP)], axis=1)


def _proj_kernel(x_ref, g_ref, w_ref, cos_ref, sin_ref, qn_ref, qw_ref, kc_ref, vc_ref, kv_ref, gl_ref):
    hn = _rms(x_ref[0], g_ref[...]).astype(MXU_DTYPE)
    p = jnp.dot(hn, w_ref[...], preferred_element_type=jnp.float32)
    cos, sin = cos_ref[...], sin_ref[...]
    lane = lax.broadcasted_iota(jnp.int32, cos.shape, 1)
    first_half = (lane % HEAD_DIM) < (HEAD_DIM // 2)

    def slab(j, rope):
        z = p[:, j * LANES:(j + 1) * LANES]
        if rope:
            swapped = jnp.where(first_half, pltpu.roll(z, LANES - HEAD_DIM // 2, 1), pltpu.roll(z, HEAD_DIM // 2, 1))
            z = z * cos + swapped * sin
        return z

    scale = HEAD_DIM ** -0.5
    for j in range(4):
        qn_ref[0, :, j * LANES:(j + 1) * LANES] = (slab(j, True) * scale).astype(qn_ref.dtype)
        qw_ref[0, :, j * LANES:(j + 1) * LANES] = (slab(4 + j, True) * scale).astype(qw_ref.dtype)
    kc_ref[0] = slab(8, True)
    vc_ref[0] = slab(12, False)
    for j in range(3):
        kv_ref[0, :, j * LANES:(j + 1) * LANES] = slab(9 + j, True).astype(kv_ref.dtype)
    for j in range(3):
        kv_ref[0, :, (3 + j) * LANES:(4 + j) * LANES] = slab(13 + j, False).astype(kv_ref.dtype)
    gl_ref[0] = slab(16, False)


def _proj(x, gain, w, cos, sin):
    B, S, D = x.shape
    ts = min(PROJ_ROWS, S)
    row = lambda width: pl.BlockSpec((1, ts, width), lambda b, i: (b, i, 0))
    full = lambda a: pl.BlockSpec(a.shape, lambda b, i: (0,) * a.ndim)
    tab = pl.BlockSpec((ts, LANES), lambda b, i: (i, 0))
    f32, bf = jnp.float32, MXU_DTYPE
    return pl.pallas_call(
        _proj_kernel,
        grid=(B, S // ts),
        in_specs=[row(D), full(gain), full(w), tab, tab],
        out_specs=[row(Q_WIDTH), row(Q_WIDTH), row(LANES), row(LANES), row(6 * LANES), row(LANES)],
        out_shape=[jax.ShapeDtypeStruct((B, S, Q_WIDTH), bf), jax.ShapeDtypeStruct((B, S, Q_WIDTH), bf),
                   jax.ShapeDtypeStruct((B, S, LANES), f32), jax.ShapeDtypeStruct((B, S, LANES), f32),
                   jax.ShapeDtypeStruct((B, S, 6 * LANES), bf), jax.ShapeDtypeStruct((B, S, LANES), f32)],
        name="proj",
        compiler_params=_params("parallel", "parallel"),
    )(x, gain, w, cos, sin)


def _compress_kernel(kc_ref, vc_ref, pe_ref, w1_ref, w2_ref, ck_ref, cv_ref):
    for kv, (src, dst) in enumerate(((kc_ref, ck_ref), (vc_ref, cv_ref))):
        ch = src[0]
        top = jnp.dot((ch + pe_ref[kv, 0]).astype(MXU_DTYPE), w1_ref[kv, 0], preferred_element_type=jnp.float32)
        bot = jnp.dot((ch + pe_ref[kv, 1]).astype(MXU_DTYPE), w1_ref[kv, 1], preferred_element_type=jnp.float32)
        n = ch.shape[0]
        hid = top + pltpu.roll(bot, n - 1, 0)
        act = _gelu_tanh(hid).astype(MXU_DTYPE)
        dst[0] = jnp.dot(act, w2_ref[kv], preferred_element_type=jnp.float32).astype(dst.dtype)


def _compress(kc_r, vc_r, pe, w1, w2):
    B, n, width = kc_r.shape
    src = pl.BlockSpec((1, n, width), lambda b: (b, 0, 0))
    full = lambda a: pl.BlockSpec(a.shape, lambda b: (0,) * a.ndim)
    dst = pl.BlockSpec((1, n, KV_WIDTH), lambda b: (b, 0, 0))
    return pl.pallas_call(
        _compress_kernel,
        grid=(B,),
        in_specs=[src, src, full(pe), full(w1), full(w2)],
        out_specs=[dst, dst],
        out_shape=[jax.ShapeDtypeStruct((B, n, KV_WIDTH), MXU_DTYPE)] * 2,
        name="compress",
        compiler_params=_params("parallel"),
    )(kc_r, vc_r, pe, w1, w2)


def _cmp_kernel(q_ref, ck_ref, cv_ref, o_ref, sel_ref, *, tq, nsb):
    q0 = pl.program_id(1) * tq
    ck, cv = ck_ref[0], cv_ref[0]
    ncp = ck.shape[0]
    qh = _split_heads(q_ref[0])
    c = lax.broadcasted_iota(jnp.int32, (ncp, tq), 0)
    t = q0 + lax.broadcasted_iota(jnp.int32, (ncp, tq), 1)
    valid = c * CMP_STRIDE + (CMP_LEN - 1) <= t

    jo = lax.broadcasted_iota(jnp.int32, (nsb, ncp), 0)
    co = lax.broadcasted_iota(jnp.int32, (nsb, ncp), 1)
    overlap = ((co * CMP_STRIDE <= jo * SLC_BLOCK + SLC_BLOCK - 1)
               & (co * CMP_STRIDE + CMP_LEN - 1 >= jo * SLC_BLOCK) & (co < ncp - 1)).astype(MXU_DTYPE)

    jb = lax.broadcasted_iota(jnp.int32, (nsb, tq), 0)
    tb = q0 + lax.broadcasted_iota(jnp.int32, (nsb, tq), 1)
    blk = tb // SLC_BLOCK
    forced = (jb == 0) | (jb == blk) | (jb == blk - 1)
    causal = jb * SLC_BLOCK <= tb

    o_t, sel_t = [], []
    for h in range(KV_HEADS):
        s = _dot_nt(ck, qh[h])
        s = jnp.concatenate([jnp.where(valid, s[:, g * tq:(g + 1) * tq], NEG) for g in range(GROUP)], axis=1)
        m = jnp.max(s, axis=0, keepdims=True)
        e = jnp.exp(s - m)
        e = jnp.concatenate([jnp.where(valid, e[:, g * tq:(g + 1) * tq], 0.0) for g in range(GROUP)], axis=1)
        p = e * (1.0 / jnp.maximum(jnp.sum(e, axis=0, keepdims=True), 1e-30))
        o_t.append(_dot_tn(cv, p.astype(MXU_DTYPE)))

        pg = p[:, 0:tq] + p[:, tq:2 * tq] + p[:, 2 * tq:3 * tq] + p[:, 3 * tq:4 * tq]
        hi = pg.astype(MXU_DTYPE)
        lo = (pg - hi.astype(jnp.float32)).astype(MXU_DTYPE)
        imp = (jnp.dot(overlap, hi, preferred_element_type=jnp.float32)
               + jnp.dot(overlap, lo, preferred_element_type=jnp.float32))
        score = jnp.where(forced, BIG, jnp.where(causal, imp, NEG))
        groups = [score[SUBLANES * k:SUBLANES * (k + 1)] for k in range(nsb // SUBLANES)]
        rank = [jnp.zeros((SUBLANES, tq), jnp.int32) for _ in groups]
        sub = lax.broadcasted_iota(jnp.int32, (SUBLANES, tq), 0)
        for i in range(nsb):
            row = score[i:i + 1, :]
            for k, blk in enumerate(groups):
                if SUBLANES * k > i:
                    beats = jnp.where(row >= blk, 1, 0)
                elif SUBLANES * k + SUBLANES - 1 <= i:
                    beats = jnp.where(row > blk, 1, 0)
                else:
                    beats = jnp.where(sub > i - SUBLANES * k, jnp.where(row >= blk, 1, 0), jnp.where(row > blk, 1, 0))
                rank[k] = rank[k] + beats
        sel_t.append(jnp.where(jnp.concatenate(rank, axis=0) < min(SLC_TOPK, nsb), 1.0, 0.0))

    o_ref[0] = _merge_heads(o_t, tq)
    sel_ref[0] = jnp.concatenate(sel_t, axis=0).astype(sel_ref.dtype)


def _cmp_attention(qn, ck, cv):
    B, S, _ = qn.shape
    tq = min(CMP_Q, S)
    nsb = S // SLC_BLOCK
    ncp = ck.shape[1]
    return pl.pallas_call(
        functools.partial(_cmp_kernel, tq=tq, nsb=nsb),
        grid=(B, S // tq),
        in_specs=[pl.BlockSpec((1, tq, Q_WIDTH), lambda b, i: (b, i, 0)),
                  pl.BlockSpec((1, ncp, KV_WIDTH), lambda b, i: (b, 0, 0)),
                  pl.BlockSpec((1, ncp, KV_WIDTH), lambda b, i: (b, 0, 0))],
        out_specs=[pl.BlockSpec((1, tq, Q_WIDTH), lambda b, i: (b, i, 0)),
                   pl.BlockSpec((1, KV_HEADS * nsb, tq), lambda b, i: (b, 0, i))],
        out_shape=[jax.ShapeDtypeStruct((B, S, Q_WIDTH), jnp.float32),
                   jax.ShapeDtypeStruct((B, KV_HEADS * nsb, S), MXU_DTYPE)],
        name="cmp_select",
        compiler_params=_params("parallel", "parallel"),
    )(qn, ck, cv)


def _slc_kernel(q_ref, k_ref, v_ref, sel_ref, o_ref, m_scr, l_scr, acc_scr, *, tq, tk, nsb):
    qi = pl.program_id(1)
    q0 = qi * tq
    qh = _split_heads(q_ref[0])
    sel = sel_ref[0]
    m_scr[...] = jnp.full(m_scr.shape, NEG, jnp.float32)
    l_scr[...] = jnp.zeros(l_scr.shape, jnp.float32)
    acc_scr[...] = jnp.zeros(acc_scr.shape, jnp.float32)
    qpos = q0 + lax.broadcasted_iota(jnp.int32, (tk, tq), 1)
    koff = lax.broadcasted_iota(jnp.int32, (tk, tq), 0)
    erow = lax.broadcasted_iota(jnp.int32, (tk, KV_HEADS * nsb), 0) // SLC_BLOCK
    ecol = lax.broadcasted_iota(jnp.int32, (tk, KV_HEADS * nsb), 1)

    def step(kt, carry):
        k0 = pl.multiple_of(kt * tk, tk)
        k = k_ref[0, pl.ds(k0, tk), :]
        v = v_ref[0, pl.ds(k0, tk), :]
        causal = (k0 + koff) <= qpos
        for h in range(KV_HEADS):
            onehot = (ecol == erow + (kt * (tk // SLC_BLOCK) + h * nsb)).astype(MXU_DTYPE)
            keep = (jnp.dot(onehot, sel, preferred_element_type=jnp.float32) > 0.5) & causal
            s = _dot_nt(k, qh[h])
            s = jnp.concatenate([jnp.where(keep, s[:, g * tq:(g + 1) * tq], NEG) for g in range(GROUP)], axis=1)
            m_prev = m_scr[h]
            m_new = jnp.maximum(m_prev, jnp.max(s, axis=0, keepdims=True))
            alpha = jnp.exp(m_prev - m_new)
            p = jnp.exp(s - m_new)
            l_scr[h] = alpha * l_scr[h] + jnp.sum(p, axis=0, keepdims=True)
            acc_scr[h] = alpha * acc_scr[h] + _dot_tn(v, p.astype(MXU_DTYPE))
            m_scr[h] = m_new
        return carry

    lax.fori_loop(0, (q0 + tq + tk - 1) // tk, step, 0)
    o_ref[0] = _merge_heads([acc_scr[h] / l_scr[h] for h in range(KV_HEADS)], tq)


def _slc_attention(qn, kv, sel):
    B, S, _ = qn.shape
    tq, tk = min(SLC_Q, S), min(SLC_K, S)
    nsb = S // SLC_BLOCK
    return pl.pallas_call(
        functools.partial(_slc_kernel, tq=tq, tk=tk, nsb=nsb),
        grid=(B, S // tq),
        in_specs=[pl.BlockSpec((1, tq, Q_WIDTH), lambda b, i: (b, i, 0)),
                  pl.BlockSpec((1, S, KV_WIDTH), lambda b, i: (b, 0, 0)),
                  pl.BlockSpec((1, S, KV_WIDTH), lambda b, i: (b, 0, 3)),
                  pl.BlockSpec((1, KV_HEADS * nsb, tq), lambda b, i: (b, 0, i))],
        out_specs=pl.BlockSpec((1, tq, Q_WIDTH), lambda b, i: (b, i, 0)),
        out_shape=jax.ShapeDtypeStruct((B, S, Q_WIDTH), jnp.float32),
        scratch_shapes=[pltpu.VMEM((KV_HEADS, 1, GROUP * tq), jnp.float32),
                        pltpu.VMEM((KV_HEADS, 1, GROUP * tq), jnp.float32),
                        pltpu.VMEM((KV_HEADS, KV_WIDTH, GROUP * tq), jnp.float32)],
        name="slc",
        compiler_params=_params("parallel", "parallel"),
    )(qn, kv, kv, sel)


def _band_kernel(*refs, tq, window, sinks):
    if sinks:
        sink_ref, q_ref, k_ref, v_ref, o_ref = refs
    else:
        q_ref, k_ref, v_ref, o_ref = refs
    q0 = pl.program_id(1) * tq
    span = window + tq
    start = pl.multiple_of(jnp.maximum(q0 - window, 0), LANES)
    k = k_ref[0, pl.ds(start, span), :]
    v = v_ref[0, pl.ds(start, span), :]
    qh = _split_heads(q_ref[0])
    kpos = start + lax.broadcasted_iota(jnp.int32, (span, tq), 0)
    qpos = q0 + lax.broadcasted_iota(jnp.int32, (span, tq), 1)
    keep = (kpos <= qpos) & (kpos > qpos - window)
    o_t = []
    for h in range(KV_HEADS):
        s = _dot_nt(k, qh[h])
        s = jnp.concatenate([jnp.where(keep, s[:, g * tq:(g + 1) * tq], NEG) for g in range(GROUP)], axis=1)
        m = jnp.max(s, axis=0, keepdims=True)
        if sinks:
            sk = jnp.concatenate([jnp.full((1, tq), sink_ref[h * GROUP + g], jnp.float32) for g in range(GROUP)], axis=1)
            m = jnp.maximum(m, sk)
        e = jnp.exp(s - m)
        den = jnp.sum(e, axis=0, keepdims=True)
        if sinks:
            den = den + jnp.exp(sk - m)
        o_t.append(_dot_tn(v, e.astype(MXU_DTYPE)) / den)
    o_ref[0] = _merge_heads(o_t, tq)


def _band_attention(q, kv, k_col, v_col, window, sinks):
    B, S, _ = q.shape
    tq = min(BAND_Q, S)
    in_specs = [pl.BlockSpec((1, tq, Q_WIDTH), lambda b, i: (b, i, 0)),
                pl.BlockSpec((1, S, KV_WIDTH), lambda b, i: (b, 0, k_col)),
                pl.BlockSpec((1, S, KV_WIDTH), lambda b, i: (b, 0, v_col))]
    args = [q, kv, kv]
    if sinks is not None:
        in_specs = [pl.BlockSpec(memory_space=pltpu.SMEM)] + in_specs
        args = [sinks] + args
    return pl.pallas_call(
        functools.partial(_band_kernel, tq=tq, window=window, sinks=sinks is not None),
        grid=(B, S // tq),
        in_specs=in_specs,
        out_specs=pl.BlockSpec((1, tq, Q_WIDTH), lambda b, i: (b, i, 0)),
        out_shape=jax.ShapeDtypeStruct((B, S, Q_WIDTH), jnp.float32),
        name="band_sink" if sinks is not None else "band",
        compiler_params=_params("parallel", "parallel"),
    )(*args)


def _mix_kernel(x_ref, oc_ref, os_ref, ow_ref, osw_ref, gl_ref, ge_ref, w_ref, gn_ref, gs_ref, gp_ref, h_ref):
    sig = 1.0 / (1.0 + jnp.exp(-gl_ref[0]))
    hi = sig.astype(MXU_DTYPE)
    lo = (sig - hi.astype(jnp.float32)).astype(MXU_DTYPE)
    gates = (jnp.dot(hi, ge_ref[...], preferred_element_type=jnp.float32)
             + jnp.dot(lo, ge_ref[...], preferred_element_type=jnp.float32))
    o_nsa = (gates[:, 0:Q_WIDTH] * oc_ref[0] + gates[:, Q_WIDTH:2 * Q_WIDTH] * os_ref[0]
             + gates[:, 2 * Q_WIDTH:3 * Q_WIDTH] * ow_ref[0])
    cat = jnp.concatenate([_rms(o_nsa, gn_ref[...]), _rms(osw_ref[0], gs_ref[...])], axis=1).astype(MXU_DTYPE)
    mixed = jnp.dot(cat, w_ref[...], preferred_element_type=jnp.float32)
    h_ref[0] = x_ref[0] + _rms(mixed, gp_ref[...])


def _mix(x, o_cmp, o_slc, o_win, o_swa, gl, gate_expand, w_out, g_nsa, g_swa, g_post):
    B, S, D = x.shape
    ts = min(MIX_ROWS, S)
    row = lambda width: pl.BlockSpec((1, ts, width), lambda b, i: (b, i, 0))
    full = lambda a: pl.BlockSpec(a.shape, lambda b, i: (0,) * a.ndim)
    return pl.pallas_call(
        _mix_kernel,
        grid=(B, S // ts),
        in_specs=[row(D), row(Q_WIDTH), row(Q_WIDTH), row(Q_WIDTH), row(Q_WIDTH), row(LANES),
                  full(gate_expand), full(w_out), full(g_nsa), full(g_swa), full(g_post)],
        out_specs=row(D),
        out_shape=jax.ShapeDtypeStruct((B, S, D), jnp.float32),
        name="mix",
        compiler_params=_params("parallel", "parallel"),
    )(x, o_cmp, o_slc, o_win, o_swa, gl, gate_expand, w_out, g_nsa, g_swa, g_post)


def _ffn_kernel(h_ref, halo_ref, gpre_ref, wup_ref, cw_ref, cb_ref, wdn_ref, gpost_ref, y_ref, *, ts, nchunk):
    h = h_ref[0]
    live = (pl.program_id(1) > 0).astype(jnp.float32)
    hn = jnp.concatenate([_rms(halo_ref[0], gpre_ref[...]) * live, _rms(h, gpre_ref[...])], axis=0).astype(MXU_DTYPE)

    def conv(j):
        u = jnp.dot(hn, wup_ref[j], preferred_element_type=jnp.float32)
        w = cw_ref[j]
        return (w[0:1] * u[HALO - 2:HALO - 2 + ts] + w[1:2] * u[HALO - 1:HALO - 1 + ts]
                + w[2:3] * u[HALO:HALO + ts] + cb_ref[j])

    def chunk(j, acc):
        act = (_gelu_tanh(conv(j)) * conv(nchunk + j)).astype(MXU_DTYPE)
        return acc + jnp.dot(act, wdn_ref[j], preferred_element_type=jnp.float32)

    y = lax.fori_loop(0, nchunk, chunk, jnp.zeros(h.shape, jnp.float32))
    y_ref[0] = h + _rms(y, gpost_ref[...])


def _ffn(h, g_pre, w_up, conv_w, conv_b, w_down, g_post):
    B, S, D = h.shape
    ts = min(FFN_ROWS, S)
    nchunk = w_down.shape[0]
    full = lambda a: pl.BlockSpec(a.shape, lambda b, i: (0,) * a.ndim)
    return pl.pallas_call(
        functools.partial(_ffn_kernel, ts=ts, nchunk=nchunk),
        grid=(B, S // ts),
        in_specs=[pl.BlockSpec((1, ts, D), lambda b, i: (b, i, 0)),
                  pl.BlockSpec((1, HALO, D), lambda b, i: (b, jnp.maximum(i * (ts // HALO) - 1, 0), 0)),
                  full(g_pre), full(w_up), full(conv_w), full(conv_b), full(w_down), full(g_post)],
        out_specs=pl.BlockSpec((1, ts, D), lambda b, i: (b, i, 0)),
        out_shape=jax.ShapeDtypeStruct((B, S, D), jnp.float32),
        name="ffn",
        compiler_params=_params("parallel", "parallel"),
    )(h, h, g_pre, w_up, conv_w, conv_b, w_down, g_post)


def _rope_tables(S):
    half = HEAD_DIM // 2
    inv = ROPE_THETA ** (-jnp.arange(half, dtype=jnp.float32) / half)
    ang = jnp.arange(S).astype(jnp.float32)[:, None] * inv[None, :]
    cos, sin = jnp.cos(ang), jnp.sin(ang)
    reps = LANES // HEAD_DIM
    return jnp.tile(jnp.concatenate([cos, cos], axis=1), (1, reps)), jnp.tile(jnp.concatenate([-sin, sin], axis=1), (1, reps))


def _layer(x, w_in, w_out, attn_pre_norm, attn_post_norm, nsa_out_norm, swa_out_norm, cmp_pos, cmp_w1, cmp_w2,
           swa_sinks, ffn_pre_norm, ffn_post_norm, w_up, conv_w, conv_b, w_down):
    B, S, D = x.shape
    f32 = jnp.float32
    perm = _q_perm()
    sizes = [Q_WIDTH] + [KV_WIDTH] * 6 + [KV_HEADS * GROUP * NSA_BRANCHES] + [Q_WIDTH, KV_WIDTH, KV_WIDTH]
    offs = np.concatenate([[0], np.cumsum(sizes)])
    col = lambda i: w_in[:, offs[i]:offs[i + 1]]
    q_n, k_c, v_c, k_s, v_s, k_w, v_w, g_n, q_w, k_sw, v_sw = (col(i) for i in range(11))
    g_pad = jnp.pad(g_n, ((0, 0), (0, LANES - g_n.shape[1])))
    w_all = jnp.concatenate([q_n[:, perm], q_w[:, perm], k_c, k_s, k_w, k_sw, v_c, v_s, v_w, v_sw, g_pad],
                            axis=1).astype(MXU_DTYPE)
    cos, sin = _rope_tables(S)
    row = lambda g: g.reshape(1, -1).astype(f32)

    qn, qw, kc, vc, kv, gl = _proj(x, row(attn_pre_norm), w_all, cos, sin)

    eye = jnp.eye(KV_HEADS, dtype=f32)
    pe = jnp.broadcast_to(cmp_pos.reshape(2, 2, CMP_STRIDE, 1, HEAD_DIM), (2, 2, CMP_STRIDE, KV_HEADS, HEAD_DIM))
    pe = pe.reshape(2, 2, 1, CMP_STRIDE * KV_WIDTH)
    w1 = cmp_w1.reshape(2, 2, CMP_STRIDE, HEAD_DIM, CMP_HIDDEN)
    w1x = jnp.einsum('kapdm,hg->kaphdgm', w1, eye).reshape(2, 2, CMP_STRIDE * KV_WIDTH, KV_HEADS * CMP_HIDDEN)
    w2x = jnp.einsum('kmd,hg->khmgd', cmp_w2, eye).reshape(2, KV_HEADS * CMP_HIDDEN, KV_WIDTH)
    nchunks = S // CMP_STRIDE
    ck, cv = _compress(kc.reshape(B, nchunks, CMP_STRIDE * KV_WIDTH), vc.reshape(B, nchunks, CMP_STRIDE * KV_WIDTH),
                       pe, w1x.astype(MXU_DTYPE), w2x.astype(MXU_DTYPE))

    o_cmp, sel = _cmp_attention(qn, ck, cv)
    o_slc = _slc_attention(qn, kv, sel)
    o_win = _band_attention(qn, kv, 1, 4, NSA_WINDOW, None)
    o_swa = _band_attention(qw, kv, 2, 5, SWA_WINDOW, swa_sinks.astype(f32))

    n = np.arange(NSA_BRANCHES * Q_WIDTH)
    br, g, h = n // Q_WIDTH, (n % Q_WIDTH) // LANES, (n % LANES) // HEAD_DIM
    expand = np.zeros((LANES, NSA_BRANCHES * Q_WIDTH), np.float32)
    expand[(h * GROUP + g) * NSA_BRANCHES + br, n] = 1.0
    w_out_p = jnp.concatenate([w_out[perm], w_out[Q_WIDTH + perm]], axis=0).astype(MXU_DTYPE)
    hmid = _mix(x, o_cmp, o_slc, o_win, o_swa, gl, jnp.asarray(expand, MXU_DTYPE), w_out_p,
                row(nsa_out_norm[perm]), row(swa_out_norm[perm]), row(attn_post_norm))

    nchunk = D_FF // FFN_CHUNK
    w_up_c = w_up.reshape(D, 2 * nchunk, FFN_CHUNK).transpose(1, 0, 2).astype(MXU_DTYPE)
    cw_c = conv_w.reshape(-1, 2 * nchunk, FFN_CHUNK).transpose(1, 0, 2).astype(f32)
    cb_c = conv_b.reshape(2 * nchunk, 1, FFN_CHUNK).astype(f32)
    w_dn_c = w_down.reshape(nchunk, FFN_CHUNK, D).astype(MXU_DTYPE)
    return _ffn(hmid, row(ffn_pre_norm), w_up_c, cw_c, cb_c, w_dn_c, row(ffn_post_norm))


def kernel(x, w_in, w_out, attn_pre_norm, attn_post_norm, nsa_out_norm, swa_out_norm, cmp_pos, cmp_w1, cmp_w2,
           swa_sinks, ffn_pre_norm, ffn_post_norm, w_up, conv_w, conv_b, w_down):
    h = x
    for l in range(w_in.shape[0]):
        h = _layer(h, w_in[l], w_out[l], attn_pre_norm[l], attn_post_norm[l], nsa_out_norm[l], swa_out_norm[l],
                   cmp_pos[l], cmp_w1[l], cmp_w2[l], swa_sinks[l], ffn_pre_norm[l], ffn_post_norm[l],
                   w_up[l], conv_w[l], conv_b[l], w_down[l])
    return h
```

```python
import functools

import numpy as np
import jax
import jax.numpy as jnp
from jax import lax
from jax.experimental import pallas as pl
from jax.experimental.pallas import tpu as pltpu

HEAD_DIM = 64
KV_HEADS = 2
GROUP = 4
Q_WIDTH = KV_HEADS * GROUP * HEAD_DIM
KV_WIDTH = KV_HEADS * HEAD_DIM
SWA_WINDOW = 128
NSA_WINDOW = 512
CMP_LEN = 32
CMP_STRIDE = 16
CMP_HIDDEN = 256
SLC_BLOCK = 64
SLC_TOPK = 16
NSA_BRANCHES = 3
D_FF = 2816
ROPE_THETA = 10000.0
RMS_EPS = 1e-6
NEG = -1e30
BIG = 1e9
LOG2E = float(np.log2(np.e))

LANES = 128
SUBLANES = 8
HALO = 8
MXU_DTYPE = jnp.bfloat16
VMEM_LIMIT = 56 * 1024 * 1024

PROJ_ROWS = 512
CMP_Q = 256
SLC_Q = 256
SLC_K = 256
BAND_Q = 128
MIX_ROWS = 512
FFN_ROWS = 512
FFN_CHUNK = 256

_N_ROPE_SLABS = 12
_N_QSLABS = 8
PROJ_WIDTH = 2 * Q_WIDTH + 9 * LANES


def _q_perm():
    n = np.arange(Q_WIDTH)
    g, h, d = n // 128, (n % 128) // 64, n % 64
    return (h * GROUP + g) * HEAD_DIM + d


def _params(*sem):
    return pltpu.CompilerParams(dimension_semantics=sem, vmem_limit_bytes=VMEM_LIMIT)


def _rms(xf, g):
    return xf * lax.rsqrt(jnp.mean(xf * xf, axis=-1, keepdims=True) + RMS_EPS) * g


def _gelu_tanh(x):
    return 0.5 * x * (1.0 + jnp.tanh(np.sqrt(2.0 / np.pi).astype(np.float32) * (x + 0.044715 * (x * x * x))))


def _dot_nt(a, b):
    return lax.dot_general(a, b, (((1,), (1,)), ((), ())), preferred_element_type=jnp.float32)


def _dot_tn(a, b):
    return lax.dot_general(a, b, (((0,), (0,)), ((), ())), preferred_element_type=jnp.float32)


def _split_heads(q):
    qs = jnp.concatenate([q[:, g * LANES:(g + 1) * LANES] for g in range(GROUP)], axis=0)
    lane = lax.broadcasted_iota(jnp.int32, qs.shape, 1)
    zero = jnp.zeros_like(qs)
    return [jnp.where((lane >= h * HEAD_DIM) & (lane < (h + 1) * HEAD_DIM), qs, zero) for h in range(KV_HEADS)]


def _merge_heads(o_t, tq):
    sub = lax.broadcasted_iota(jnp.int32, o_t[0].shape, 0)
    both = jnp.where(sub < HEAD_DIM, o_t[0], o_t[1])
    return jnp.concatenate([both[:, g * tq:(g + 1) * tq].T for g in range(GROUP)], axis=1)


def _proj_kernel(x_ref, g_ref, w_ref, cos_ref, sin_ref, qn_ref, qw_ref, kc_ref, vc_ref, kv_ref, gl_ref):
    hn = _rms(x_ref[0], g_ref[...]).astype(MXU_DTYPE)
    p = jnp.dot(hn, w_ref[...], preferred_element_type=jnp.float32)
    cos, sin = cos_ref[...], sin_ref[...]
    lane = lax.broadcasted_iota(jnp.int32, cos.shape, 1)
    first_half = (lane % HEAD_DIM) < (HEAD_DIM // 2)

    def slab(j, rope):
        z = p[:, j * LANES:(j + 1) * LANES]
        if rope:
            swapped = jnp.where(first_half, pltpu.roll(z, LANES - HEAD_DIM // 2, 1), pltpu.roll(z, HEAD_DIM // 2, 1))
            z = z * cos + swapped * sin
        return z

    scale = HEAD_DIM ** -0.5 * LOG2E
    for j in range(4):
        qn_ref[0, :, j * LANES:(j + 1) * LANES] = (slab(j, True) * scale).astype(qn_ref.dtype)
        qw_ref[0, :, j * LANES:(j + 1) * LANES] = (slab(4 + j, True) * scale).astype(qw_ref.dtype)
    kc_ref[0] = slab(8, True)
    vc_ref[0] = slab(12, False)
    for j in range(3):
        kv_ref[0, :, j * LANES:(j + 1) * LANES] = slab(9 + j, True).astype(kv_ref.dtype)
    for j in range(3):
        kv_ref[0, :, (3 + j) * LANES:(4 + j) * LANES] = slab(13 + j, False).astype(kv_ref.dtype)
    gl_ref[0] = slab(16, False)


def _proj(x, gain, w, cos, sin):
    B, S, D = x.shape
    ts = min(PROJ_ROWS, S)
    row = lambda width: pl.BlockSpec((1, ts, width), lambda b, i: (b, i, 0))
    full = lambda a: pl.BlockSpec(a.shape, lambda b, i: (0,) * a.ndim)
    tab = pl.BlockSpec((ts, LANES), lambda b, i: (i, 0))
    f32, bf = jnp.float32, MXU_DTYPE
    return pl.pallas_call(
        _proj_kernel,
        grid=(B, S // ts),
        in_specs=[row(D), full(gain), full(w), tab, tab],
        out_specs=[row(Q_WIDTH), row(Q_WIDTH), row(LANES), row(LANES), row(6 * LANES), row(LANES)],
        out_shape=[jax.ShapeDtypeStruct((B, S, Q_WIDTH), bf), jax.ShapeDtypeStruct((B, S, Q_WIDTH), bf),
                   jax.ShapeDtypeStruct((B, S, LANES), f32), jax.ShapeDtypeStruct((B, S, LANES), f32),
                   jax.ShapeDtypeStruct((B, S, 6 * LANES), bf), jax.ShapeDtypeStruct((B, S, LANES), f32)],
        name="proj",
        compiler_params=_params("parallel", "parallel"),
    )(x, gain, w, cos, sin)


def _compress_kernel(kc_ref, vc_ref, pe_ref, w1_ref, w2_ref, ck_ref, cv_ref):
    for kv, (src, dst) in enumerate(((kc_ref, ck_ref), (vc_ref, cv_ref))):
        ch = src[0]
        top = jnp.dot((ch + pe_ref[kv, 0]).astype(MXU_DTYPE), w1_ref[kv, 0], preferred_element_type=jnp.float32)
        bot = jnp.dot((ch + pe_ref[kv, 1]).astype(MXU_DTYPE), w1_ref[kv, 1], preferred_element_type=jnp.float32)
        n = ch.shape[0]
        hid = top + pltpu.roll(bot, n - 1, 0)
        act = _gelu_tanh(hid).astype(MXU_DTYPE)
        dst[0] = jnp.dot(act, w2_ref[kv], preferred_element_type=jnp.float32).astype(dst.dtype)


def _compress(kc_r, vc_r, pe, w1, w2):
    B, n, width = kc_r.shape
    src = pl.BlockSpec((1, n, width), lambda b: (b, 0, 0))
    full = lambda a: pl.BlockSpec(a.shape, lambda b: (0,) * a.ndim)
    dst = pl.BlockSpec((1, n, KV_WIDTH), lambda b: (b, 0, 0))
    return pl.pallas_call(
        _compress_kernel,
        grid=(B,),
        in_specs=[src, src, full(pe), full(w1), full(w2)],
        out_specs=[dst, dst],
        out_shape=[jax.ShapeDtypeStruct((B, n, KV_WIDTH), MXU_DTYPE)] * 2,
        name="compress",
        compiler_params=_params("parallel"),
    )(kc_r, vc_r, pe, w1, w2)


def _cmp_kernel(q_ref, ck_ref, cv_ref, o_ref, sel_ref, *, tq, nsb):
    q0 = pl.program_id(1) * tq
    ck, cv = ck_ref[0], cv_ref[0]
    ncp = ck.shape[0]
    qh = _split_heads(q_ref[0])
    c = lax.broadcasted_iota(jnp.int32, (ncp, tq), 0)
    t = q0 + lax.broadcasted_iota(jnp.int32, (ncp, tq), 1)
    valid = c * CMP_STRIDE + (CMP_LEN - 1) <= t

    jo = lax.broadcasted_iota(jnp.int32, (nsb, ncp), 0)
    co = lax.broadcasted_iota(jnp.int32, (nsb, ncp), 1)
    overlap = ((co * CMP_STRIDE <= jo * SLC_BLOCK + SLC_BLOCK - 1)
               & (co * CMP_STRIDE + CMP_LEN - 1 >= jo * SLC_BLOCK) & (co < ncp - 1)).astype(MXU_DTYPE)

    jb = lax.broadcasted_iota(jnp.int32, (nsb, tq), 0)
    tb = q0 + lax.broadcasted_iota(jnp.int32, (nsb, tq), 1)
    blk = tb // SLC_BLOCK
    forced = (jb == 0) | (jb == blk) | (jb == blk - 1)
    causal = jb * SLC_BLOCK <= tb

    o_t, sel_t = [], []
    for h in range(KV_HEADS):
        s = _dot_nt(ck, qh[h])
        s = jnp.concatenate([jnp.where(valid, s[:, g * tq:(g + 1) * tq], NEG) for g in range(GROUP)], axis=1)
        m = jnp.max(s, axis=0, keepdims=True)
        e = jnp.exp2(s - m)
        e = jnp.concatenate([jnp.where(valid, e[:, g * tq:(g + 1) * tq], 0.0) for g in range(GROUP)], axis=1)
        p = e * (1.0 / jnp.maximum(jnp.sum(e, axis=0, keepdims=True), 1e-30))
        o_t.append(_dot_tn(cv, p.astype(MXU_DTYPE)))

        pg = p[:, 0:tq] + p[:, tq:2 * tq] + p[:, 2 * tq:3 * tq] + p[:, 3 * tq:4 * tq]
        hi = pg.astype(MXU_DTYPE)
        lo = (pg - hi.astype(jnp.float32)).astype(MXU_DTYPE)
        imp = (jnp.dot(overlap, hi, preferred_element_type=jnp.float32)
               + jnp.dot(overlap, lo, preferred_element_type=jnp.float32))
        score = jnp.where(forced, BIG, jnp.where(causal, imp, NEG))
        groups = [score[SUBLANES * k:SUBLANES * (k + 1)] for k in range(nsb // SUBLANES)]
        rank = [jnp.zeros((SUBLANES, tq), jnp.int32) for _ in groups]
        sub = lax.broadcasted_iota(jnp.int32, (SUBLANES, tq), 0)
        for i in range(nsb):
            row = score[i:i + 1, :]
            for k, blk in enumerate(groups):
                if SUBLANES * k > i:
                    beats = jnp.where(row >= blk, 1, 0)
                elif SUBLANES * k + SUBLANES - 1 <= i:
                    beats = jnp.where(row > blk, 1, 0)
                else:
                    beats = jnp.where(sub > i - SUBLANES * k, jnp.where(row >= blk, 1, 0), jnp.where(row > blk, 1, 0))
                rank[k] = rank[k] + beats
        sel_t.append(jnp.where(jnp.concatenate(rank, axis=0) < min(SLC_TOPK, nsb), 0.0, NEG))

    o_ref[0] = _merge_heads(o_t, tq)
    pad = [jnp.zeros((LANES - KV_HEADS * nsb, tq), jnp.float32)] if KV_HEADS * nsb < LANES else []
    sel_ref[0] = jnp.concatenate(sel_t + pad, axis=0).T.astype(sel_ref.dtype)


def _cmp_attention(qn, ck, cv):
    B, S, _ = qn.shape
    tq = min(CMP_Q, S)
    nsb = S // SLC_BLOCK
    ncp = ck.shape[1]
    return pl.pallas_call(
        functools.partial(_cmp_kernel, tq=tq, nsb=nsb),
        grid=(B, S // tq),
        in_specs=[pl.BlockSpec((1, tq, Q_WIDTH), lambda b, i: (b, i, 0)),
                  pl.BlockSpec((1, ncp, KV_WIDTH), lambda b, i: (b, 0, 0)),
                  pl.BlockSpec((1, ncp, KV_WIDTH), lambda b, i: (b, 0, 0))],
        out_specs=[pl.BlockSpec((1, tq, Q_WIDTH), lambda b, i: (b, i, 0)),
                   pl.BlockSpec((1, tq, LANES), lambda b, i: (b, i, 0))],
        out_shape=[jax.ShapeDtypeStruct((B, S, Q_WIDTH), jnp.float32),
                   jax.ShapeDtypeStruct((B, S, LANES), MXU_DTYPE)],
        name="cmp_select",
        compiler_params=_params("parallel", "parallel"),
    )(qn, ck, cv)


def _slc_kernel(q_ref, k_ref, v_ref, sel_ref, o_ref, *scratch, tq, nsb):
    nheads = KV_HEADS * GROUP
    m_scr, l_scr, acc_scr, s_scr = (scratch[i * nheads:(i + 1) * nheads] for i in range(4))
    qi = pl.program_id(1)
    bias = sel_ref[0]
    col = lax.broadcasted_iota(jnp.int32, bias.shape, 1)
    zero = jnp.zeros_like(bias)
    q_aug = []
    for h, qs in enumerate(_split_heads(q_ref[0])):
        bias_h = jnp.where((col >= h * nsb) & (col < (h + 1) * nsb), bias, zero)
        q_aug += [jnp.concatenate([qs[g * tq:(g + 1) * tq], bias_h], axis=1) for g in range(GROUP)]
    for n in range(nheads):
        m_scr[n][...] = jnp.full(m_scr[n].shape, NEG, jnp.float32)
        l_scr[n][...] = jnp.zeros(l_scr[n].shape, jnp.float32)
        acc_scr[n][...] = jnp.zeros(acc_scr[n].shape, jnp.float32)
    erow = lax.broadcasted_iota(jnp.int32, (tq, LANES), 0) // SLC_BLOCK
    ecol = lax.broadcasted_iota(jnp.int32, (tq, LANES), 1)
    ecol = jnp.where(ecol < KV_HEADS * nsb, ecol % nsb, -1)

    def keys(kt):
        k0 = pl.multiple_of(kt * tq, tq)
        onehot = jnp.where(ecol == erow + kt * (tq // SLC_BLOCK), 1.0, 0.0).astype(MXU_DTYPE)
        return jnp.concatenate([k_ref[0, pl.ds(k0, tq), :], onehot], axis=1)

    def scores(n, k_aug):
        s_scr[n][...] = _dot_nt(k_aug, q_aug[n])

    def consume(n, s, v):
        m_prev = m_scr[n][...]
        m_new = jnp.maximum(m_prev, jnp.max(s, axis=0, keepdims=True))
        alpha = jnp.exp2(m_prev - m_new)
        p = jnp.exp2(s - m_new)
        l_scr[n][...] = alpha * l_scr[n][...] + jnp.sum(p, axis=0, keepdims=True)
        acc_scr[n][...] = alpha * acc_scr[n][...] + _dot_tn(v, p.astype(MXU_DTYPE))
        m_scr[n][...] = m_new

    def body(kt, carry):
        v = v_ref[0, pl.ds(pl.multiple_of(kt * tq, tq), tq), :]
        k_next = keys(kt + 1)
        s_cur = s_scr[0][...]
        scores(0, k_next)
        for n in range(nheads):
            s = s_cur
            if n + 1 < nheads:
                s_cur = s_scr[n + 1][...]
                scores(n + 1, k_next)
            consume(n, s, v)
        return carry

    k_first = keys(0)
    for n in range(nheads):
        scores(n, k_first)
    lax.fori_loop(0, qi, body, 0)
    causal = lax.broadcasted_iota(jnp.int32, (tq, tq), 0) <= lax.broadcasted_iota(jnp.int32, (tq, tq), 1)
    v_diag = v_ref[0, pl.ds(pl.multiple_of(qi * tq, tq), tq), :]
    for n in range(nheads):
        consume(n, jnp.where(causal, s_scr[n][...], NEG), v_diag)
    o_t = [jnp.concatenate([acc_scr[h * GROUP + g][...] / l_scr[h * GROUP + g][...] for g in range(GROUP)], axis=1)
           for h in range(KV_HEADS)]
    o_ref[0] = _merge_heads(o_t, tq)


def _slc_attention(qn, kv, sel):
    B, S, _ = qn.shape
    tq = min(SLC_Q, S)
    nsb = S // SLC_BLOCK
    return pl.pallas_call(
        functools.partial(_slc_kernel, tq=tq, nsb=nsb),
        grid=(B, S // tq),
        in_specs=[pl.BlockSpec((1, tq, Q_WIDTH), lambda b, i: (b, i, 0)),
                  pl.BlockSpec((1, S, KV_WIDTH), lambda b, i: (b, 0, 0)),
                  pl.BlockSpec((1, S, KV_WIDTH), lambda b, i: (b, 0, 3)),
                  pl.BlockSpec((1, tq, LANES), lambda b, i: (b, i, 0))],
        out_specs=pl.BlockSpec((1, tq, Q_WIDTH), lambda b, i: (b, i, 0)),
        out_shape=jax.ShapeDtypeStruct((B, S, Q_WIDTH), jnp.float32),
        scratch_shapes=([pltpu.VMEM((1, tq), jnp.float32)] * (2 * KV_HEADS * GROUP)
                        + [pltpu.VMEM((KV_WIDTH, tq), jnp.float32)] * (KV_HEADS * GROUP)
                        + [pltpu.VMEM((tq, tq), jnp.float32)] * (KV_HEADS * GROUP)),
        name="slc",
        compiler_params=_params("parallel", "parallel"),
    )(qn, kv, kv, sel)


def _band_kernel(*refs, tq, window, sinks):
    if sinks:
        sink_ref, q_ref, k_ref, v_ref, o_ref = refs
    else:
        q_ref, k_ref, v_ref, o_ref = refs
    q0 = pl.program_id(1) * tq
    span = window + tq
    start = pl.multiple_of(jnp.maximum(q0 - window, 0), LANES)
    k = k_ref[0, pl.ds(start, span), :]
    v = v_ref[0, pl.ds(start, span), :]
    qh = _split_heads(q_ref[0])
    kpos = start + lax.broadcasted_iota(jnp.int32, (span, tq), 0)
    qpos = q0 + lax.broadcasted_iota(jnp.int32, (span, tq), 1)
    keep = (kpos <= qpos) & (kpos > qpos - window)
    o_t = []
    for h in range(KV_HEADS):
        s = _dot_nt(k, qh[h])
        s = jnp.concatenate([jnp.where(keep, s[:, g * tq:(g + 1) * tq], NEG) for g in range(GROUP)], axis=1)
        m = jnp.max(s, axis=0, keepdims=True)
        if sinks:
            sk = jnp.concatenate([jnp.full((1, tq), sink_ref[h * GROUP + g] * LOG2E, jnp.float32)
                                  for g in range(GROUP)], axis=1)
            m = jnp.maximum(m, sk)
        e = jnp.exp2(s - m)
        den = jnp.sum(e, axis=0, keepdims=True)
        if sinks:
            den = den + jnp.exp2(sk - m)
        o_t.append(_dot_tn(v, e.astype(MXU_DTYPE)) / den)
    o_ref[0] = _merge_heads(o_t, tq)


def _band_attention(q, kv, k_col, v_col, window, sinks):
    B, S, _ = q.shape
    tq = min(BAND_Q, S)
    in_specs = [pl.BlockSpec((1, tq, Q_WIDTH), lambda b, i: (b, i, 0)),
                pl.BlockSpec((1, S, KV_WIDTH), lambda b, i: (b, 0, k_col)),
                pl.BlockSpec((1, S, KV_WIDTH), lambda b, i: (b, 0, v_col))]
    args = [q, kv, kv]
    if sinks is not None:
        in_specs = [pl.BlockSpec(memory_space=pltpu.SMEM)] + in_specs
        args = [sinks] + args
    return pl.pallas_call(
        functools.partial(_band_kernel, tq=tq, window=window, sinks=sinks is not None),
        grid=(B, S // tq),
        in_specs=in_specs,
        out_specs=pl.BlockSpec((1, tq, Q_WIDTH), lambda b, i: (b, i, 0)),
        out_shape=jax.ShapeDtypeStruct((B, S, Q_WIDTH), jnp.float32),
        name="band_sink" if sinks is not None else "band",
        compiler_params=_params("parallel", "parallel"),
    )(*args)


def _mix_kernel(x_ref, oc_ref, os_ref, ow_ref, osw_ref, gl_ref, ge_ref, w_ref, gn_ref, gs_ref, gp_ref, h_ref):
    sig = 1.0 / (1.0 + jnp.exp(-gl_ref[0]))
    hi = sig.astype(MXU_DTYPE)
    lo = (sig - hi.astype(jnp.float32)).astype(MXU_DTYPE)
    gates = (jnp.dot(hi, ge_ref[...], preferred_element_type=jnp.float32)
             + jnp.dot(lo, ge_ref[...], preferred_element_type=jnp.float32))
    o_nsa = (gates[:, 0:Q_WIDTH] * oc_ref[0] + gates[:, Q_WIDTH:2 * Q_WIDTH] * os_ref[0]
             + gates[:, 2 * Q_WIDTH:3 * Q_WIDTH] * ow_ref[0])
    cat = jnp.concatenate([_rms(o_nsa, gn_ref[...]), _rms(osw_ref[0], gs_ref[...])], axis=1).astype(MXU_DTYPE)
    mixed = jnp.dot(cat, w_ref[...], preferred_element_type=jnp.float32)
    h_ref[0] = x_ref[0] + _rms(mixed, gp_ref[...])


def _mix(x, o_cmp, o_slc, o_win, o_swa, gl, gate_expand, w_out, g_nsa, g_swa, g_post):
    B, S, D = x.shape
    ts = min(MIX_ROWS, S)
    row = lambda width: pl.BlockSpec((1, ts, width), lambda b, i: (b, i, 0))
    full = lambda a: pl.BlockSpec(a.shape, lambda b, i: (0,) * a.ndim)
    return pl.pallas_call(
        _mix_kernel,
        grid=(B, S // ts),
        in_specs=[row(D), row(Q_WIDTH), row(Q_WIDTH), row(Q_WIDTH), row(Q_WIDTH), row(LANES),
                  full(gate_expand), full(w_out), full(g_nsa), full(g_swa), full(g_post)],
        out_specs=row(D),
        out_shape=jax.ShapeDtypeStruct((B, S, D), jnp.float32),
        name="mix",
        compiler_params=_params("parallel", "parallel"),
    )(x, o_cmp, o_slc, o_win, o_swa, gl, gate_expand, w_out, g_nsa, g_swa, g_post)


def _ffn_kernel(h_ref, halo_ref, gpre_ref, wup_ref, cw_ref, cb_ref, wdn_ref, gpost_ref, y_ref, u_scr, act_scr,
                *, ts, nchunk):
    h = h_ref[0]
    live = (pl.program_id(1) > 0).astype(jnp.float32)
    hn = jnp.concatenate([_rms(halo_ref[0], gpre_ref[...]) * live, _rms(h, gpre_ref[...])], axis=0).astype(MXU_DTYPE)
    width = wup_ref.shape[-1]

    def conv(j, slot):
        u_scr[slot] = jnp.dot(hn, wup_ref[j], preferred_element_type=jnp.float32)
        w = cw_ref[j]
        return (w[0:1] * u_scr[slot, pl.ds(HALO - 2, ts), :] + w[1:2] * u_scr[slot, pl.ds(HALO - 1, ts), :]
                + w[2:3] * u_scr[slot, pl.ds(HALO, ts), :] + cb_ref[j])

    for j in range(nchunk):
        slot = 2 * (j % 2)
        act = _gelu_tanh(conv(j, slot)) * conv(nchunk + j, slot + 1)
        act_scr[:, j * width:(j + 1) * width] = act.astype(MXU_DTYPE)
    y = jnp.dot(act_scr[...], wdn_ref[...], preferred_element_type=jnp.float32)
    y_ref[0] = h + _rms(y, gpost_ref[...])


def _ffn(h, g_pre, w_up, conv_w, conv_b, w_down, g_post):
    B, S, D = h.shape
    ts = min(FFN_ROWS, S)
    nchunk, _, width = w_up.shape
    nchunk //= 2
    full = lambda a: pl.BlockSpec(a.shape, lambda b, i: (0,) * a.ndim, pipeline_mode=pl.Buffered(1))
    return pl.pallas_call(
        functools.partial(_ffn_kernel, ts=ts, nchunk=nchunk),
        grid=(B, S // ts),
        in_specs=[pl.BlockSpec((1, ts, D), lambda b, i: (b, i, 0)),
                  pl.BlockSpec((1, HALO, D), lambda b, i: (b, jnp.maximum(i * (ts // HALO) - 1, 0), 0)),
                  full(g_pre), full(w_up), full(conv_w), full(conv_b), full(w_down), full(g_post)],
        out_specs=pl.BlockSpec((1, ts, D), lambda b, i: (b, i, 0)),
        out_shape=jax.ShapeDtypeStruct((B, S, D), jnp.float32),
        scratch_shapes=[pltpu.VMEM((4, HALO + ts, width), jnp.float32),
                        pltpu.VMEM((ts, nchunk * width), MXU_DTYPE)],
        name="ffn",
        compiler_params=_params("parallel", "parallel"),
    )(h, h, g_pre, w_up, conv_w, conv_b, w_down, g_post)


def _rope_tables(S):
    half = HEAD_DIM // 2
    inv = ROPE_THETA ** (-jnp.arange(half, dtype=jnp.float32) / half)
    ang = jnp.arange(S).astype(jnp.float32)[:, None] * inv[None, :]
    cos, sin = jnp.cos(ang), jnp.sin(ang)
    reps = LANES // HEAD_DIM
    return jnp.tile(jnp.concatenate([cos, cos], axis=1), (1, reps)), jnp.tile(jnp.concatenate([-sin, sin], axis=1), (1, reps))


def _layer(x, w_in, w_out, attn_pre_norm, attn_post_norm, nsa_out_norm, swa_out_norm, cmp_pos, cmp_w1, cmp_w2,
           swa_sinks, ffn_pre_norm, ffn_post_norm, w_up, conv_w, conv_b, w_down):
    B, S, D = x.shape
    f32 = jnp.float32
    perm = _q_perm()
    sizes = [Q_WIDTH] + [KV_WIDTH] * 6 + [KV_HEADS * GROUP * NSA_BRANCHES] + [Q_WIDTH, KV_WIDTH, KV_WIDTH]
    offs = np.concatenate([[0], np.cumsum(sizes)])
    col = lambda i: w_in[:, offs[i]:offs[i + 1]]
    q_n, k_c, v_c, k_s, v_s, k_w, v_w, g_n, q_w, k_sw, v_sw = (col(i) for i in range(11))
    g_pad = jnp.pad(g_n, ((0, 0), (0, LANES - g_n.shape[1])))
    w_all = jnp.concatenate([q_n[:, perm], q_w[:, perm], k_c, k_s, k_w, k_sw, v_c, v_s, v_w, v_sw, g_pad],
                            axis=1).astype(MXU_DTYPE)
    cos, sin = _rope_tables(S)
    row = lambda g: g.reshape(1, -1).astype(f32)

    qn, qw, kc, vc, kv, gl = _proj(x, row(attn_pre_norm), w_all, cos, sin)

    eye = jnp.eye(KV_HEADS, dtype=f32)
    pe = jnp.broadcast_to(cmp_pos.reshape(2, 2, CMP_STRIDE, 1, HEAD_DIM), (2, 2, CMP_STRIDE, KV_HEADS, HEAD_DIM))
    pe = pe.reshape(2, 2, 1, CMP_STRIDE * KV_WIDTH)
    w1 = cmp_w1.reshape(2, 2, CMP_STRIDE, HEAD_DIM, CMP_HIDDEN)
    w1x = jnp.einsum('kapdm,hg->kaphdgm', w1, eye).reshape(2, 2, CMP_STRIDE * KV_WIDTH, KV_HEADS * CMP_HIDDEN)
    w2x = jnp.einsum('kmd,hg->khmgd', cmp_w2, eye).reshape(2, KV_HEADS * CMP_HIDDEN, KV_WIDTH)
    nchunks = S // CMP_STRIDE
    ck, cv = _compress(kc.reshape(B, nchunks, CMP_STRIDE * KV_WIDTH), vc.reshape(B, nchunks, CMP_STRIDE * KV_WIDTH),
                       pe, w1x.astype(MXU_DTYPE), w2x.astype(MXU_DTYPE))

    o_cmp, sel = _cmp_attention(qn, ck, cv)
    o_slc = _slc_attention(qn, kv, sel)
    o_win = _band_attention(qn, kv, 1, 4, NSA_WINDOW, None)
    o_swa = _band_attention(qw, kv, 2, 5, SWA_WINDOW, swa_sinks.astype(f32))

    n = np.arange(NSA_BRANCHES * Q_WIDTH)
    br, g, h = n // Q_WIDTH, (n % Q_WIDTH) // LANES, (n % LANES) // HEAD_DIM
    expand = np.zeros((LANES, NSA_BRANCHES * Q_WIDTH), np.float32)
    expand[(h * GROUP + g) * NSA_BRANCHES + br, n] = 1.0
    w_out_p = jnp.concatenate([w_out[perm], w_out[Q_WIDTH + perm]], axis=0).astype(MXU_DTYPE)
    hmid = _mix(x, o_cmp, o_slc, o_win, o_swa, gl, jnp.asarray(expand, MXU_DTYPE), w_out_p,
                row(nsa_out_norm[perm]), row(swa_out_norm[perm]), row(attn_post_norm))

    nchunk = D_FF // FFN_CHUNK
    w_up_c = w_up.reshape(D, 2 * nchunk, FFN_CHUNK).transpose(1, 0, 2).astype(MXU_DTYPE)
    cw_c = conv_w.reshape(-1, 2 * nchunk, FFN_CHUNK).transpose(1, 0, 2).astype(f32)
    cb_c = conv_b.reshape(2 * nchunk, 1, FFN_CHUNK).astype(f32)
    w_dn_c = w_down.astype(MXU_DTYPE)
    return _ffn(hmid, row(ffn_pre_norm), w_up_c, cw_c, cb_c, w_dn_c, row(ffn_post_norm))


def kernel(x, w_in, w_out, attn_pre_norm, attn_post_norm, nsa_out_norm, swa_out_norm, cmp_pos, cmp_w1, cmp_w2,
           swa_sinks, ffn_pre_norm, ffn_post_norm, w_up, conv_w, conv_b, w_down):
    h = x
    for l in range(w_in.shape[0]):
        h = _layer(h, w_in[l], w_out[l], attn_pre_norm[l], attn_post_norm[l], nsa_out_norm[l], swa_out_norm[l],
                   cmp_pos[l], cmp_w1[l], cmp_w2[l], swa_sinks[l], ffn_pre_norm[l], ffn_post_norm[l],
                   w_up[l], conv_w[l], conv_b[l], w_down[l])
    return h
```

```python
import functools

import numpy as np
import jax
import jax.numpy as jnp
from jax import lax
from jax.experimental import pallas as pl
from jax.experimental.pallas import tpu as pltpu

HEAD_DIM = 64
KV_HEADS = 2
GROUP = 4
Q_WIDTH = KV_HEADS * GROUP * HEAD_DIM
KV_WIDTH = KV_HEADS * HEAD_DIM
SWA_WINDOW = 128
NSA_WINDOW = 512
CMP_LEN = 32
CMP_STRIDE = 16
CMP_HIDDEN = 256
SLC_BLOCK = 64
SLC_TOPK = 16
NSA_BRANCHES = 3
D_FF = 2816
ROPE_THETA = 10000.0
RMS_EPS = 1e-6
NEG = -1e30
BIG = 1e9
LOG2E = float(np.log2(np.e))

LANES = 128
SUBLANES = 8
HALO = 8
MXU_DTYPE = jnp.bfloat16
VMEM_LIMIT = 56 * 1024 * 1024

PROJ_ROWS = 512
CMP_Q = 256
SLC_Q = 256
SLC_K = 256
BAND_Q = 256
MIX_ROWS = 512
FFN_ROWS = 1024
FFN_CHUNK = 256
RANK_STEP = 16

_N_ROPE_SLABS = 12
_N_QSLABS = 8
PROJ_WIDTH = 2 * Q_WIDTH + 9 * LANES


def _q_perm():
    n = np.arange(Q_WIDTH)
    g, h, d = n // 128, (n % 128) // 64, n % 64
    return (h * GROUP + g) * HEAD_DIM + d


def _params(*sem):
    return pltpu.CompilerParams(dimension_semantics=sem, vmem_limit_bytes=VMEM_LIMIT)


def _rms(xf, g):
    return xf * lax.rsqrt(jnp.mean(xf * xf, axis=-1, keepdims=True) + RMS_EPS) * g


def _gelu_tanh(x):
    return 0.5 * x * (1.0 + jnp.tanh(np.sqrt(2.0 / np.pi).astype(np.float32) * (x + 0.044715 * (x * x * x))))


def _dot_nt(a, b):
    return lax.dot_general(a, b, (((1,), (1,)), ((), ())), preferred_element_type=jnp.float32)


def _dot_tn(a, b):
    return lax.dot_general(a, b, (((0,), (0,)), ((), ())), preferred_element_type=jnp.float32)


def _split_heads(q):
    qs = jnp.concatenate([q[:, g * LANES:(g + 1) * LANES] for g in range(GROUP)], axis=0)
    lane = lax.broadcasted_iota(jnp.int32, qs.shape, 1)
    zero = jnp.zeros_like(qs)
    return [jnp.where((lane >= h * HEAD_DIM) & (lane < (h + 1) * HEAD_DIM), qs, zero) for h in range(KV_HEADS)]


def _merge_heads(o_t, tq):
    sub = lax.broadcasted_iota(jnp.int32, o_t[0].shape, 0)
    both = jnp.where(sub < HEAD_DIM, o_t[0], o_t[1])
    return jnp.concatenate([both[:, g * tq:(g + 1) * tq].T for g in range(GROUP)], axis=1)


def _proj_kernel(x_ref, g_ref, w_ref, cos_ref, sin_ref, qn_ref, qw_ref, kc_ref, vc_ref, kv_ref, gl_ref):
    hn = _rms(x_ref[0], g_ref[...]).astype(MXU_DTYPE)
    p = jnp.dot(hn, w_ref[...], preferred_element_type=jnp.float32)
    cos, sin = cos_ref[...], sin_ref[...]
    lane = lax.broadcasted_iota(jnp.int32, cos.shape, 1)
    first_half = (lane % HEAD_DIM) < (HEAD_DIM // 2)

    def slab(j, rope):
        z = p[:, j * LANES:(j + 1) * LANES]
        if rope:
            swapped = jnp.where(first_half, pltpu.roll(z, LANES - HEAD_DIM // 2, 1), pltpu.roll(z, HEAD_DIM // 2, 1))
            z = z * cos + swapped * sin
        return z

    scale = HEAD_DIM ** -0.5 * LOG2E
    for j in range(4):
        qn_ref[0, :, j * LANES:(j + 1) * LANES] = (slab(j, True) * scale).astype(qn_ref.dtype)
        qw_ref[0, :, j * LANES:(j + 1) * LANES] = (slab(4 + j, True) * scale).astype(qw_ref.dtype)
    kc_ref[0] = slab(8, True)
    vc_ref[0] = slab(12, False)
    for j in range(3):
        kv_ref[0, :, j * LANES:(j + 1) * LANES] = slab(9 + j, True).astype(kv_ref.dtype)
    for j in range(3):
        kv_ref[0, :, (3 + j) * LANES:(4 + j) * LANES] = slab(13 + j, False).astype(kv_ref.dtype)
    gl_ref[0] = slab(16, False)


def _proj(x, gain, w, cos, sin):
    B, S, D = x.shape
    ts = min(PROJ_ROWS, S)
    row = lambda width: pl.BlockSpec((1, ts, width), lambda b, i: (b, i, 0))
    full = lambda a: pl.BlockSpec(a.shape, lambda b, i: (0,) * a.ndim)
    tab = pl.BlockSpec((ts, LANES), lambda b, i: (i, 0))
    f32, bf = jnp.float32, MXU_DTYPE
    return pl.pallas_call(
        _proj_kernel,
        grid=(B, S // ts),
        in_specs=[row(D), full(gain), full(w), tab, tab],
        out_specs=[row(Q_WIDTH), row(Q_WIDTH), row(LANES), row(LANES), row(6 * LANES), row(LANES)],
        out_shape=[jax.ShapeDtypeStruct((B, S, Q_WIDTH), bf), jax.ShapeDtypeStruct((B, S, Q_WIDTH), bf),
                   jax.ShapeDtypeStruct((B, S, LANES), f32), jax.ShapeDtypeStruct((B, S, LANES), f32),
                   jax.ShapeDtypeStruct((B, S, 6 * LANES), bf), jax.ShapeDtypeStruct((B, S, LANES), f32)],
        name="proj",
        compiler_params=_params("parallel", "parallel"),
    )(x, gain, w, cos, sin)


def _compress_kernel(kc_ref, vc_ref, pe_ref, w1_ref, w2_ref, ck_ref, cv_ref):
    for kv, (src, dst) in enumerate(((kc_ref, ck_ref), (vc_ref, cv_ref))):
        ch = src[0]
        top = jnp.dot((ch + pe_ref[kv, 0]).astype(MXU_DTYPE), w1_ref[kv, 0], preferred_element_type=jnp.float32)
        bot = jnp.dot((ch + pe_ref[kv, 1]).astype(MXU_DTYPE), w1_ref[kv, 1], preferred_element_type=jnp.float32)
        n = ch.shape[0]
        hid = top + pltpu.roll(bot, n - 1, 0)
        act = _gelu_tanh(hid).astype(MXU_DTYPE)
        dst[0] = jnp.dot(act, w2_ref[kv], preferred_element_type=jnp.float32).astype(dst.dtype)


def _compress(kc_r, vc_r, pe, w1, w2):
    B, n, width = kc_r.shape
    src = pl.BlockSpec((1, n, width), lambda b: (b, 0, 0))
    full = lambda a: pl.BlockSpec(a.shape, lambda b: (0,) * a.ndim)
    dst = pl.BlockSpec((1, n, KV_WIDTH), lambda b: (b, 0, 0))
    return pl.pallas_call(
        _compress_kernel,
        grid=(B,),
        in_specs=[src, src, full(pe), full(w1), full(w2)],
        out_specs=[dst, dst],
        out_shape=[jax.ShapeDtypeStruct((B, n, KV_WIDTH), MXU_DTYPE)] * 2,
        name="compress",
        compiler_params=_params("parallel"),
    )(kc_r, vc_r, pe, w1, w2)


def _select_bias(score, *, live):
    nsb, tq = score.shape
    groups = [score[SUBLANES * k:SUBLANES * (k + 1)] for k in range(live // SUBLANES)]
    rank = [jnp.zeros((SUBLANES, tq), jnp.int32) for _ in groups]
    sub = lax.broadcasted_iota(jnp.int32, (SUBLANES, tq), 0)
    for i in range(live):
        row = score[i:i + 1, :]
        for k, grp in enumerate(groups):
            if SUBLANES * k > i:
                beats = jnp.where(row >= grp, 1, 0)
            elif SUBLANES * k + SUBLANES - 1 <= i:
                beats = jnp.where(row > grp, 1, 0)
            else:
                beats = jnp.where(sub > i - SUBLANES * k, jnp.where(row >= grp, 1, 0), jnp.where(row > grp, 1, 0))
            rank[k] = rank[k] + beats
    bias = [jnp.where(jnp.concatenate(rank, axis=0) < min(SLC_TOPK, nsb), 0.0, NEG)]
    if live < nsb:
        bias.append(jnp.full((nsb - live, tq), NEG, jnp.float32))
    return jnp.concatenate(bias, axis=0)


def _cmp_kernel(q_ref, ck_ref, cv_ref, o_ref, sel_ref, *, tq, nsb):
    q0 = pl.program_id(1) * tq
    ck, cv = ck_ref[0], cv_ref[0]
    ncp = ck.shape[0]
    qh = _split_heads(q_ref[0])
    c = lax.broadcasted_iota(jnp.int32, (ncp, tq), 0)
    t = q0 + lax.broadcasted_iota(jnp.int32, (ncp, tq), 1)
    valid = c * CMP_STRIDE + (CMP_LEN - 1) <= t

    jo = lax.broadcasted_iota(jnp.int32, (nsb, ncp), 0)
    co = lax.broadcasted_iota(jnp.int32, (nsb, ncp), 1)
    overlap = ((co * CMP_STRIDE <= jo * SLC_BLOCK + SLC_BLOCK - 1)
               & (co * CMP_STRIDE + CMP_LEN - 1 >= jo * SLC_BLOCK) & (co < ncp - 1)).astype(MXU_DTYPE)

    jb = lax.broadcasted_iota(jnp.int32, (nsb, tq), 0)
    tb = q0 + lax.broadcasted_iota(jnp.int32, (nsb, tq), 1)
    blk = tb // SLC_BLOCK
    forced = (jb == 0) | (jb == blk) | (jb == blk - 1)
    causal = jb * SLC_BLOCK <= tb

    has_valid = (q0 + lax.broadcasted_iota(jnp.int32, (1, GROUP * tq), 1) % tq >= CMP_LEN - 1).astype(jnp.float32)
    rankers = [functools.partial(_select_bias, live=live) for live in range(RANK_STEP, nsb + 1, RANK_STEP)]
    which = jnp.minimum((q0 + tq - 1) // (RANK_STEP * SLC_BLOCK), len(rankers) - 1)

    o_t, sel_t = [], []
    for h in range(KV_HEADS):
        s = _dot_nt(ck, qh[h])
        s = jnp.concatenate([jnp.where(valid, s[:, g * tq:(g + 1) * tq], NEG) for g in range(GROUP)], axis=1)
        e = jnp.exp2(s - jnp.max(s, axis=0, keepdims=True))
        p = e * (has_valid / jnp.maximum(jnp.sum(e, axis=0, keepdims=True), 1e-30))
        o_t.append(_dot_tn(cv, p.astype(MXU_DTYPE)))

        pg = p[:, 0:tq] + p[:, tq:2 * tq] + p[:, 2 * tq:3 * tq] + p[:, 3 * tq:4 * tq]
        hi = pg.astype(MXU_DTYPE)
        lo = (pg - hi.astype(jnp.float32)).astype(MXU_DTYPE)
        imp = (jnp.dot(overlap, hi, preferred_element_type=jnp.float32)
               + jnp.dot(overlap, lo, preferred_element_type=jnp.float32))
        score = jnp.where(forced, BIG, jnp.where(causal, imp, NEG))
        sel_t.append(lax.switch(which, rankers, score))

    o_ref[0] = _merge_heads(o_t, tq)
    pad = [jnp.zeros((LANES - KV_HEADS * nsb, tq), jnp.float32)] if KV_HEADS * nsb < LANES else []
    sel_ref[0] = jnp.concatenate(sel_t + pad, axis=0).T.astype(sel_ref.dtype)


def _cmp_attention(qn, ck, cv):
    B, S, _ = qn.shape
    tq = min(CMP_Q, S)
    nsb = S // SLC_BLOCK
    ncp = ck.shape[1]
    return pl.pallas_call(
        functools.partial(_cmp_kernel, tq=tq, nsb=nsb),
        grid=(B, S // tq),
        in_specs=[pl.BlockSpec((1, tq, Q_WIDTH), lambda b, i: (b, i, 0)),
                  pl.BlockSpec((1, ncp, KV_WIDTH), lambda b, i: (b, 0, 0)),
                  pl.BlockSpec((1, ncp, KV_WIDTH), lambda b, i: (b, 0, 0))],
        out_specs=[pl.BlockSpec((1, tq, Q_WIDTH), lambda b, i: (b, i, 0)),
                   pl.BlockSpec((1, tq, LANES), lambda b, i: (b, i, 0))],
        out_shape=[jax.ShapeDtypeStruct((B, S, Q_WIDTH), jnp.float32),
                   jax.ShapeDtypeStruct((B, S, LANES), MXU_DTYPE)],
        name="cmp_select",
        compiler_params=_params("parallel", "parallel"),
    )(qn, ck, cv)


def _slc_kernel(q_ref, k_ref, v_ref, sel_ref, o_ref, *scratch, tq, nsb):
    nheads = KV_HEADS * GROUP
    m_scr, l_scr, acc_scr, s_scr = (scratch[i * nheads:(i + 1) * nheads] for i in range(4))
    qi = pl.program_id(1)
    bias = sel_ref[0]
    col = lax.broadcasted_iota(jnp.int32, bias.shape, 1)
    zero = jnp.zeros_like(bias)
    q_aug = []
    for h, qs in enumerate(_split_heads(q_ref[0])):
        bias_h = jnp.where((col >= h * nsb) & (col < (h + 1) * nsb), bias, zero)
        q_aug += [jnp.concatenate([qs[g * tq:(g + 1) * tq], bias_h], axis=1) for g in range(GROUP)]
    for n in range(nheads):
        m_scr[n][...] = jnp.full(m_scr[n].shape, NEG, jnp.float32)
        l_scr[n][...] = jnp.zeros(l_scr[n].shape, jnp.float32)
        acc_scr[n][...] = jnp.zeros(acc_scr[n].shape, jnp.float32)
    erow = lax.broadcasted_iota(jnp.int32, (tq, LANES), 0) // SLC_BLOCK
    ecol = lax.broadcasted_iota(jnp.int32, (tq, LANES), 1)
    ecol = jnp.where(ecol < KV_HEADS * nsb, ecol % nsb, -1)

    def keys(kt):
        k0 = pl.multiple_of(kt * tq, tq)
        onehot = jnp.where(ecol == erow + kt * (tq // SLC_BLOCK), 1.0, 0.0).astype(MXU_DTYPE)
        return jnp.concatenate([k_ref[0, pl.ds(k0, tq), :], onehot], axis=1)

    def scores(n, k_aug):
        s_scr[n][...] = _dot_nt(k_aug, q_aug[n])

    def consume(n, s, v):
        m_prev = m_scr[n][...]
        m_new = jnp.maximum(m_prev, jnp.max(s, axis=0, keepdims=True))
        alpha = jnp.exp2(m_prev - m_new)
        p = jnp.exp2(s - m_new)
        l_scr[n][...] = alpha * l_scr[n][...] + jnp.sum(p, axis=0, keepdims=True)
        acc_scr[n][...] = alpha * acc_scr[n][...] + _dot_tn(v, p.astype(MXU_DTYPE))
        m_scr[n][...] = m_new

    def body(kt, carry):
        v = v_ref[0, pl.ds(pl.multiple_of(kt * tq, tq), tq), :]
        k_next = keys(kt + 1)
        s_cur = s_scr[0][...]
        scores(0, k_next)
        for n in range(nheads):
            s = s_cur
            if n + 1 < nheads:
                s_cur = s_scr[n + 1][...]
                scores(n + 1, k_next)
            consume(n, s, v)
        return carry

    k_first = keys(0)
    for n in range(nheads):
        scores(n, k_first)
    lax.fori_loop(0, qi, body, 0)
    causal = lax.broadcasted_iota(jnp.int32, (tq, tq), 0) <= lax.broadcasted_iota(jnp.int32, (tq, tq), 1)
    v_diag = v_ref[0, pl.ds(pl.multiple_of(qi * tq, tq), tq), :]
    for n in range(nheads):
        consume(n, jnp.where(causal, s_scr[n][...], NEG), v_diag)
    o_t = [jnp.concatenate([acc_scr[h * GROUP + g][...] / l_scr[h * GROUP + g][...] for g in range(GROUP)], axis=1)
           for h in range(KV_HEADS)]
    o_ref[0] = _merge_heads(o_t, tq)


def _slc_attention(qn, kv, sel):
    B, S, _ = qn.shape
    tq = min(SLC_Q, S)
    nsb = S // SLC_BLOCK
    return pl.pallas_call(
        functools.partial(_slc_kernel, tq=tq, nsb=nsb),
        grid=(B, S // tq),
        in_specs=[pl.BlockSpec((1, tq, Q_WIDTH), lambda b, i: (b, i, 0)),
                  pl.BlockSpec((1, S, KV_WIDTH), lambda b, i: (b, 0, 0)),
                  pl.BlockSpec((1, S, KV_WIDTH), lambda b, i: (b, 0, 3)),
                  pl.BlockSpec((1, tq, LANES), lambda b, i: (b, i, 0))],
        out_specs=pl.BlockSpec((1, tq, Q_WIDTH), lambda b, i: (b, i, 0)),
        out_shape=jax.ShapeDtypeStruct((B, S, Q_WIDTH), jnp.float32),
        scratch_shapes=([pltpu.VMEM((1, tq), jnp.float32)] * (2 * KV_HEADS * GROUP)
                        + [pltpu.VMEM((KV_WIDTH, tq), jnp.float32)] * (KV_HEADS * GROUP)
                        + [pltpu.VMEM((tq, tq), jnp.float32)] * (KV_HEADS * GROUP)),
        name="slc",
        compiler_params=_params("parallel", "parallel"),
    )(qn, kv, kv, sel)


def _band_kernel(*refs, tq, window, sinks):
    if sinks:
        sink_ref, q_ref, k_ref, v_ref, o_ref = refs
    else:
        q_ref, k_ref, v_ref, o_ref = refs
    q0 = pl.program_id(1) * tq
    span = window + tq
    nsub, nhalf, wsub = span // LANES, tq // LANES, window // LANES
    nheads = KV_HEADS * GROUP
    heads = [qs[g * tq:(g + 1) * tq] for qs in _split_heads(q_ref[0]) for g in range(GROUP)]
    kl = lax.broadcasted_iota(jnp.int32, (LANES, LANES), 0)
    ql = lax.broadcasted_iota(jnp.int32, (LANES, LANES), 1)

    def attend(start, mask_of, lookahead):
        k = k_ref[0, pl.ds(start, span), :]
        v = v_ref[0, pl.ds(start, span), :]

        def finish(n, s):
            cols, dens = [], []
            for c in range(nhalf):
                live = {}
                for r in range(nsub):
                    mask = mask_of(r, c)
                    if mask is False:
                        continue
                    blk = s[r * LANES:(r + 1) * LANES, c * LANES:(c + 1) * LANES]
                    live[r] = blk if mask is None else jnp.where(mask, blk, NEG)
                m = functools.reduce(jnp.maximum, [jnp.max(b, axis=0, keepdims=True) for b in live.values()])
                if sinks:
                    sk = jnp.full((1, LANES), sink_ref[n] * LOG2E, jnp.float32)
                    m = jnp.maximum(m, sk)
                e = {r: jnp.exp2(b - m) for r, b in live.items()}
                den = functools.reduce(jnp.add, [jnp.sum(b, axis=0, keepdims=True) for b in e.values()])
                if sinks:
                    den = den + jnp.exp2(sk - m)
                zero = jnp.zeros((LANES, LANES), jnp.float32)
                cols.append(jnp.concatenate([e.get(r, zero) for r in range(nsub)], axis=0))
                dens.append(den)
            p = jnp.concatenate(cols, axis=1).astype(MXU_DTYPE)
            return _dot_tn(v, p) / jnp.concatenate(dens, axis=1)

        pending = [_dot_nt(k, heads[n]) for n in range(lookahead)]
        outs = []
        for n in range(nheads):
            if n + lookahead < nheads:
                pending.append(_dot_nt(k, heads[n + lookahead]))
            outs.append(finish(n, pending[n]))
        o_ref[0] = _merge_heads([jnp.concatenate(outs[h * GROUP:(h + 1) * GROUP], axis=1) for h in range(KV_HEADS)], tq)

    @pl.when(q0 >= window)
    def _():
        def mask_of(r, c):
            if r == c:
                return kl > ql
            if r == c + wsub:
                return kl <= ql
            return None if c < r < c + wsub else False
        attend(pl.multiple_of(q0 - window, LANES), mask_of, 2)

    @pl.when(q0 < window)
    def _():
        def mask_of(r, c):
            kpos, qpos = r * LANES + kl, q0 + c * LANES + ql
            return (kpos <= qpos) & (kpos > qpos - window)
        attend(0, mask_of, 1)


def _band_attention(q, kv, k_col, v_col, window, sinks):
    B, S, _ = q.shape
    tq = min(BAND_Q, S)
    in_specs = [pl.BlockSpec((1, tq, Q_WIDTH), lambda b, i: (b, i, 0)),
                pl.BlockSpec((1, S, KV_WIDTH), lambda b, i: (b, 0, k_col)),
                pl.BlockSpec((1, S, KV_WIDTH), lambda b, i: (b, 0, v_col))]
    args = [q, kv, kv]
    if sinks is not None:
        in_specs = [pl.BlockSpec(memory_space=pltpu.SMEM)] + in_specs
        args = [sinks] + args
    return pl.pallas_call(
        functools.partial(_band_kernel, tq=tq, window=window, sinks=sinks is not None),
        grid=(B, S // tq),
        in_specs=in_specs,
        out_specs=pl.BlockSpec((1, tq, Q_WIDTH), lambda b, i: (b, i, 0)),
        out_shape=jax.ShapeDtypeStruct((B, S, Q_WIDTH), jnp.float32),
        name="band_sink" if sinks is not None else "band",
        compiler_params=_params("parallel", "parallel"),
    )(*args)


def _mix_kernel(x_ref, oc_ref, os_ref, ow_ref, osw_ref, gl_ref, ge_ref, w_ref, gn_ref, gs_ref, gp_ref, h_ref):
    sig = 1.0 / (1.0 + jnp.exp(-gl_ref[0]))
    hi = sig.astype(MXU_DTYPE)
    lo = (sig - hi.astype(jnp.float32)).astype(MXU_DTYPE)
    gates = jnp.dot(jnp.concatenate([hi, lo], axis=1), ge_ref[...],
                    preferred_element_type=jnp.float32)
    o_nsa = (gates[:, 0:Q_WIDTH] * oc_ref[0] + gates[:, Q_WIDTH:2 * Q_WIDTH] * os_ref[0]
             + gates[:, 2 * Q_WIDTH:3 * Q_WIDTH] * ow_ref[0])
    cat = jnp.concatenate([_rms(o_nsa, gn_ref[...]), _rms(osw_ref[0], gs_ref[...])], axis=1).astype(MXU_DTYPE)
    mixed = jnp.dot(cat, w_ref[...], preferred_element_type=jnp.float32)
    h_ref[0] = x_ref[0] + _rms(mixed, gp_ref[...])


def _mix(x, o_cmp, o_slc, o_win, o_swa, gl, gate_expand, w_out, g_nsa, g_swa, g_post):
    B, S, D = x.shape
    ts = min(MIX_ROWS, S)
    row = lambda width: pl.BlockSpec((1, ts, width), lambda b, i: (b, i, 0))
    full = lambda a: pl.BlockSpec(a.shape, lambda b, i: (0,) * a.ndim)
    return pl.pallas_call(
        _mix_kernel,
        grid=(B, S // ts),
        in_specs=[row(D), row(Q_WIDTH), row(Q_WIDTH), row(Q_WIDTH), row(Q_WIDTH), row(LANES),
                  full(gate_expand), full(w_out), full(g_nsa), full(g_swa), full(g_post)],
        out_specs=row(D),
        out_shape=jax.ShapeDtypeStruct((B, S, D), jnp.float32),
        name="mix",
        compiler_params=_params("parallel", "parallel"),
    )(x, o_cmp, o_slc, o_win, o_swa, gl, gate_expand, w_out, g_nsa, g_swa, g_post)


def _ffn_kernel(h_ref, halo_ref, gpre_ref, wup_ref, cw_ref, cb_ref, wdn_ref, gpost_ref, y_ref, u_scr, act_scr,
                *, ts, nchunk):
    h = h_ref[0]
    live = (pl.program_id(1) > 0).astype(jnp.float32)
    hn = jnp.concatenate([_rms(halo_ref[0], gpre_ref[...]) * live, _rms(h, gpre_ref[...])], axis=0).astype(MXU_DTYPE)
    width = wup_ref.shape[-1]

    def conv(j, slot):
        u_scr[slot] = jnp.dot(hn, wup_ref[j], preferred_element_type=jnp.float32)
        w = cw_ref[j]
        return (w[0:1] * u_scr[slot, pl.ds(HALO - 2, ts), :] + w[1:2] * u_scr[slot, pl.ds(HALO - 1, ts), :]
                + w[2:3] * u_scr[slot, pl.ds(HALO, ts), :] + cb_ref[j])

    for j in range(nchunk):
        slot = 2 * (j % 2)
        act = _gelu_tanh(conv(j, slot)) * conv(nchunk + j, slot + 1)
        act_scr[:, j * width:(j + 1) * width] = act.astype(MXU_DTYPE)
    y = jnp.dot(act_scr[...], wdn_ref[...], preferred_element_type=jnp.float32)
    y_ref[0] = h + _rms(y, gpost_ref[...])


def _ffn(h, g_pre, w_up, conv_w, conv_b, w_down, g_post):
    B, S, D = h.shape
    ts = min(FFN_ROWS, S)
    nchunk, _, width = w_up.shape
    nchunk //= 2
    full = lambda a: pl.BlockSpec(a.shape, lambda b, i: (0,) * a.ndim, pipeline_mode=pl.Buffered(1))
    return pl.pallas_call(
        functools.partial(_ffn_kernel, ts=ts, nchunk=nchunk),
        grid=(B, S // ts),
        in_specs=[pl.BlockSpec((1, ts, D), lambda b, i: (b, i, 0)),
                  pl.BlockSpec((1, HALO, D), lambda b, i: (b, jnp.maximum(i * (ts // HALO) - 1, 0), 0)),
                  full(g_pre), full(w_up), full(conv_w), full(conv_b), full(w_down), full(g_post)],
        out_specs=pl.BlockSpec((1, ts, D), lambda b, i: (b, i, 0)),
        out_shape=jax.ShapeDtypeStruct((B, S, D), jnp.float32),
        scratch_shapes=[pltpu.VMEM((4, HALO + ts, width), jnp.float32),
                        pltpu.VMEM((ts, nchunk * width), MXU_DTYPE)],
        name="ffn",
        compiler_params=_params("parallel", "parallel"),
    )(h, h, g_pre, w_up, conv_w, conv_b, w_down, g_post)


def _rope_tables(S):
    half = HEAD_DIM // 2
    inv = ROPE_THETA ** (-jnp.arange(half, dtype=jnp.float32) / half)
    ang = jnp.arange(S).astype(jnp.float32)[:, None] * inv[None, :]
    cos, sin = jnp.cos(ang), jnp.sin(ang)
    reps = LANES // HEAD_DIM
    return jnp.tile(jnp.concatenate([cos, cos], axis=1), (1, reps)), jnp.tile(jnp.concatenate([-sin, sin], axis=1), (1, reps))


def _layer(x, w_in, w_out, attn_pre_norm, attn_post_norm, nsa_out_norm, swa_out_norm, cmp_pos, cmp_w1, cmp_w2,
           swa_sinks, ffn_pre_norm, ffn_post_norm, w_up, conv_w, conv_b, w_down):
    B, S, D = x.shape
    f32 = jnp.float32
    perm = _q_perm()
    sizes = [Q_WIDTH] + [KV_WIDTH] * 6 + [KV_HEADS * GROUP * NSA_BRANCHES] + [Q_WIDTH, KV_WIDTH, KV_WIDTH]
    offs = np.concatenate([[0], np.cumsum(sizes)])
    col = lambda i: w_in[:, offs[i]:offs[i + 1]]
    q_n, k_c, v_c, k_s, v_s, k_w, v_w, g_n, q_w, k_sw, v_sw = (col(i) for i in range(11))
    g_pad = jnp.pad(g_n, ((0, 0), (0, LANES - g_n.shape[1])))
    w_all = jnp.concatenate([q_n[:, perm], q_w[:, perm], k_c, k_s, k_w, k_sw, v_c, v_s, v_w, v_sw, g_pad],
                            axis=1).astype(MXU_DTYPE)
    cos, sin = _rope_tables(S)
    row = lambda g: g.reshape(1, -1).astype(f32)

    qn, qw, kc, vc, kv, gl = _proj(x, row(attn_pre_norm), w_all, cos, sin)

    eye = jnp.eye(KV_HEADS, dtype=f32)
    pe = jnp.broadcast_to(cmp_pos.reshape(2, 2, CMP_STRIDE, 1, HEAD_DIM), (2, 2, CMP_STRIDE, KV_HEADS, HEAD_DIM))
    pe = pe.reshape(2, 2, 1, CMP_STRIDE * KV_WIDTH)
    w1 = cmp_w1.reshape(2, 2, CMP_STRIDE, HEAD_DIM, CMP_HIDDEN)
    w1x = jnp.einsum('kapdm,hg->kaphdgm', w1, eye).reshape(2, 2, CMP_STRIDE * KV_WIDTH, KV_HEADS * CMP_HIDDEN)
    w2x = jnp.einsum('kmd,hg->khmgd', cmp_w2, eye).reshape(2, KV_HEADS * CMP_HIDDEN, KV_WIDTH)
    nchunks = S // CMP_STRIDE
    ck, cv = _compress(kc.reshape(B, nchunks, CMP_STRIDE * KV_WIDTH), vc.reshape(B, nchunks, CMP_STRIDE * KV_WIDTH),
                       pe, w1x.astype(MXU_DTYPE), w2x.astype(MXU_DTYPE))

    o_cmp, sel = _cmp_attention(qn, ck, cv)
    o_slc = _slc_attention(qn, kv, sel)
    o_win = _band_attention(qn, kv, 1, 4, NSA_WINDOW, None)
    o_swa = _band_attention(qw, kv, 2, 5, SWA_WINDOW, swa_sinks.astype(f32))

    n = np.arange(NSA_BRANCHES * Q_WIDTH)
    br, g, h = n // Q_WIDTH, (n % Q_WIDTH) // LANES, (n % LANES) // HEAD_DIM
    expand = np.zeros((LANES, NSA_BRANCHES * Q_WIDTH), np.float32)
    expand[(h * GROUP + g) * NSA_BRANCHES + br, n] = 1.0
    expand = np.concatenate([expand, expand], axis=0)
    w_out_p = jnp.concatenate([w_out[perm], w_out[Q_WIDTH + perm]], axis=0).astype(MXU_DTYPE)
    hmid = _mix(x, o_cmp, o_slc, o_win, o_swa, gl, jnp.asarray(expand, MXU_DTYPE), w_out_p,
                row(nsa_out_norm[perm]), row(swa_out_norm[perm]), row(attn_post_norm))

    nchunk = D_FF // FFN_CHUNK
    w_up_c = w_up.reshape(D, 2 * nchunk, FFN_CHUNK).transpose(1, 0, 2).astype(MXU_DTYPE)
    cw_c = conv_w.reshape(-1, 2 * nchunk, FFN_CHUNK).transpose(1, 0, 2).astype(f32)
    cb_c = conv_b.reshape(2 * nchunk, 1, FFN_CHUNK).astype(f32)
    w_dn_c = w_down.astype(MXU_DTYPE)
    return _ffn(hmid, row(ffn_pre_norm), w_up_c, cw_c, cb_c, w_dn_c, row(ffn_post_norm))


def kernel(x, w_in, w_out, attn_pre_norm, attn_post_norm, nsa_out_norm, swa_out_norm, cmp_pos, cmp_w1, cmp_w2,
           swa_sinks, ffn_pre_norm, ffn_post_norm, w_up, conv_w, conv_b, w_down):
    h = x
    for l in range(w_in.shape[0]):
        h = _layer(h, w_in[l], w_out[l], attn_pre_norm[l], attn_post_norm[l], nsa_out_norm[l], swa_out_norm[l],
                   cmp_pos[l], cmp_w1[l], cmp_w2[l], swa_sinks[l], ffn_pre_norm[l], ffn_post_norm[l],
                   w_up[l], conv_w[l], conv_b[l], w_down[l])
    return h
```

```python
import functools

import numpy as np
import jax
import jax.numpy as jnp
from jax import lax
from jax.experimental import pallas as pl
from jax.experimental.pallas import tpu as pltpu

HEAD_DIM = 64
KV_HEADS = 2
GROUP = 4
Q_WIDTH = KV_HEADS * GROUP * HEAD_DIM
KV_WIDTH = KV_HEADS * HEAD_DIM
SWA_WINDOW = 128
NSA_WINDOW = 512
CMP_LEN = 32
CMP_STRIDE = 16
CMP_HIDDEN = 256
SLC_BLOCK = 64
SLC_TOPK = 16
NSA_BRANCHES = 3
D_FF = 2816
ROPE_THETA = 10000.0
RMS_EPS = 1e-6
NEG = -1e30
BIG = 1e9
LOG2E = float(np.log2(np.e))

LANES = 128
SUBLANES = 8
HALO = 8
MXU_DTYPE = jnp.bfloat16
VMEM_LIMIT = 56 * 1024 * 1024

PROJ_ROWS = 512
CMP_Q = 256
SLC_Q = 256
SLC_K = 256
BAND_Q = 256
MIX_ROWS = 512
FFN_ROWS = 1024
FFN_CHUNK = 256
RANK_STEP = 16

_N_ROPE_SLABS = 12
_N_QSLABS = 8
PROJ_WIDTH = 2 * Q_WIDTH + 9 * LANES


def _q_perm():
    n = np.arange(Q_WIDTH)
    g, h, d = n // 128, (n % 128) // 64, n % 64
    return (h * GROUP + g) * HEAD_DIM + d


def _params(*sem):
    return pltpu.CompilerParams(dimension_semantics=sem, vmem_limit_bytes=VMEM_LIMIT)


def _rms(xf, g):
    return xf * lax.rsqrt(jnp.mean(xf * xf, axis=-1, keepdims=True) + RMS_EPS) * g


def _gelu_tanh(x):
    return 0.5 * x * (1.0 + jnp.tanh(np.sqrt(2.0 / np.pi).astype(np.float32) * (x + 0.044715 * (x * x * x))))


def _dot_nt(a, b):
    return lax.dot_general(a, b, (((1,), (1,)), ((), ())), preferred_element_type=jnp.float32)


def _dot_tn(a, b):
    return lax.dot_general(a, b, (((0,), (0,)), ((), ())), preferred_element_type=jnp.float32)


def _split_heads(q):
    qs = jnp.concatenate([q[:, g * LANES:(g + 1) * LANES] for g in range(GROUP)], axis=0)
    lane = lax.broadcasted_iota(jnp.int32, qs.shape, 1)
    zero = jnp.zeros_like(qs)
    return [jnp.where((lane >= h * HEAD_DIM) & (lane < (h + 1) * HEAD_DIM), qs, zero) for h in range(KV_HEADS)]


def _values_with_ones(v):
    lane = lax.broadcasted_iota(jnp.int32, v.shape, 1)
    one = jnp.ones_like(v)
    return [jnp.where((lane >= h * HEAD_DIM) & (lane < (h + 1) * HEAD_DIM), v, one) for h in range(KV_HEADS)]


def _merge_heads(o_t, tq):
    sub = lax.broadcasted_iota(jnp.int32, o_t[0].shape, 0)
    both = jnp.where(sub < HEAD_DIM, o_t[0], o_t[1])
    return jnp.concatenate([both[:, g * tq:(g + 1) * tq].T for g in range(GROUP)], axis=1)


def _proj_kernel(x_ref, g_ref, w_ref, cos_ref, sin_ref, qn_ref, qw_ref, kc_ref, vc_ref, kv_ref, gl_ref):
    hn = _rms(x_ref[0], g_ref[...]).astype(MXU_DTYPE)
    p = jnp.dot(hn, w_ref[...], preferred_element_type=jnp.float32)
    cos, sin = cos_ref[...], sin_ref[...]
    lane = lax.broadcasted_iota(jnp.int32, cos.shape, 1)
    first_half = (lane % HEAD_DIM) < (HEAD_DIM // 2)

    def slab(j, rope):
        z = p[:, j * LANES:(j + 1) * LANES]
        if rope:
            swapped = jnp.where(first_half, pltpu.roll(z, LANES - HEAD_DIM // 2, 1), pltpu.roll(z, HEAD_DIM // 2, 1))
            z = z * cos + swapped * sin
        return z

    scale = HEAD_DIM ** -0.5 * LOG2E
    for j in range(4):
        qn_ref[0, :, j * LANES:(j + 1) * LANES] = (slab(j, True) * scale).astype(qn_ref.dtype)
        qw_ref[0, :, j * LANES:(j + 1) * LANES] = (slab(4 + j, True) * scale).astype(qw_ref.dtype)
    kc_ref[0] = slab(8, True)
    vc_ref[0] = slab(12, False)
    for j in range(3):
        kv_ref[0, :, j * LANES:(j + 1) * LANES] = slab(9 + j, True).astype(kv_ref.dtype)
    for j in range(3):
        kv_ref[0, :, (3 + j) * LANES:(4 + j) * LANES] = slab(13 + j, False).astype(kv_ref.dtype)
    gl_ref[0] = slab(16, False)


def _proj(x, gain, w, cos, sin):
    B, S, D = x.shape
    ts = min(PROJ_ROWS, S)
    row = lambda width: pl.BlockSpec((1, ts, width), lambda b, i: (b, i, 0))
    full = lambda a: pl.BlockSpec(a.shape, lambda b, i: (0,) * a.ndim)
    tab = pl.BlockSpec((ts, LANES), lambda b, i: (i, 0))
    f32, bf = jnp.float32, MXU_DTYPE
    return pl.pallas_call(
        _proj_kernel,
        grid=(B, S // ts),
        in_specs=[row(D), full(gain), full(w), tab, tab],
        out_specs=[row(Q_WIDTH), row(Q_WIDTH), row(LANES), row(LANES), row(6 * LANES), row(LANES)],
        out_shape=[jax.ShapeDtypeStruct((B, S, Q_WIDTH), bf), jax.ShapeDtypeStruct((B, S, Q_WIDTH), bf),
                   jax.ShapeDtypeStruct((B, S, LANES), f32), jax.ShapeDtypeStruct((B, S, LANES), f32),
                   jax.ShapeDtypeStruct((B, S, 6 * LANES), bf), jax.ShapeDtypeStruct((B, S, LANES), f32)],
        name="proj",
        compiler_params=_params("parallel", "parallel"),
    )(x, gain, w, cos, sin)


def _compress_kernel(kc_ref, vc_ref, pe_ref, w1_ref, w2_ref, ck_ref, cv_ref):
    n = kc_ref.shape[1] // CMP_STRIDE
    for kv, (src, dst) in enumerate(((kc_ref, ck_ref), (vc_ref, cv_ref))):
        ch = jnp.concatenate([src[0, pl.ds(p, n, stride=CMP_STRIDE), :] for p in range(CMP_STRIDE)], axis=1)
        top = jnp.dot((ch + pe_ref[kv, 0]).astype(MXU_DTYPE), w1_ref[kv, 0], preferred_element_type=jnp.float32)
        bot = jnp.dot((ch + pe_ref[kv, 1]).astype(MXU_DTYPE), w1_ref[kv, 1], preferred_element_type=jnp.float32)
        hid = top + pltpu.roll(bot, n - 1, 0)
        act = _gelu_tanh(hid).astype(MXU_DTYPE)
        dst[0] = jnp.dot(act, w2_ref[kv], preferred_element_type=jnp.float32).astype(dst.dtype)


def _compress(kc, vc, pe, w1, w2):
    B, S, width = kc.shape
    n = S // CMP_STRIDE
    src = pl.BlockSpec((1, S, width), lambda b: (b, 0, 0))
    full = lambda a: pl.BlockSpec(a.shape, lambda b: (0,) * a.ndim)
    dst = pl.BlockSpec((1, n, KV_WIDTH), lambda b: (b, 0, 0))
    return pl.pallas_call(
        _compress_kernel,
        grid=(B,),
        in_specs=[src, src, full(pe), full(w1), full(w2)],
        out_specs=[dst, dst],
        out_shape=[jax.ShapeDtypeStruct((B, n, KV_WIDTH), MXU_DTYPE)] * 2,
        name="compress",
        compiler_params=_params("parallel"),
    )(kc, vc, pe, w1, w2)


def _select_bias(score, *, live):
    nsb, tq = score.shape
    groups = [score[SUBLANES * k:SUBLANES * (k + 1)] for k in range(live // SUBLANES)]
    rank = [jnp.zeros((SUBLANES, tq), jnp.int32) for _ in groups]
    sub = lax.broadcasted_iota(jnp.int32, (SUBLANES, tq), 0)
    for i in range(live):
        row = score[i:i + 1, :]
        for k, grp in enumerate(groups):
            if SUBLANES * k > i:
                beats = jnp.where(row >= grp, 1, 0)
            elif SUBLANES * k + SUBLANES - 1 <= i:
                beats = jnp.where(row > grp, 1, 0)
            else:
                beats = jnp.where(sub > i - SUBLANES * k, jnp.where(row >= grp, 1, 0), jnp.where(row > grp, 1, 0))
            rank[k] = rank[k] + beats
    bias = [jnp.where(jnp.concatenate(rank, axis=0) < min(SLC_TOPK, nsb), 0.0, NEG)]
    if live < nsb:
        bias.append(jnp.full((nsb - live, tq), NEG, jnp.float32))
    return jnp.concatenate(bias, axis=0)


def _cmp_kernel(q_ref, ck_ref, cv_ref, o_ref, sel_ref, *, tq, nsb):
    q0 = pl.program_id(1) * tq
    ck, cv = ck_ref[0], cv_ref[0]
    ncp = ck.shape[0]
    qh = _split_heads(q_ref[0])
    c = lax.broadcasted_iota(jnp.int32, (ncp, tq), 0)
    t = q0 + lax.broadcasted_iota(jnp.int32, (ncp, tq), 1)
    valid = c * CMP_STRIDE + (CMP_LEN - 1) <= t

    jo = lax.broadcasted_iota(jnp.int32, (nsb, ncp), 0)
    co = lax.broadcasted_iota(jnp.int32, (nsb, ncp), 1)
    overlap = ((co * CMP_STRIDE <= jo * SLC_BLOCK + SLC_BLOCK - 1)
               & (co * CMP_STRIDE + CMP_LEN - 1 >= jo * SLC_BLOCK) & (co < ncp - 1)).astype(MXU_DTYPE)

    jb = lax.broadcasted_iota(jnp.int32, (nsb, tq), 0)
    tb = q0 + lax.broadcasted_iota(jnp.int32, (nsb, tq), 1)
    blk = tb // SLC_BLOCK
    forced = (jb == 0) | (jb == blk) | (jb == blk - 1)
    causal = jb * SLC_BLOCK <= tb

    has_valid = (q0 + lax.broadcasted_iota(jnp.int32, (1, GROUP * tq), 1) % tq >= CMP_LEN - 1).astype(jnp.float32)
    rankers = [functools.partial(_select_bias, live=live) for live in range(RANK_STEP, nsb + 1, RANK_STEP)]
    which = jnp.minimum((q0 + tq - 1) // (RANK_STEP * SLC_BLOCK), len(rankers) - 1)

    o_t, sel_t = [], []
    for h in range(KV_HEADS):
        s = _dot_nt(ck, qh[h])
        s = jnp.concatenate([jnp.where(valid, s[:, g * tq:(g + 1) * tq], NEG) for g in range(GROUP)], axis=1)
        e = jnp.exp2(s - jnp.max(s, axis=0, keepdims=True))
        p = e * (has_valid / jnp.maximum(jnp.sum(e, axis=0, keepdims=True), 1e-30))
        o_t.append(_dot_tn(cv, p.astype(MXU_DTYPE)))

        pg = p[:, 0:tq] + p[:, tq:2 * tq] + p[:, 2 * tq:3 * tq] + p[:, 3 * tq:4 * tq]
        hi = pg.astype(MXU_DTYPE)
        lo = (pg - hi.astype(jnp.float32)).astype(MXU_DTYPE)
        imp = (jnp.dot(overlap, hi, preferred_element_type=jnp.float32)
               + jnp.dot(overlap, lo, preferred_element_type=jnp.float32))
        score = jnp.where(forced, BIG, jnp.where(causal, imp, NEG))
        sel_t.append(lax.switch(which, rankers, score))

    o_ref[0] = _merge_heads(o_t, tq)
    pad = [jnp.zeros((LANES - KV_HEADS * nsb, tq), jnp.float32)] if KV_HEADS * nsb < LANES else []
    sel_ref[0] = jnp.concatenate(sel_t + pad, axis=0).T.astype(sel_ref.dtype)


def _cmp_attention(qn, ck, cv):
    B, S, _ = qn.shape
    tq = min(CMP_Q, S)
    nsb = S // SLC_BLOCK
    ncp = ck.shape[1]
    return pl.pallas_call(
        functools.partial(_cmp_kernel, tq=tq, nsb=nsb),
        grid=(B, S // tq),
        in_specs=[pl.BlockSpec((1, tq, Q_WIDTH), lambda b, i: (b, i, 0)),
                  pl.BlockSpec((1, ncp, KV_WIDTH), lambda b, i: (b, 0, 0)),
                  pl.BlockSpec((1, ncp, KV_WIDTH), lambda b, i: (b, 0, 0))],
        out_specs=[pl.BlockSpec((1, tq, Q_WIDTH), lambda b, i: (b, i, 0)),
                   pl.BlockSpec((1, tq, LANES), lambda b, i: (b, i, 0))],
        out_shape=[jax.ShapeDtypeStruct((B, S, Q_WIDTH), jnp.float32),
                   jax.ShapeDtypeStruct((B, S, LANES), MXU_DTYPE)],
        name="cmp_select",
        compiler_params=_params("parallel", "parallel"),
    )(qn, ck, cv)


def _slc_kernel(q_ref, k_ref, v_ref, sel_ref, o_ref, *scratch, tq, nsb):
    nheads = KV_HEADS * GROUP
    m_scr, acc_scr, s_scr = (scratch[i * nheads:(i + 1) * nheads] for i in range(3))
    qi = pl.program_id(1)
    bias = sel_ref[0]
    col = lax.broadcasted_iota(jnp.int32, bias.shape, 1)
    zero = jnp.zeros_like(bias)
    q_aug = []
    for h, qs in enumerate(_split_heads(q_ref[0])):
        bias_h = jnp.where((col >= h * nsb) & (col < (h + 1) * nsb), bias, zero)
        q_aug += [jnp.concatenate([qs[g * tq:(g + 1) * tq], bias_h], axis=1) for g in range(GROUP)]
    for n in range(nheads):
        m_scr[n][...] = jnp.full(m_scr[n].shape, NEG, jnp.float32)
        acc_scr[n][...] = jnp.zeros(acc_scr[n].shape, jnp.float32)
    erow = lax.broadcasted_iota(jnp.int32, (tq, LANES), 0) // SLC_BLOCK
    ecol = lax.broadcasted_iota(jnp.int32, (tq, LANES), 1)
    ecol = jnp.where(ecol < KV_HEADS * nsb, ecol % nsb, -1)

    def keys(kt):
        k0 = pl.multiple_of(kt * tq, tq)
        onehot = jnp.where(ecol == erow + kt * (tq // SLC_BLOCK), 1.0, 0.0).astype(MXU_DTYPE)
        return jnp.concatenate([k_ref[0, pl.ds(k0, tq), :], onehot], axis=1)

    def scores(n, k_aug):
        s_scr[n][...] = _dot_nt(k_aug, q_aug[n])

    def consume(n, s, v):
        m_prev = m_scr[n][...]
        m_new = jnp.maximum(m_prev, jnp.max(s, axis=0, keepdims=True))
        alpha = jnp.exp2(m_prev - m_new)
        p = jnp.exp2(s - m_new)
        acc_scr[n][...] = alpha * acc_scr[n][...] + _dot_tn(v[n // GROUP], p.astype(MXU_DTYPE))
        m_scr[n][...] = m_new

    def body(kt, carry):
        v = _values_with_ones(v_ref[0, pl.ds(pl.multiple_of(kt * tq, tq), tq), :])
        k_next = keys(kt + 1)
        s_cur = s_scr[0][...]
        scores(0, k_next)
        for n in range(nheads):
            s = s_cur
            if n + 1 < nheads:
                s_cur = s_scr[n + 1][...]
                scores(n + 1, k_next)
            consume(n, s, v)
        return carry

    k_first = keys(0)
    for n in range(nheads):
        scores(n, k_first)
    lax.fori_loop(0, qi, body, 0)
    causal = lax.broadcasted_iota(jnp.int32, (tq, tq), 0) <= lax.broadcasted_iota(jnp.int32, (tq, tq), 1)
    v_diag = _values_with_ones(v_ref[0, pl.ds(pl.multiple_of(qi * tq, tq), tq), :])
    for n in range(nheads):
        consume(n, jnp.where(causal, s_scr[n][...], NEG), v_diag)
    o_t = []
    for h in range(KV_HEADS):
        den_row = (1 - h) * HEAD_DIM
        o_t.append(jnp.concatenate([acc_scr[n][...] / acc_scr[n][den_row:den_row + 1, :]
                                    for n in range(h * GROUP, (h + 1) * GROUP)], axis=1))
    o_ref[0] = _merge_heads(o_t, tq)


def _slc_attention(qn, kv, sel):
    B, S, _ = qn.shape
    tq = min(SLC_Q, S)
    nsb = S // SLC_BLOCK
    return pl.pallas_call(
        functools.partial(_slc_kernel, tq=tq, nsb=nsb),
        grid=(B, S // tq),
        in_specs=[pl.BlockSpec((1, tq, Q_WIDTH), lambda b, i: (b, i, 0)),
                  pl.BlockSpec((1, S, KV_WIDTH), lambda b, i: (b, 0, 0)),
                  pl.BlockSpec((1, S, KV_WIDTH), lambda b, i: (b, 0, 3)),
                  pl.BlockSpec((1, tq, LANES), lambda b, i: (b, i, 0))],
        out_specs=pl.BlockSpec((1, tq, Q_WIDTH), lambda b, i: (b, i, 0)),
        out_shape=jax.ShapeDtypeStruct((B, S, Q_WIDTH), jnp.float32),
        scratch_shapes=([pltpu.VMEM((1, tq), jnp.float32)] * (KV_HEADS * GROUP)
                        + [pltpu.VMEM((KV_WIDTH, tq), jnp.float32)] * (KV_HEADS * GROUP)
                        + [pltpu.VMEM((tq, tq), jnp.float32)] * (KV_HEADS * GROUP)),
        name="slc",
        compiler_params=_params("parallel", "parallel"),
    )(qn, kv, kv, sel)


def _band_kernel(*refs, tq, window, sinks):
    if sinks:
        sink_ref, q_ref, k_ref, v_ref, o_ref = refs
    else:
        q_ref, k_ref, v_ref, o_ref = refs
    q0 = pl.program_id(1) * tq
    span = window + tq
    nsub, nhalf, wsub = span // LANES, tq // LANES, window // LANES
    nheads = KV_HEADS * GROUP
    heads = [qs[g * tq:(g + 1) * tq] for qs in _split_heads(q_ref[0]) for g in range(GROUP)]
    kl = lax.broadcasted_iota(jnp.int32, (LANES, LANES), 0)
    ql = lax.broadcasted_iota(jnp.int32, (LANES, LANES), 1)

    def attend(start, mask_of, lookahead):
        k = k_ref[0, pl.ds(start, span), :]
        v = _values_with_ones(v_ref[0, pl.ds(start, span), :])

        def finish(n, s):
            cols, sink_terms = [], []
            den_row = (1 - n // GROUP) * HEAD_DIM
            for c in range(nhalf):
                live = {}
                for r in range(nsub):
                    mask = mask_of(r, c)
                    if mask is False:
                        continue
                    blk = s[r * LANES:(r + 1) * LANES, c * LANES:(c + 1) * LANES]
                    live[r] = blk if mask is None else jnp.where(mask, blk, NEG)
                m = functools.reduce(jnp.maximum, [jnp.max(b, axis=0, keepdims=True) for b in live.values()])
                if sinks:
                    sk = jnp.full((1, LANES), sink_ref[n] * LOG2E, jnp.float32)
                    m = jnp.maximum(m, sk)
                e = {r: jnp.exp2(b - m) for r, b in live.items()}
                if sinks:
                    sink_terms.append(jnp.exp2(sk - m))
                zero = jnp.zeros((LANES, LANES), jnp.float32)
                cols.append(jnp.concatenate([e.get(r, zero) for r in range(nsub)], axis=0))
            p = jnp.concatenate(cols, axis=1).astype(MXU_DTYPE)
            pv = _dot_tn(v[n // GROUP], p)
            den = pv[den_row:den_row + 1, :]
            if sinks:
                den = den + jnp.concatenate(sink_terms, axis=1)
            return pv / den

        pending = [_dot_nt(k, heads[n]) for n in range(lookahead)]
        outs = []
        for n in range(nheads):
            if n + lookahead < nheads:
                pending.append(_dot_nt(k, heads[n + lookahead]))
            outs.append(finish(n, pending[n]))
        o_ref[0] = _merge_heads([jnp.concatenate(outs[h * GROUP:(h + 1) * GROUP], axis=1) for h in range(KV_HEADS)], tq)

    @pl.when(q0 >= window)
    def _():
        def mask_of(r, c):
            if r == c:
                return kl > ql
            if r == c + wsub:
                return kl <= ql
            return None if c < r < c + wsub else False
        attend(pl.multiple_of(q0 - window, LANES), mask_of, 2)

    @pl.when(q0 < window)
    def _():
        def mask_of(r, c):
            kpos, qpos = r * LANES + kl, q0 + c * LANES + ql
            return (kpos <= qpos) & (kpos > qpos - window)
        attend(0, mask_of, 1)


def _band_attention(q, kv, k_col, v_col, window, sinks):
    B, S, _ = q.shape
    tq = min(BAND_Q, S)
    in_specs = [pl.BlockSpec((1, tq, Q_WIDTH), lambda b, i: (b, i, 0)),
                pl.BlockSpec((1, S, KV_WIDTH), lambda b, i: (b, 0, k_col)),
                pl.BlockSpec((1, S, KV_WIDTH), lambda b, i: (b, 0, v_col))]
    args = [q, kv, kv]
    if sinks is not None:
        in_specs = [pl.BlockSpec(memory_space=pltpu.SMEM)] + in_specs
        args = [sinks] + args
    return pl.pallas_call(
        functools.partial(_band_kernel, tq=tq, window=window, sinks=sinks is not None),
        grid=(B, S // tq),
        in_specs=in_specs,
        out_specs=pl.BlockSpec((1, tq, Q_WIDTH), lambda b, i: (b, i, 0)),
        out_shape=jax.ShapeDtypeStruct((B, S, Q_WIDTH), jnp.float32),
        name="band_sink" if sinks is not None else "band",
        compiler_params=_params("parallel", "parallel"),
    )(*args)


def _mix_kernel(x_ref, oc_ref, os_ref, ow_ref, osw_ref, gl_ref, ge_ref, w_ref, gn_ref, gs_ref, gp_ref, h_ref):
    sig = 1.0 / (1.0 + jnp.exp(-gl_ref[0]))
    hi = sig.astype(MXU_DTYPE)
    lo = (sig - hi.astype(jnp.float32)).astype(MXU_DTYPE)
    gates = jnp.dot(jnp.concatenate([hi, lo], axis=1), ge_ref[...],
                    preferred_element_type=jnp.float32)
    o_nsa = (gates[:, 0:Q_WIDTH] * oc_ref[0] + gates[:, Q_WIDTH:2 * Q_WIDTH] * os_ref[0]
             + gates[:, 2 * Q_WIDTH:3 * Q_WIDTH] * ow_ref[0])
    cat = jnp.concatenate([_rms(o_nsa, gn_ref[...]), _rms(osw_ref[0], gs_ref[...])], axis=1).astype(MXU_DTYPE)
    mixed = jnp.dot(cat, w_ref[...], preferred_element_type=jnp.float32)
    h_ref[0] = x_ref[0] + _rms(mixed, gp_ref[...])


def _mix(x, o_cmp, o_slc, o_win, o_swa, gl, gate_expand, w_out, g_nsa, g_swa, g_post):
    B, S, D = x.shape
    ts = min(MIX_ROWS, S)
    row = lambda width: pl.BlockSpec((1, ts, width), lambda b, i: (b, i, 0))
    full = lambda a: pl.BlockSpec(a.shape, lambda b, i: (0,) * a.ndim)
    return pl.pallas_call(
        _mix_kernel,
        grid=(B, S // ts),
        in_specs=[row(D), row(Q_WIDTH), row(Q_WIDTH), row(Q_WIDTH), row(Q_WIDTH), row(LANES),
                  full(gate_expand), full(w_out), full(g_nsa), full(g_swa), full(g_post)],
        out_specs=row(D),
        out_shape=jax.ShapeDtypeStruct((B, S, D), jnp.float32),
        name="mix",
        compiler_params=_params("parallel", "parallel"),
    )(x, o_cmp, o_slc, o_win, o_swa, gl, gate_expand, w_out, g_nsa, g_swa, g_post)


def _ffn_kernel(h_ref, halo_ref, gpre_ref, wup_ref, cw_ref, cb_ref, wdn_ref, gpost_ref, y_ref, act_scr, perm_scr,
                *, ts, d_ff, chunk):
    groups = ts // SUBLANES
    nblk = perm_scr.shape[0]
    h = h_ref[0]
    hn = _rms(h, gpre_ref[...])
    for c in range(nblk):
        perm_scr[c] = hn[:, c * LANES:(c + 1) * LANES]
    hn = jnp.concatenate([jnp.concatenate([perm_scr[c, pl.ds(a, SUBLANES, stride=groups), :] for c in range(nblk)],
                                          axis=1) for a in range(groups)], axis=0)
    live = (pl.program_id(1) > 0).astype(jnp.float32)
    hn = jnp.concatenate([_rms(halo_ref[0], gpre_ref[...]) * live, hn], axis=0).astype(MXU_DTYPE)
    sub = lax.broadcasted_iota(jnp.int32, (SUBLANES, chunk), 0)

    def conv(col):
        u = jnp.dot(hn, wup_ref[:, col:col + chunk], preferred_element_type=jnp.float32)
        halo, u = u[:HALO], u[HALO:]
        t1 = jnp.where(sub == 0, halo[HALO - 1:HALO], pltpu.roll(u[ts - SUBLANES:], 1, 0))
        t2 = jnp.where(sub == 0, halo[HALO - 2:HALO - 1], pltpu.roll(u[ts - 2 * SUBLANES:ts - SUBLANES], 1, 0))
        tap1 = jnp.concatenate([t1, u[:ts - SUBLANES]], axis=0)
        tap2 = jnp.concatenate([t2, t1, u[:ts - 2 * SUBLANES]], axis=0)
        w = cw_ref[:, col:col + chunk]
        return w[0:1] * tap2 + w[1:2] * tap1 + w[2:3] * u + cb_ref[:, col:col + chunk]

    for col in range(0, d_ff, chunk):
        act_scr[:, col:col + chunk] = (_gelu_tanh(conv(col)) * conv(d_ff + col)).astype(MXU_DTYPE)
    y = jnp.dot(act_scr[...], wdn_ref[...], preferred_element_type=jnp.float32)
    for a in range(groups):
        for c in range(nblk):
            perm_scr[c, pl.ds(a, SUBLANES, stride=groups), :] = y[SUBLANES * a:SUBLANES * (a + 1),
                                                                  c * LANES:(c + 1) * LANES]
    y = jnp.concatenate([perm_scr[c] for c in range(nblk)], axis=1)
    y_ref[0] = h + _rms(y, gpost_ref[...])


def _ffn(h, g_pre, w_up, conv_w, conv_b, w_down, g_post):
    B, S, D = h.shape
    ts = min(FFN_ROWS, S)
    d_ff = w_down.shape[0]
    full = lambda a: pl.BlockSpec(a.shape, lambda b, i: (0,) * a.ndim, pipeline_mode=pl.Buffered(1))
    return pl.pallas_call(
        functools.partial(_ffn_kernel, ts=ts, d_ff=d_ff, chunk=FFN_CHUNK),
        grid=(B, S // ts),
        in_specs=[pl.BlockSpec((1, ts, D), lambda b, i: (b, i, 0)),
                  pl.BlockSpec((1, HALO, D), lambda b, i: (b, jnp.maximum(i * (ts // HALO) - 1, 0), 0)),
                  full(g_pre), full(w_up), full(conv_w), full(conv_b), full(w_down), full(g_post)],
        out_specs=pl.BlockSpec((1, ts, D), lambda b, i: (b, i, 0)),
        out_shape=jax.ShapeDtypeStruct((B, S, D), jnp.float32),
        scratch_shapes=[pltpu.VMEM((ts, d_ff), MXU_DTYPE), pltpu.VMEM((D // LANES, ts, LANES), jnp.float32)],
        name="ffn",
        compiler_params=_params("parallel", "parallel"),
    )(h, h, g_pre, w_up, conv_w, conv_b, w_down, g_post)


def _rope_tables(S):
    half = HEAD_DIM // 2
    inv = ROPE_THETA ** (-jnp.arange(half, dtype=jnp.float32) / half)
    ang = jnp.arange(S).astype(jnp.float32)[:, None] * inv[None, :]
    cos, sin = jnp.cos(ang), jnp.sin(ang)
    reps = LANES // HEAD_DIM
    return jnp.tile(jnp.concatenate([cos, cos], axis=1), (1, reps)), jnp.tile(jnp.concatenate([-sin, sin], axis=1), (1, reps))


def _layer(x, w_in, w_out, attn_pre_norm, attn_post_norm, nsa_out_norm, swa_out_norm, cmp_pos, cmp_w1, cmp_w2,
           swa_sinks, ffn_pre_norm, ffn_post_norm, w_up, conv_w, conv_b, w_down):
    B, S, D = x.shape
    f32 = jnp.float32
    perm = _q_perm()
    sizes = [Q_WIDTH] + [KV_WIDTH] * 6 + [KV_HEADS * GROUP * NSA_BRANCHES] + [Q_WIDTH, KV_WIDTH, KV_WIDTH]
    offs = np.concatenate([[0], np.cumsum(sizes)])
    col = lambda i: w_in[:, offs[i]:offs[i + 1]]
    q_n, k_c, v_c, k_s, v_s, k_w, v_w, g_n, q_w, k_sw, v_sw = (col(i) for i in range(11))
    g_pad = jnp.pad(g_n, ((0, 0), (0, LANES - g_n.shape[1])))
    w_all = jnp.concatenate([q_n[:, perm], q_w[:, perm], k_c, k_s, k_w, k_sw, v_c, v_s, v_w, v_sw, g_pad],
                            axis=1).astype(MXU_DTYPE)
    cos, sin = _rope_tables(S)
    row = lambda g: g.reshape(1, -1).astype(f32)

    qn, qw, kc, vc, kv, gl = _proj(x, row(attn_pre_norm), w_all, cos, sin)

    eye = jnp.eye(KV_HEADS, dtype=f32)
    pe = jnp.broadcast_to(cmp_pos.reshape(2, 2, CMP_STRIDE, 1, HEAD_DIM), (2, 2, CMP_STRIDE, KV_HEADS, HEAD_DIM))
    pe = pe.reshape(2, 2, 1, CMP_STRIDE * KV_WIDTH)
    w1 = cmp_w1.reshape(2, 2, CMP_STRIDE, HEAD_DIM, CMP_HIDDEN)
    w1x = jnp.einsum('kapdm,hg->kaphdgm', w1, eye).reshape(2, 2, CMP_STRIDE * KV_WIDTH, KV_HEADS * CMP_HIDDEN)
    w2x = jnp.einsum('kmd,hg->khmgd', cmp_w2, eye).reshape(2, KV_HEADS * CMP_HIDDEN, KV_WIDTH)
    ck, cv = _compress(kc, vc, pe, w1x.astype(MXU_DTYPE), w2x.astype(MXU_DTYPE))

    o_cmp, sel = _cmp_attention(qn, ck, cv)
    o_slc = _slc_attention(qn, kv, sel)
    o_win = _band_attention(qn, kv, 1, 4, NSA_WINDOW, None)
    o_swa = _band_attention(qw, kv, 2, 5, SWA_WINDOW, swa_sinks.astype(f32))

    n = np.arange(NSA_BRANCHES * Q_WIDTH)
    br, g, h = n // Q_WIDTH, (n % Q_WIDTH) // LANES, (n % LANES) // HEAD_DIM
    expand = np.zeros((LANES, NSA_BRANCHES * Q_WIDTH), np.float32)
    expand[(h * GROUP + g) * NSA_BRANCHES + br, n] = 1.0
    expand = np.concatenate([expand, expand], axis=0)
    w_out_p = jnp.concatenate([w_out[perm], w_out[Q_WIDTH + perm]], axis=0).astype(MXU_DTYPE)
    hmid = _mix(x, o_cmp, o_slc, o_win, o_swa, gl, jnp.asarray(expand, MXU_DTYPE), w_out_p,
                row(nsa_out_norm[perm]), row(swa_out_norm[perm]), row(attn_post_norm))

    return _ffn(hmid, row(ffn_pre_norm), w_up.astype(MXU_DTYPE), conv_w.astype(f32), row(conv_b),
                w_down.astype(MXU_DTYPE), row(ffn_post_norm))


def kernel(x, w_in, w_out, attn_pre_norm, attn_post_norm, nsa_out_norm, swa_out_norm, cmp_pos, cmp_w1, cmp_w2,
           swa_sinks, ffn_pre_norm, ffn_post_norm, w_up, conv_w, conv_b, w_down):
    h = x
    for l in range(w_in.shape[0]):
        h = _layer(h, w_in[l], w_out[l], attn_pre_norm[l], attn_post_norm[l], nsa_out_norm[l], swa_out_norm[l],
                   cmp_pos[l], cmp_w1[l], cmp_w2[l], swa_sinks[l], ffn_pre_norm[l], ffn_post_norm[l],
                   w_up[l], conv_w[l], conv_b[l], w_down[l])
    return h
```

```python
import functools

import numpy as np
import jax
import jax.numpy as jnp
from jax import lax
from jax.experimental import pallas as pl
from jax.experimental.pallas import tpu as pltpu

HEAD_DIM = 64
KV_HEADS = 2
GROUP = 4
Q_WIDTH = KV_HEADS * GROUP * HEAD_DIM
KV_WIDTH = KV_HEADS * HEAD_DIM
SWA_WINDOW = 128
NSA_WINDOW = 512
CMP_LEN = 32
CMP_STRIDE = 16
CMP_HIDDEN = 256
SLC_BLOCK = 64
SLC_TOPK = 16
NSA_BRANCHES = 3
D_FF = 2816
ROPE_THETA = 10000.0
RMS_EPS = 1e-6
NEG = -1e30
BIG = 1e9
LOG2E = float(np.log2(np.e))

LANES = 128
SUBLANES = 8
HALO = 8
MXU_DTYPE = jnp.bfloat16
VMEM_LIMIT = 56 * 1024 * 1024

PROJ_ROWS = 512
CMP_Q = 256
SLC_Q = 256
SLC_K = 256
BAND_Q = 256
MIX_ROWS = 512
FFN_ROWS = 1024
FFN_CHUNK = 256
RANK_STEP = 16

_N_ROPE_SLABS = 12
_N_QSLABS = 8
PROJ_WIDTH = 2 * Q_WIDTH + 9 * LANES


def _q_perm():
    n = np.arange(Q_WIDTH)
    g, h, d = n // 128, (n % 128) // 64, n % 64
    return (h * GROUP + g) * HEAD_DIM + d


def _params(*sem):
    return pltpu.CompilerParams(dimension_semantics=sem, vmem_limit_bytes=VMEM_LIMIT)


def _rms(xf, g):
    return xf * lax.rsqrt(jnp.mean(xf * xf, axis=-1, keepdims=True) + RMS_EPS) * g


def _gelu_tanh(x):
    return 0.5 * x * (1.0 + jnp.tanh(np.sqrt(2.0 / np.pi).astype(np.float32) * (x + 0.044715 * (x * x * x))))


def _dot_nt(a, b):
    return lax.dot_general(a, b, (((1,), (1,)), ((), ())), preferred_element_type=jnp.float32)


def _dot_tn(a, b):
    return lax.dot_general(a, b, (((0,), (0,)), ((), ())), preferred_element_type=jnp.float32)


def _split_heads(q):
    qs = jnp.concatenate([q[:, g * LANES:(g + 1) * LANES] for g in range(GROUP)], axis=0)
    lane = lax.broadcasted_iota(jnp.int32, qs.shape, 1)
    zero = jnp.zeros_like(qs)
    return [jnp.where((lane >= h * HEAD_DIM) & (lane < (h + 1) * HEAD_DIM), qs, zero) for h in range(KV_HEADS)]


def _values_with_ones(v):
    lane = lax.broadcasted_iota(jnp.int32, v.shape, 1)
    one = jnp.ones_like(v)
    return [jnp.where((lane >= h * HEAD_DIM) & (lane < (h + 1) * HEAD_DIM), v, one) for h in range(KV_HEADS)]


def _merge_heads(o_t, tq):
    sub = lax.broadcasted_iota(jnp.int32, o_t[0].shape, 0)
    both = jnp.where(sub < HEAD_DIM, o_t[0], o_t[1])
    return jnp.concatenate([both[:, g * tq:(g + 1) * tq].T for g in range(GROUP)], axis=1)


def _proj_kernel(x_ref, g_ref, w_ref, cos_ref, sin_ref, qn_ref, qw_ref, kc_ref, vc_ref, kv_ref, gl_ref):
    hn = _rms(x_ref[0], g_ref[...]).astype(MXU_DTYPE)
    p = jnp.dot(hn, w_ref[...], preferred_element_type=jnp.float32)
    cos, sin = cos_ref[...], sin_ref[...]
    lane = lax.broadcasted_iota(jnp.int32, cos.shape, 1)
    first_half = (lane % HEAD_DIM) < (HEAD_DIM // 2)

    def slab(j, rope):
        z = p[:, j * LANES:(j + 1) * LANES]
        if rope:
            swapped = jnp.where(first_half, pltpu.roll(z, LANES - HEAD_DIM // 2, 1), pltpu.roll(z, HEAD_DIM // 2, 1))
            z = z * cos + swapped * sin
        return z

    scale = HEAD_DIM ** -0.5 * LOG2E
    for j in range(4):
        qn_ref[0, :, j * LANES:(j + 1) * LANES] = (slab(j, True) * scale).astype(qn_ref.dtype)
        qw_ref[0, :, j * LANES:(j + 1) * LANES] = (slab(4 + j, True) * scale).astype(qw_ref.dtype)
    kc_ref[0] = slab(8, True)
    vc_ref[0] = slab(12, False)
    for j in range(3):
        kv_ref[0, :, j * LANES:(j + 1) * LANES] = slab(9 + j, True).astype(kv_ref.dtype)
    for j in range(3):
        kv_ref[0, :, (3 + j) * LANES:(4 + j) * LANES] = slab(13 + j, False).astype(kv_ref.dtype)
    gl_ref[0] = slab(16, False)


def _proj(x, gain, w, cos, sin):
    B, S, D = x.shape
    ts = min(PROJ_ROWS, S)
    row = lambda width: pl.BlockSpec((1, ts, width), lambda b, i: (b, i, 0))
    full = lambda a: pl.BlockSpec(a.shape, lambda b, i: (0,) * a.ndim)
    tab = pl.BlockSpec((ts, LANES), lambda b, i: (i, 0))
    f32, bf = jnp.float32, MXU_DTYPE
    return pl.pallas_call(
        _proj_kernel,
        grid=(B, S // ts),
        in_specs=[row(D), full(gain), full(w), tab, tab],
        out_specs=[row(Q_WIDTH), row(Q_WIDTH), row(LANES), row(LANES), row(6 * LANES), row(LANES)],
        out_shape=[jax.ShapeDtypeStruct((B, S, Q_WIDTH), bf), jax.ShapeDtypeStruct((B, S, Q_WIDTH), bf),
                   jax.ShapeDtypeStruct((B, S, LANES), f32), jax.ShapeDtypeStruct((B, S, LANES), f32),
                   jax.ShapeDtypeStruct((B, S, 6 * LANES), bf), jax.ShapeDtypeStruct((B, S, LANES), f32)],
        name="proj",
        compiler_params=_params("parallel", "parallel"),
    )(x, gain, w, cos, sin)


def _compress_kernel(kc_ref, vc_ref, pe_ref, w1_ref, w2_ref, ck_ref, cv_ref):
    n = kc_ref.shape[1] // CMP_STRIDE
    for kv, (src, dst) in enumerate(((kc_ref, ck_ref), (vc_ref, cv_ref))):
        ch = jnp.concatenate([src[0, pl.ds(p, n, stride=CMP_STRIDE), :] for p in range(CMP_STRIDE)], axis=1)
        top = jnp.dot((ch + pe_ref[kv, 0]).astype(MXU_DTYPE), w1_ref[kv, 0], preferred_element_type=jnp.float32)
        bot = jnp.dot((ch + pe_ref[kv, 1]).astype(MXU_DTYPE), w1_ref[kv, 1], preferred_element_type=jnp.float32)
        hid = top + pltpu.roll(bot, n - 1, 0)
        act = _gelu_tanh(hid).astype(MXU_DTYPE)
        dst[0] = jnp.dot(act, w2_ref[kv], preferred_element_type=jnp.float32).astype(dst.dtype)


def _compress(kc, vc, pe, w1, w2):
    B, S, width = kc.shape
    n = S // CMP_STRIDE
    src = pl.BlockSpec((1, S, width), lambda b: (b, 0, 0))
    full = lambda a: pl.BlockSpec(a.shape, lambda b: (0,) * a.ndim)
    dst = pl.BlockSpec((1, n, KV_WIDTH), lambda b: (b, 0, 0))
    return pl.pallas_call(
        _compress_kernel,
        grid=(B,),
        in_specs=[src, src, full(pe), full(w1), full(w2)],
        out_specs=[dst, dst],
        out_shape=[jax.ShapeDtypeStruct((B, n, KV_WIDTH), MXU_DTYPE)] * 2,
        name="compress",
        compiler_params=_params("parallel"),
    )(kc, vc, pe, w1, w2)


def _select_bias(score, *, live):
    nsb, tq = score.shape
    groups = [score[SUBLANES * k:SUBLANES * (k + 1)] for k in range(live // SUBLANES)]
    rank = [jnp.zeros((SUBLANES, tq), jnp.int32) for _ in groups]
    sub = lax.broadcasted_iota(jnp.int32, (SUBLANES, tq), 0)
    for i in range(live):
        row = score[i:i + 1, :]
        for k, grp in enumerate(groups):
            if SUBLANES * k > i:
                beats = jnp.where(row >= grp, 1, 0)
            elif SUBLANES * k + SUBLANES - 1 <= i:
                beats = jnp.where(row > grp, 1, 0)
            else:
                beats = jnp.where(sub > i - SUBLANES * k, jnp.where(row >= grp, 1, 0), jnp.where(row > grp, 1, 0))
            rank[k] = rank[k] + beats
    bias = [jnp.where(jnp.concatenate(rank, axis=0) < min(SLC_TOPK, nsb), 0.0, NEG)]
    if live < nsb:
        bias.append(jnp.full((nsb - live, tq), NEG, jnp.float32))
    return jnp.concatenate(bias, axis=0)


def _cmp_kernel(q_ref, ck_ref, cv_ref, o_ref, sel_ref, *, tq, nsb):
    q0 = pl.program_id(1) * tq
    ck, cv = ck_ref[0], cv_ref[0]
    ncp = ck.shape[0]
    qh = _split_heads(q_ref[0])
    c = lax.broadcasted_iota(jnp.int32, (ncp, tq), 0)
    t = q0 + lax.broadcasted_iota(jnp.int32, (ncp, tq), 1)
    valid = c * CMP_STRIDE + (CMP_LEN - 1) <= t

    jo = lax.broadcasted_iota(jnp.int32, (nsb, ncp), 0)
    co = lax.broadcasted_iota(jnp.int32, (nsb, ncp), 1)
    overlap = ((co * CMP_STRIDE <= jo * SLC_BLOCK + SLC_BLOCK - 1)
               & (co * CMP_STRIDE + CMP_LEN - 1 >= jo * SLC_BLOCK) & (co < ncp - 1)).astype(MXU_DTYPE)

    jb = lax.broadcasted_iota(jnp.int32, (nsb, tq), 0)
    tb = q0 + lax.broadcasted_iota(jnp.int32, (nsb, tq), 1)
    blk = tb // SLC_BLOCK
    forced = (jb == 0) | (jb == blk) | (jb == blk - 1)
    causal = jb * SLC_BLOCK <= tb

    has_valid = (q0 + lax.broadcasted_iota(jnp.int32, (1, GROUP * tq), 1) % tq >= CMP_LEN - 1).astype(jnp.float32)
    rankers = [functools.partial(_select_bias, live=live) for live in range(RANK_STEP, nsb + 1, RANK_STEP)]
    which = jnp.minimum((q0 + tq - 1) // (RANK_STEP * SLC_BLOCK), len(rankers) - 1)

    o_t, sel_t = [], []
    for h in range(KV_HEADS):
        s = _dot_nt(ck, qh[h])
        s = jnp.concatenate([jnp.where(valid, s[:, g * tq:(g + 1) * tq], NEG) for g in range(GROUP)], axis=1)
        e = jnp.exp2(s - jnp.max(s, axis=0, keepdims=True))
        p = e * (has_valid / jnp.maximum(jnp.sum(e, axis=0, keepdims=True), 1e-30))
        o_t.append(_dot_tn(cv, p.astype(MXU_DTYPE)))

        pg = p[:, 0:tq] + p[:, tq:2 * tq] + p[:, 2 * tq:3 * tq] + p[:, 3 * tq:4 * tq]
        hi = pg.astype(MXU_DTYPE)
        lo = (pg - hi.astype(jnp.float32)).astype(MXU_DTYPE)
        imp = (jnp.dot(overlap, hi, preferred_element_type=jnp.float32)
               + jnp.dot(overlap, lo, preferred_element_type=jnp.float32))
        score = jnp.where(forced, BIG, jnp.where(causal, imp, NEG))
        sel_t.append(lax.switch(which, rankers, score))

    o_ref[0] = _merge_heads(o_t, tq)
    pad = [jnp.zeros((LANES - KV_HEADS * nsb, tq), jnp.float32)] if KV_HEADS * nsb < LANES else []
    sel_ref[0] = jnp.concatenate(sel_t + pad, axis=0).T.astype(sel_ref.dtype)


def _cmp_attention(qn, ck, cv):
    B, S, _ = qn.shape
    tq = min(CMP_Q, S)
    nsb = S // SLC_BLOCK
    ncp = ck.shape[1]
    return pl.pallas_call(
        functools.partial(_cmp_kernel, tq=tq, nsb=nsb),
        grid=(B, S // tq),
        in_specs=[pl.BlockSpec((1, tq, Q_WIDTH), lambda b, i: (b, i, 0)),
                  pl.BlockSpec((1, ncp, KV_WIDTH), lambda b, i: (b, 0, 0)),
                  pl.BlockSpec((1, ncp, KV_WIDTH), lambda b, i: (b, 0, 0))],
        out_specs=[pl.BlockSpec((1, tq, Q_WIDTH), lambda b, i: (b, i, 0)),
                   pl.BlockSpec((1, tq, LANES), lambda b, i: (b, i, 0))],
        out_shape=[jax.ShapeDtypeStruct((B, S, Q_WIDTH), jnp.float32),
                   jax.ShapeDtypeStruct((B, S, LANES), MXU_DTYPE)],
        name="cmp_select",
        compiler_params=_params("parallel", "parallel"),
    )(qn, ck, cv)


def _slc_kernel(q_ref, k_ref, v_ref, sel_ref, qn_ref, kn_ref, seln_ref, o_ref, *scratch, tq, nsb):
    nheads = KV_HEADS * GROUP
    m_scr, acc_scr, s_scr = (scratch[i * nheads:(i + 1) * nheads] for i in range(3))
    qi = pl.program_id(1)
    erow = lax.broadcasted_iota(jnp.int32, (tq, LANES), 0) // SLC_BLOCK
    ecol = lax.broadcasted_iota(jnp.int32, (tq, LANES), 1)
    ecol = jnp.where(ecol < KV_HEADS * nsb, ecol % nsb, -1)

    def queries(q, bias):
        col = lax.broadcasted_iota(jnp.int32, bias.shape, 1)
        zero = jnp.zeros_like(bias)
        out = []
        for h, qs in enumerate(_split_heads(q)):
            bias_h = jnp.where((col >= h * nsb) & (col < (h + 1) * nsb), bias, zero)
            out += [jnp.concatenate([qs[g * tq:(g + 1) * tq], bias_h], axis=1) for g in range(GROUP)]
        return out

    def keys(k, kt):
        onehot = jnp.where(ecol == erow + kt * (tq // SLC_BLOCK), 1.0, 0.0).astype(MXU_DTYPE)
        return jnp.concatenate([k, onehot], axis=1)

    def consume(n, s, v):
        m_prev = m_scr[n][...]
        m_new = jnp.maximum(m_prev, jnp.max(s, axis=0, keepdims=True))
        alpha = jnp.exp2(m_prev - m_new)
        p = jnp.exp2(s - m_new)
        acc_scr[n][...] = alpha * acc_scr[n][...] + _dot_tn(v[n // GROUP], p.astype(MXU_DTYPE))
        m_scr[n][...] = m_new

    def step(kt, q_next, k_next, mask):
        v = _values_with_ones(v_ref[0, pl.ds(pl.multiple_of(kt * tq, tq), tq), :])
        s_cur = s_scr[0][...]
        s_scr[0][...] = _dot_nt(k_next, q_next[0])
        for n in range(nheads):
            s = s_cur
            if n + 1 < nheads:
                s_cur = s_scr[n + 1][...]
                s_scr[n + 1][...] = _dot_nt(k_next, q_next[n + 1])
            consume(n, s if mask is None else jnp.where(mask, s, NEG), v)

    q_aug = queries(q_ref[0], sel_ref[0])
    for n in range(nheads):
        m_scr[n][...] = jnp.full(m_scr[n].shape, NEG, jnp.float32)
        acc_scr[n][...] = jnp.zeros(acc_scr[n].shape, jnp.float32)

    @pl.when((pl.program_id(0) == 0) & (qi == 0))
    def _():
        k_first = keys(k_ref[0, pl.ds(0, tq), :], 0)
        for n in range(nheads):
            s_scr[n][...] = _dot_nt(k_first, q_aug[n])

    def inner(kt):
        step(kt, q_aug, keys(k_ref[0, pl.ds(pl.multiple_of((kt + 1) * tq, tq), tq), :], kt + 1), None)

    def pair(i, carry):
        inner(2 * i)
        inner(2 * i + 1)
        return carry

    lax.fori_loop(0, qi // 2, pair, 0)

    @pl.when(qi % 2 == 1)
    def _():
        inner(qi - 1)

    causal = lax.broadcasted_iota(jnp.int32, (tq, tq), 0) <= lax.broadcasted_iota(jnp.int32, (tq, tq), 1)
    step(qi, queries(qn_ref[0], seln_ref[0]), keys(kn_ref[0], 0), causal)
    o_t = []
    for h in range(KV_HEADS):
        den_row = (1 - h) * HEAD_DIM
        o_t.append(jnp.concatenate([acc_scr[n][...] / acc_scr[n][den_row:den_row + 1, :]
                                    for n in range(h * GROUP, (h + 1) * GROUP)], axis=1))
    o_ref[0] = _merge_heads(o_t, tq)


def _slc_attention(qn, kv, sel):
    B, S, _ = qn.shape
    tq = min(SLC_Q, S)
    nq = S // tq
    nsb = S // SLC_BLOCK

    def following(b, i):
        wrap = i + 1 == nq
        return jnp.where(wrap, jnp.minimum(b + 1, B - 1), b), jnp.where(wrap, 0, i + 1)

    return pl.pallas_call(
        functools.partial(_slc_kernel, tq=tq, nsb=nsb),
        grid=(B, nq),
        in_specs=[pl.BlockSpec((1, tq, Q_WIDTH), lambda b, i: (b, i, 0)),
                  pl.BlockSpec((1, S, KV_WIDTH), lambda b, i: (b, 0, 0)),
                  pl.BlockSpec((1, S, KV_WIDTH), lambda b, i: (b, 0, 3)),
                  pl.BlockSpec((1, tq, LANES), lambda b, i: (b, i, 0)),
                  pl.BlockSpec((1, tq, Q_WIDTH), lambda b, i: (*following(b, i), 0)),
                  pl.BlockSpec((1, tq, KV_WIDTH), lambda b, i: (following(b, i)[0], 0, 0)),
                  pl.BlockSpec((1, tq, LANES), lambda b, i: (*following(b, i), 0))],
        out_specs=pl.BlockSpec((1, tq, Q_WIDTH), lambda b, i: (b, i, 0)),
        out_shape=jax.ShapeDtypeStruct((B, S, Q_WIDTH), jnp.float32),
        scratch_shapes=([pltpu.VMEM((1, tq), jnp.float32)] * (KV_HEADS * GROUP)
                        + [pltpu.VMEM((KV_WIDTH, tq), jnp.float32)] * (KV_HEADS * GROUP)
                        + [pltpu.VMEM((tq, tq), jnp.float32)] * (KV_HEADS * GROUP)),
        name="slc",
        compiler_params=_params("arbitrary", "arbitrary"),
    )(qn, kv, kv, sel, qn, kv, sel)


def _band_kernel(*refs, tq, window, sinks):
    if sinks:
        sink_ref, q_ref, k_ref, v_ref, o_ref = refs
    else:
        q_ref, k_ref, v_ref, o_ref = refs
    q0 = pl.program_id(1) * tq
    span = window + tq
    nsub, nhalf, wsub = span // LANES, tq // LANES, window // LANES
    nheads = KV_HEADS * GROUP
    heads = [qs[g * tq:(g + 1) * tq] for qs in _split_heads(q_ref[0]) for g in range(GROUP)]
    kl = lax.broadcasted_iota(jnp.int32, (LANES, LANES), 0)
    ql = lax.broadcasted_iota(jnp.int32, (LANES, LANES), 1)

    def attend(start, mask_of, lookahead):
        k = k_ref[0, pl.ds(start, span), :]
        v = _values_with_ones(v_ref[0, pl.ds(start, span), :])

        def finish(n, s):
            cols, sink_terms = [], []
            den_row = (1 - n // GROUP) * HEAD_DIM
            for c in range(nhalf):
                live = {}
                for r in range(nsub):
                    mask = mask_of(r, c)
                    if mask is False:
                        continue
                    blk = s[r * LANES:(r + 1) * LANES, c * LANES:(c + 1) * LANES]
                    live[r] = blk if mask is None else jnp.where(mask, blk, NEG)
                m = functools.reduce(jnp.maximum, [jnp.max(b, axis=0, keepdims=True) for b in live.values()])
                if sinks:
                    sk = jnp.full((1, LANES), sink_ref[n] * LOG2E, jnp.float32)
                    m = jnp.maximum(m, sk)
                e = {r: jnp.exp2(b - m) for r, b in live.items()}
                if sinks:
                    sink_terms.append(jnp.exp2(sk - m))
                zero = jnp.zeros((LANES, LANES), jnp.float32)
                cols.append(jnp.concatenate([e.get(r, zero) for r in range(nsub)], axis=0))
            p = jnp.concatenate(cols, axis=1).astype(MXU_DTYPE)
            pv = _dot_tn(v[n // GROUP], p)
            den = pv[den_row:den_row + 1, :]
            if sinks:
                den = den + jnp.concatenate(sink_terms, axis=1)
            return pv / den

        pending = [_dot_nt(k, heads[n]) for n in range(lookahead)]
        outs = []
        for n in range(nheads):
            if n + lookahead < nheads:
                pending.append(_dot_nt(k, heads[n + lookahead]))
            outs.append(finish(n, pending[n]))
        o_ref[0] = _merge_heads([jnp.concatenate(outs[h * GROUP:(h + 1) * GROUP], axis=1) for h in range(KV_HEADS)], tq)

    @pl.when(q0 >= window)
    def _():
        def mask_of(r, c):
            if r == c:
                return kl > ql
            if r == c + wsub:
                return kl <= ql
            return None if c < r < c + wsub else False
        attend(pl.multiple_of(q0 - window, LANES), mask_of, 2)

    @pl.when(q0 < window)
    def _():
        def mask_of(r, c):
            kpos, qpos = r * LANES + kl, q0 + c * LANES + ql
            return (kpos <= qpos) & (kpos > qpos - window)
        attend(0, mask_of, 1)


def _band_attention(q, kv, k_col, v_col, window, sinks):
    B, S, _ = q.shape
    tq = min(BAND_Q, S)
    in_specs = [pl.BlockSpec((1, tq, Q_WIDTH), lambda b, i: (b, i, 0)),
                pl.BlockSpec((1, S, KV_WIDTH), lambda b, i: (b, 0, k_col)),
                pl.BlockSpec((1, S, KV_WIDTH), lambda b, i: (b, 0, v_col))]
    args = [q, kv, kv]
    if sinks is not None:
        in_specs = [pl.BlockSpec(memory_space=pltpu.SMEM)] + in_specs
        args = [sinks] + args
    return pl.pallas_call(
        functools.partial(_band_kernel, tq=tq, window=window, sinks=sinks is not None),
        grid=(B, S // tq),
        in_specs=in_specs,
        out_specs=pl.BlockSpec((1, tq, Q_WIDTH), lambda b, i: (b, i, 0)),
        out_shape=jax.ShapeDtypeStruct((B, S, Q_WIDTH), jnp.float32),
        name="band_sink" if sinks is not None else "band",
        compiler_params=_params("parallel", "parallel"),
    )(*args)


def _mix_kernel(x_ref, oc_ref, os_ref, ow_ref, osw_ref, gl_ref, ge_ref, w_ref, gn_ref, gs_ref, gp_ref, h_ref):
    sig = 1.0 / (1.0 + jnp.exp(-gl_ref[0]))
    hi = sig.astype(MXU_DTYPE)
    lo = (sig - hi.astype(jnp.float32)).astype(MXU_DTYPE)
    gates = jnp.dot(jnp.concatenate([hi, lo], axis=1), ge_ref[...],
                    preferred_element_type=jnp.float32)
    o_nsa = (gates[:, 0:Q_WIDTH] * oc_ref[0] + gates[:, Q_WIDTH:2 * Q_WIDTH] * os_ref[0]
             + gates[:, 2 * Q_WIDTH:3 * Q_WIDTH] * ow_ref[0])
    cat = jnp.concatenate([_rms(o_nsa, gn_ref[...]), _rms(osw_ref[0], gs_ref[...])], axis=1).astype(MXU_DTYPE)
    mixed = jnp.dot(cat, w_ref[...], preferred_element_type=jnp.float32)
    h_ref[0] = x_ref[0] + _rms(mixed, gp_ref[...])


def _mix(x, o_cmp, o_slc, o_win, o_swa, gl, gate_expand, w_out, g_nsa, g_swa, g_post):
    B, S, D = x.shape
    ts = min(MIX_ROWS, S)
    row = lambda width: pl.BlockSpec((1, ts, width), lambda b, i: (b, i, 0))
    full = lambda a: pl.BlockSpec(a.shape, lambda b, i: (0,) * a.ndim)
    return pl.pallas_call(
        _mix_kernel,
        grid=(B, S // ts),
        in_specs=[row(D), row(Q_WIDTH), row(Q_WIDTH), row(Q_WIDTH), row(Q_WIDTH), row(LANES),
                  full(gate_expand), full(w_out), full(g_nsa), full(g_swa), full(g_post)],
        out_specs=row(D),
        out_shape=jax.ShapeDtypeStruct((B, S, D), jnp.float32),
        name="mix",
        compiler_params=_params("parallel", "parallel"),
    )(x, o_cmp, o_slc, o_win, o_swa, gl, gate_expand, w_out, g_nsa, g_swa, g_post)


def _ffn_kernel(h_ref, halo_ref, gpre_ref, wup_ref, cw_ref, cb_ref, wdn_ref, gpost_ref, y_ref, act_scr,
                *, ts, d_ff, chunk):
    h = h_ref[0]
    hn = pltpu.einshape("(sa)d->(as)d", _rms(h, gpre_ref[...]), s=SUBLANES)
    live = (pl.program_id(1) > 0).astype(jnp.float32)
    hn = jnp.concatenate([_rms(halo_ref[0], gpre_ref[...]) * live, hn], axis=0).astype(MXU_DTYPE)
    sub = lax.broadcasted_iota(jnp.int32, (SUBLANES, chunk), 0)

    def conv(col):
        u = jnp.dot(hn, wup_ref[:, col:col + chunk], preferred_element_type=jnp.float32)
        halo, u = u[:HALO], u[HALO:]
        t1 = jnp.where(sub == 0, halo[HALO - 1:HALO], pltpu.roll(u[ts - SUBLANES:], 1, 0))
        t2 = jnp.where(sub == 0, halo[HALO - 2:HALO - 1], pltpu.roll(u[ts - 2 * SUBLANES:ts - SUBLANES], 1, 0))
        tap1 = jnp.concatenate([t1, u[:ts - SUBLANES]], axis=0)
        tap2 = jnp.concatenate([t2, t1, u[:ts - 2 * SUBLANES]], axis=0)
        w = cw_ref[:, col:col + chunk]
        return w[0:1] * tap2 + w[1:2] * tap1 + w[2:3] * u + cb_ref[:, col:col + chunk]

    for col in range(0, d_ff, chunk):
        act_scr[:, col:col + chunk] = (_gelu_tanh(conv(col)) * conv(d_ff + col)).astype(MXU_DTYPE)
    y = jnp.dot(act_scr[...], wdn_ref[...], preferred_element_type=jnp.float32)
    y = pltpu.einshape("(as)d->(sa)d", y, s=SUBLANES)
    y_ref[0] = h + _rms(y, gpost_ref[...])


def _ffn(h, g_pre, w_up, conv_w, conv_b, w_down, g_post):
    B, S, D = h.shape
    ts = min(FFN_ROWS, S)
    d_ff = w_down.shape[0]
    full = lambda a: pl.BlockSpec(a.shape, lambda b, i: (0,) * a.ndim, pipeline_mode=pl.Buffered(1))
    return pl.pallas_call(
        functools.partial(_ffn_kernel, ts=ts, d_ff=d_ff, chunk=FFN_CHUNK),
        grid=(B, S // ts),
        in_specs=[pl.BlockSpec((1, ts, D), lambda b, i: (b, i, 0)),
                  pl.BlockSpec((1, HALO, D), lambda b, i: (b, jnp.maximum(i * (ts // HALO) - 1, 0), 0)),
                  full(g_pre), full(w_up), full(conv_w), full(conv_b), full(w_down), full(g_post)],
        out_specs=pl.BlockSpec((1, ts, D), lambda b, i: (b, i, 0)),
        out_shape=jax.ShapeDtypeStruct((B, S, D), jnp.float32),
        scratch_shapes=[pltpu.VMEM((ts, d_ff), MXU_DTYPE)],
        name="ffn",
        compiler_params=_params("parallel", "parallel"),
    )(h, h, g_pre, w_up, conv_w, conv_b, w_down, g_post)


def _rope_tables(S):
    half = HEAD_DIM // 2
    inv = ROPE_THETA ** (-jnp.arange(half, dtype=jnp.float32) / half)
    ang = jnp.arange(S).astype(jnp.float32)[:, None] * inv[None, :]
    cos, sin = jnp.cos(ang), jnp.sin(ang)
    reps = LANES // HEAD_DIM
    return jnp.tile(jnp.concatenate([cos, cos], axis=1), (1, reps)), jnp.tile(jnp.concatenate([-sin, sin], axis=1), (1, reps))


def _layer(x, w_in, w_out, attn_pre_norm, attn_post_norm, nsa_out_norm, swa_out_norm, cmp_pos, cmp_w1, cmp_w2,
           swa_sinks, ffn_pre_norm, ffn_post_norm, w_up, conv_w, conv_b, w_down):
    B, S, D = x.shape
    f32 = jnp.float32
    perm = _q_perm()
    sizes = [Q_WIDTH] + [KV_WIDTH] * 6 + [KV_HEADS * GROUP * NSA_BRANCHES] + [Q_WIDTH, KV_WIDTH, KV_WIDTH]
    offs = np.concatenate([[0], np.cumsum(sizes)])
    col = lambda i: w_in[:, offs[i]:offs[i + 1]]
    q_n, k_c, v_c, k_s, v_s, k_w, v_w, g_n, q_w, k_sw, v_sw = (col(i) for i in range(11))
    g_pad = jnp.pad(g_n, ((0, 0), (0, LANES - g_n.shape[1])))
    w_all = jnp.concatenate([q_n[:, perm], q_w[:, perm], k_c, k_s, k_w, k_sw, v_c, v_s, v_w, v_sw, g_pad],
                            axis=1).astype(MXU_DTYPE)
    cos, sin = _rope_tables(S)
    row = lambda g: g.reshape(1, -1).astype(f32)

    qn, qw, kc, vc, kv, gl = _proj(x, row(attn_pre_norm), w_all, cos, sin)

    eye = jnp.eye(KV_HEADS, dtype=f32)
    pe = jnp.broadcast_to(cmp_pos.reshape(2, 2, CMP_STRIDE, 1, HEAD_DIM), (2, 2, CMP_STRIDE, KV_HEADS, HEAD_DIM))
    pe = pe.reshape(2, 2, 1, CMP_STRIDE * KV_WIDTH)
    w1 = cmp_w1.reshape(2, 2, CMP_STRIDE, HEAD_DIM, CMP_HIDDEN)
    w1x = jnp.einsum('kapdm,hg->kaphdgm', w1, eye).reshape(2, 2, CMP_STRIDE * KV_WIDTH, KV_HEADS * CMP_HIDDEN)
    w2x = jnp.einsum('kmd,hg->khmgd', cmp_w2, eye).reshape(2, KV_HEADS * CMP_HIDDEN, KV_WIDTH)
    ck, cv = _compress(kc, vc, pe, w1x.astype(MXU_DTYPE), w2x.astype(MXU_DTYPE))

    o_cmp, sel = _cmp_attention(qn, ck, cv)
    o_slc = _slc_attention(qn, kv, sel)
    o_win = _band_attention(qn, kv, 1, 4, NSA_WINDOW, None)
    o_swa = _band_attention(qw, kv, 2, 5, SWA_WINDOW, swa_sinks.astype(f32))

    n = np.arange(NSA_BRANCHES * Q_WIDTH)
    br, g, h = n // Q_WIDTH, (n % Q_WIDTH) // LANES, (n % LANES) // HEAD_DIM
    expand = np.zeros((LANES, NSA_BRANCHES * Q_WIDTH), np.float32)
    expand[(h * GROUP + g) * NSA_BRANCHES + br, n] = 1.0
    expand = np.concatenate([expand, expand], axis=0)
    w_out_p = jnp.concatenate([w_out[perm], w_out[Q_WIDTH + perm]], axis=0).astype(MXU_DTYPE)
    hmid = _mix(x, o_cmp, o_slc, o_win, o_swa, gl, jnp.asarray(expand, MXU_DTYPE), w_out_p,
                row(nsa_out_norm[perm]), row(swa_out_norm[perm]), row(attn_post_norm))

    return _ffn(hmid, row(ffn_pre_norm), w_up.astype(MXU_DTYPE), conv_w.astype(f32), row(conv_b),
                w_down.astype(MXU_DTYPE), row(ffn_post_norm))


def kernel(x, w_in, w_out, attn_pre_norm, attn_post_norm, nsa_out_norm, swa_out_norm, cmp_pos, cmp_w1, cmp_w2,
           swa_sinks, ffn_pre_norm, ffn_post_norm, w_up, conv_w, conv_b, w_down):
    h = x
    for l in range(w_in.shape[0]):
        h = _layer(h, w_in[l], w_out[l], attn_pre_norm[l], attn_post_norm[l], nsa_out_norm[l], swa_out_norm[l],
                   cmp_pos[l], cmp_w1[l], cmp_w2[l], swa_sinks[l], ffn_pre_norm[l], ffn_post_norm[l],
                   w_up[l], conv_w[l], conv_b[l], w_down[l])
    return h
```

```python
import functools

import numpy as np
import jax
import jax.numpy as jnp
from jax import lax
from jax.experimental import pallas as pl
from jax.experimental.pallas import tpu as pltpu

HEAD_DIM = 64
KV_HEADS = 2
GROUP = 4
Q_WIDTH = KV_HEADS * GROUP * HEAD_DIM
KV_WIDTH = KV_HEADS * HEAD_DIM
SWA_WINDOW = 128
NSA_WINDOW = 512
CMP_LEN = 32
CMP_STRIDE = 16
CMP_HIDDEN = 256
SLC_BLOCK = 64
SLC_TOPK = 16
NSA_BRANCHES = 3
D_FF = 2816
ROPE_THETA = 10000.0
RMS_EPS = 1e-6
NEG = -1e30
BIG = 1e9
LOG2E = float(np.log2(np.e))

LANES = 128
SUBLANES = 8
HALO = 8
MXU_DTYPE = jnp.bfloat16
VMEM_LIMIT = 56 * 1024 * 1024

PROJ_ROWS = 512
CMP_Q = 256
SLC_Q = 256
SLC_K = 256
BAND_Q = 256
BAND_TILES = 2
MIX_ROWS = 512
FFN_ROWS = 1024
FFN_CHUNK = 256
RANK_STEP = 16

_N_ROPE_SLABS = 12
_N_QSLABS = 8
PROJ_WIDTH = 2 * Q_WIDTH + 9 * LANES


def _q_perm():
    n = np.arange(Q_WIDTH)
    g, h, d = n // 128, (n % 128) // 64, n % 64
    return (h * GROUP + g) * HEAD_DIM + d


def _params(*sem):
    return pltpu.CompilerParams(dimension_semantics=sem, vmem_limit_bytes=VMEM_LIMIT)


def _rms(xf, g):
    return xf * lax.rsqrt(jnp.mean(xf * xf, axis=-1, keepdims=True) + RMS_EPS) * g


def _gelu_tanh(x):
    return 0.5 * x * (1.0 + jnp.tanh(np.sqrt(2.0 / np.pi).astype(np.float32) * (x + 0.044715 * (x * x * x))))


def _dot_nt(a, b):
    return lax.dot_general(a, b, (((1,), (1,)), ((), ())), preferred_element_type=jnp.float32)


def _dot_tn(a, b):
    return lax.dot_general(a, b, (((0,), (0,)), ((), ())), preferred_element_type=jnp.float32)


def _split_heads(q):
    qs = jnp.concatenate([q[:, g * LANES:(g + 1) * LANES] for g in range(GROUP)], axis=0)
    lane = lax.broadcasted_iota(jnp.int32, qs.shape, 1)
    zero = jnp.zeros_like(qs)
    return [jnp.where((lane >= h * HEAD_DIM) & (lane < (h + 1) * HEAD_DIM), qs, zero) for h in range(KV_HEADS)]


def _values_with_ones(v):
    lane = lax.broadcasted_iota(jnp.int32, v.shape, 1)
    one = jnp.ones_like(v)
    return [jnp.where((lane >= h * HEAD_DIM) & (lane < (h + 1) * HEAD_DIM), v, one) for h in range(KV_HEADS)]


def _merge_heads(o_t, tq):
    sub = lax.broadcasted_iota(jnp.int32, o_t[0].shape, 0)
    both = jnp.where(sub < HEAD_DIM, o_t[0], o_t[1])
    return jnp.concatenate([both[:, g * tq:(g + 1) * tq].T for g in range(GROUP)], axis=1)


def _proj_kernel(x_ref, g_ref, w_ref, cos_ref, sin_ref, qn_ref, qw_ref, kc_ref, vc_ref, kv_ref, gl_ref):
    hn = _rms(x_ref[0], g_ref[...]).astype(MXU_DTYPE)
    p = jnp.dot(hn, w_ref[...], preferred_element_type=jnp.float32)
    cos, sin = cos_ref[...], sin_ref[...]
    lane = lax.broadcasted_iota(jnp.int32, cos.shape, 1)
    first_half = (lane % HEAD_DIM) < (HEAD_DIM // 2)

    def slab(j, rope):
        z = p[:, j * LANES:(j + 1) * LANES]
        if rope:
            swapped = jnp.where(first_half, pltpu.roll(z, LANES - HEAD_DIM // 2, 1), pltpu.roll(z, HEAD_DIM // 2, 1))
            z = z * cos + swapped * sin
        return z

    scale = HEAD_DIM ** -0.5 * LOG2E
    for j in range(4):
        qn_ref[0, :, j * LANES:(j + 1) * LANES] = (slab(j, True) * scale).astype(qn_ref.dtype)
        qw_ref[0, :, j * LANES:(j + 1) * LANES] = (slab(4 + j, True) * scale).astype(qw_ref.dtype)
    kc_ref[0] = slab(8, True)
    vc_ref[0] = slab(12, False)
    for j in range(3):
        kv_ref[0, :, j * LANES:(j + 1) * LANES] = slab(9 + j, True).astype(kv_ref.dtype)
    for j in range(3):
        kv_ref[0, :, (3 + j) * LANES:(4 + j) * LANES] = slab(13 + j, False).astype(kv_ref.dtype)
    gl_ref[0] = slab(16, False)


def _proj(x, gain, w, cos, sin):
    B, S, D = x.shape
    ts = min(PROJ_ROWS, S)
    row = lambda width: pl.BlockSpec((1, ts, width), lambda b, i: (b, i, 0))
    full = lambda a: pl.BlockSpec(a.shape, lambda b, i: (0,) * a.ndim)
    tab = pl.BlockSpec((ts, LANES), lambda b, i: (i, 0))
    f32, bf = jnp.float32, MXU_DTYPE
    return pl.pallas_call(
        _proj_kernel,
        grid=(B, S // ts),
        in_specs=[row(D), full(gain), full(w), tab, tab],
        out_specs=[row(Q_WIDTH), row(Q_WIDTH), row(LANES), row(LANES), row(6 * LANES), row(LANES)],
        out_shape=[jax.ShapeDtypeStruct((B, S, Q_WIDTH), bf), jax.ShapeDtypeStruct((B, S, Q_WIDTH), bf),
                   jax.ShapeDtypeStruct((B, S, LANES), f32), jax.ShapeDtypeStruct((B, S, LANES), f32),
                   jax.ShapeDtypeStruct((B, S, 6 * LANES), bf), jax.ShapeDtypeStruct((B, S, LANES), f32)],
        name="proj",
        compiler_params=_params("parallel", "parallel"),
    )(x, gain, w, cos, sin)


def _compress_kernel(kc_ref, vc_ref, pe_ref, w1_ref, w2_ref, ck_ref, cv_ref):
    n = kc_ref.shape[1] // CMP_STRIDE
    for kv, (src, dst) in enumerate(((kc_ref, ck_ref), (vc_ref, cv_ref))):
        ch = jnp.concatenate([src[0, pl.ds(p, n, stride=CMP_STRIDE), :] for p in range(CMP_STRIDE)], axis=1)
        top = jnp.dot((ch + pe_ref[kv, 0]).astype(MXU_DTYPE), w1_ref[kv, 0], preferred_element_type=jnp.float32)
        bot = jnp.dot((ch + pe_ref[kv, 1]).astype(MXU_DTYPE), w1_ref[kv, 1], preferred_element_type=jnp.float32)
        hid = top + pltpu.roll(bot, n - 1, 0)
        act = _gelu_tanh(hid).astype(MXU_DTYPE)
        dst[0] = jnp.dot(act, w2_ref[kv], preferred_element_type=jnp.float32).astype(dst.dtype)


def _compress(kc, vc, pe, w1, w2):
    B, S, width = kc.shape
    n = S // CMP_STRIDE
    src = pl.BlockSpec((1, S, width), lambda b: (b, 0, 0))
    full = lambda a: pl.BlockSpec(a.shape, lambda b: (0,) * a.ndim)
    dst = pl.BlockSpec((1, n, KV_WIDTH), lambda b: (b, 0, 0))
    return pl.pallas_call(
        _compress_kernel,
        grid=(B,),
        in_specs=[src, src, full(pe), full(w1), full(w2)],
        out_specs=[dst, dst],
        out_shape=[jax.ShapeDtypeStruct((B, n, KV_WIDTH), MXU_DTYPE)] * 2,
        name="compress",
        compiler_params=_params("parallel"),
    )(kc, vc, pe, w1, w2)


def _select_bias(score, *, live):
    tq = score.shape[1]
    groups = [score[SUBLANES * k:SUBLANES * (k + 1)] for k in range(live // SUBLANES)]
    rank = [jnp.zeros((SUBLANES, tq), jnp.int32) for _ in groups]
    sub = lax.broadcasted_iota(jnp.int32, (SUBLANES, tq), 0)
    for i in range(live):
        row = score[i:i + 1, :]
        for k, grp in enumerate(groups):
            if SUBLANES * k > i:
                beats = jnp.where(row >= grp, 1, 0)
            elif SUBLANES * k + SUBLANES - 1 <= i:
                beats = jnp.where(row > grp, 1, 0)
            else:
                beats = jnp.where(sub > i - SUBLANES * k, jnp.where(row >= grp, 1, 0), jnp.where(row > grp, 1, 0))
            rank[k] = rank[k] + beats
    return jnp.where(jnp.concatenate(rank, axis=0) < min(SLC_TOPK, live), 0.0, NEG)


def _cmp_kernel(q_ref, ck_ref, cv_ref, o_ref, sel_ref, *, tq, nsb):
    q0 = pl.program_id(1) * tq
    ncp = ck_ref.shape[1]
    nheads = KV_HEADS * GROUP
    heads = [qs[g * tq:(g + 1) * tq] for qs in _split_heads(q_ref[0]) for g in range(GROUP)]
    has_valid = (q0 + lax.broadcasted_iota(jnp.int32, (1, tq), 1) >= CMP_LEN - 1).astype(jnp.float32)
    ratio = SLC_BLOCK // CMP_STRIDE

    def attend(live):
        ncl = min(live * ratio, ncp)
        ck = ck_ref[0, 0:ncl, :]
        cv = cv_ref[0, 0:ncl, :]
        c = lax.broadcasted_iota(jnp.int32, (ncl, tq), 0)
        t = q0 + lax.broadcasted_iota(jnp.int32, (ncl, tq), 1)
        valid = c * CMP_STRIDE + (CMP_LEN - 1) <= t
        jo = lax.broadcasted_iota(jnp.int32, (live, ncl), 0)
        co = lax.broadcasted_iota(jnp.int32, (live, ncl), 1)
        overlap = ((co * CMP_STRIDE <= jo * SLC_BLOCK + SLC_BLOCK - 1)
                   & (co * CMP_STRIDE + CMP_LEN - 1 >= jo * SLC_BLOCK) & (co < ncp - 1)).astype(MXU_DTYPE)
        jb = lax.broadcasted_iota(jnp.int32, (live, tq), 0)
        tb = q0 + lax.broadcasted_iota(jnp.int32, (live, tq), 1)
        forced = (jb == 0) | (jb == tb // SLC_BLOCK) | (jb == tb // SLC_BLOCK - 1)
        causal = jb * SLC_BLOCK <= tb

        def finish(s):
            s = jnp.where(valid, s, NEG)
            e = jnp.exp2(s - jnp.max(s, axis=0, keepdims=True))
            r = has_valid / jnp.maximum(jnp.sum(e, axis=0, keepdims=True), 1e-30)
            return _dot_tn(cv, e.astype(MXU_DTYPE)) * r, e * r

        lookahead = 2
        pending = [_dot_nt(ck, heads[n]) for n in range(lookahead)]
        outs, probs = [], []
        for n in range(nheads):
            if n + lookahead < nheads:
                pending.append(_dot_nt(ck, heads[n + lookahead]))
            o, p = finish(pending[n])
            outs.append(o)
            probs.append(p)
        sel_t = []
        for h in range(KV_HEADS):
            pg = functools.reduce(jnp.add, probs[h * GROUP:(h + 1) * GROUP])
            hi = pg.astype(MXU_DTYPE)
            lo = (pg - hi.astype(jnp.float32)).astype(MXU_DTYPE)
            imp = (jnp.dot(overlap, hi, preferred_element_type=jnp.float32)
                   + jnp.dot(overlap, lo, preferred_element_type=jnp.float32))
            bias = _select_bias(jnp.where(forced, BIG, jnp.where(causal, imp, NEG)), live=live)
            sel_t += [bias] + ([jnp.full((nsb - live, tq), NEG, jnp.float32)] if live < nsb else [])
        o_ref[0] = _merge_heads([jnp.concatenate(outs[h * GROUP:(h + 1) * GROUP], axis=1)
                                 for h in range(KV_HEADS)], tq).astype(o_ref.dtype)
        pad = [jnp.zeros((LANES - KV_HEADS * nsb, tq), jnp.float32)] if KV_HEADS * nsb < LANES else []
        sel_ref[0] = jnp.concatenate(sel_t + pad, axis=0).T.astype(sel_ref.dtype)

    extents = list(range(RANK_STEP, nsb + 1, RANK_STEP))
    which = jnp.minimum((q0 + tq - 1) // (RANK_STEP * SLC_BLOCK), len(extents) - 1)
    for v, live in enumerate(extents):
        pl.when(which == v)(functools.partial(attend, live))


def _cmp_attention(qn, ck, cv):
    B, S, _ = qn.shape
    tq = min(CMP_Q, S)
    nsb = S // SLC_BLOCK
    ncp = ck.shape[1]
    return pl.pallas_call(
        functools.partial(_cmp_kernel, tq=tq, nsb=nsb),
        grid=(B, S // tq),
        in_specs=[pl.BlockSpec((1, tq, Q_WIDTH), lambda b, i: (b, i, 0)),
                  pl.BlockSpec((1, ncp, KV_WIDTH), lambda b, i: (b, 0, 0)),
                  pl.BlockSpec((1, ncp, KV_WIDTH), lambda b, i: (b, 0, 0))],
        out_specs=[pl.BlockSpec((1, tq, Q_WIDTH), lambda b, i: (b, i, 0)),
                   pl.BlockSpec((1, tq, LANES), lambda b, i: (b, i, 0))],
        out_shape=[jax.ShapeDtypeStruct((B, S, Q_WIDTH), MXU_DTYPE),
                   jax.ShapeDtypeStruct((B, S, LANES), MXU_DTYPE)],
        name="cmp_select",
        compiler_params=_params("parallel", "parallel"),
    )(qn, ck, cv)


def _slc_kernel(q_ref, k_ref, v_ref, sel_ref, qn_ref, kn_ref, seln_ref, o_ref, *scratch, tq, nsb):
    nheads = KV_HEADS * GROUP
    m_scr, acc_scr, s_scr = (scratch[i * nheads:(i + 1) * nheads] for i in range(3))
    qi = pl.program_id(1)
    erow = lax.broadcasted_iota(jnp.int32, (tq, LANES), 0) // SLC_BLOCK
    ecol = lax.broadcasted_iota(jnp.int32, (tq, LANES), 1)
    ecol = jnp.where(ecol < KV_HEADS * nsb, ecol % nsb, -1)

    def queries(q, bias):
        col = lax.broadcasted_iota(jnp.int32, bias.shape, 1)
        zero = jnp.zeros_like(bias)
        out = []
        for h, qs in enumerate(_split_heads(q)):
            bias_h = jnp.where((col >= h * nsb) & (col < (h + 1) * nsb), bias, zero)
            out += [jnp.concatenate([qs[g * tq:(g + 1) * tq], bias_h], axis=1) for g in range(GROUP)]
        return out

    def keys(k, kt):
        onehot = jnp.where(ecol == erow + kt * (tq // SLC_BLOCK), 1.0, 0.0).astype(MXU_DTYPE)
        return jnp.concatenate([k, onehot], axis=1)

    def consume(n, s, v):
        m_prev = m_scr[n][...]
        m_new = jnp.maximum(m_prev, jnp.max(s, axis=0, keepdims=True))
        alpha = jnp.exp2(m_prev - m_new)
        p = jnp.exp2(s - m_new)
        acc_scr[n][...] = alpha * acc_scr[n][...] + _dot_tn(v[n // GROUP], p.astype(MXU_DTYPE))
        m_scr[n][...] = m_new

    def step(kt, q_next, k_next, mask):
        v = _values_with_ones(v_ref[0, pl.ds(pl.multiple_of(kt * tq, tq), tq), :])
        s_cur = s_scr[0][...]
        s_scr[0][...] = _dot_nt(k_next, q_next[0])
        for n in range(nheads):
            s = s_cur
            if n + 1 < nheads:
                s_cur = s_scr[n + 1][...]
                s_scr[n + 1][...] = _dot_nt(k_next, q_next[n + 1])
            consume(n, s if mask is None else jnp.where(mask, s, NEG), v)

    q_aug = queries(q_ref[0], sel_ref[0])
    for n in range(nheads):
        m_scr[n][...] = jnp.full(m_scr[n].shape, NEG, jnp.float32)
        acc_scr[n][...] = jnp.zeros(acc_scr[n].shape, jnp.float32)

    @pl.when((pl.program_id(0) == 0) & (qi == 0))
    def _():
        k_first = keys(k_ref[0, pl.ds(0, tq), :], 0)
        for n in range(nheads):
            s_scr[n][...] = _dot_nt(k_first, q_aug[n])

    def inner(kt):
        step(kt, q_aug, keys(k_ref[0, pl.ds(pl.multiple_of((kt + 1) * tq, tq), tq), :], kt + 1), None)

    def pair(i, carry):
        inner(2 * i)
        inner(2 * i + 1)
        return carry

    lax.fori_loop(0, qi // 2, pair, 0)

    @pl.when(qi % 2 == 1)
    def _():
        inner(qi - 1)

    causal = lax.broadcasted_iota(jnp.int32, (tq, tq), 0) <= lax.broadcasted_iota(jnp.int32, (tq, tq), 1)
    step(qi, queries(qn_ref[0], seln_ref[0]), keys(kn_ref[0], 0), causal)
    o_t = []
    for h in range(KV_HEADS):
        den_row = (1 - h) * HEAD_DIM
        o_t.append(jnp.concatenate([acc_scr[n][...] / acc_scr[n][den_row:den_row + 1, :]
                                    for n in range(h * GROUP, (h + 1) * GROUP)], axis=1))
    o_ref[0] = _merge_heads(o_t, tq).astype(o_ref.dtype)


def _slc_attention(qn, kv, sel):
    B, S, _ = qn.shape
    tq = min(SLC_Q, S)
    nq = S // tq
    nsb = S // SLC_BLOCK

    def following(b, i):
        wrap = i + 1 == nq
        return jnp.where(wrap, jnp.minimum(b + 1, B - 1), b), jnp.where(wrap, 0, i + 1)

    return pl.pallas_call(
        functools.partial(_slc_kernel, tq=tq, nsb=nsb),
        grid=(B, nq),
        in_specs=[pl.BlockSpec((1, tq, Q_WIDTH), lambda b, i: (b, i, 0)),
                  pl.BlockSpec((1, S, KV_WIDTH), lambda b, i: (b, 0, 0)),
                  pl.BlockSpec((1, S, KV_WIDTH), lambda b, i: (b, 0, 3)),
                  pl.BlockSpec((1, tq, LANES), lambda b, i: (b, i, 0)),
                  pl.BlockSpec((1, tq, Q_WIDTH), lambda b, i: (*following(b, i), 0)),
                  pl.BlockSpec((1, tq, KV_WIDTH), lambda b, i: (following(b, i)[0], 0, 0)),
                  pl.BlockSpec((1, tq, LANES), lambda b, i: (*following(b, i), 0))],
        out_specs=pl.BlockSpec((1, tq, Q_WIDTH), lambda b, i: (b, i, 0)),
        out_shape=jax.ShapeDtypeStruct((B, S, Q_WIDTH), MXU_DTYPE),
        scratch_shapes=([pltpu.VMEM((1, tq), jnp.float32)] * (KV_HEADS * GROUP)
                        + [pltpu.VMEM((KV_WIDTH, tq), jnp.float32)] * (KV_HEADS * GROUP)
                        + [pltpu.VMEM((tq, tq), jnp.float32)] * (KV_HEADS * GROUP)),
        name="slc",
        compiler_params=_params("arbitrary", "arbitrary"),
    )(qn, kv, kv, sel, qn, kv, sel)


def _band_kernel(*refs, tq, tiles, window, sinks):
    if sinks:
        sink_ref, q_ref, k_ref, v_ref, o_ref = refs
    else:
        q_ref, k_ref, v_ref, o_ref = refs
    q0 = pl.program_id(1) * (tiles * tq)
    span = window + tq
    nsub, nhalf, wsub = span // LANES, tq // LANES, window // LANES
    nheads = KV_HEADS * GROUP
    kl = lax.broadcasted_iota(jnp.int32, (LANES, LANES), 0)
    ql = lax.broadcasted_iota(jnp.int32, (LANES, LANES), 1)

    def band_mask(delta):
        if delta >= 1 or delta + wsub <= -1:
            return False
        if delta == 0:
            return kl <= ql
        return kl > ql if delta + wsub == 0 else None

    def attend(t, start, shift):
        heads = [qs[g * tq:(g + 1) * tq] for qs in _split_heads(q_ref[0, t * tq:(t + 1) * tq, :])
                 for g in range(GROUP)]
        k = k_ref[0, pl.ds(start, span), :]
        v = _values_with_ones(v_ref[0, pl.ds(start, span), :])

        def finish(n, s):
            cols, sink_terms = [], []
            den_row = (1 - n // GROUP) * HEAD_DIM
            for c in range(nhalf):
                live = {}
                for r in range(nsub):
                    mask = band_mask(r + shift - c)
                    if mask is False:
                        continue
                    blk = s[r * LANES:(r + 1) * LANES, c * LANES:(c + 1) * LANES]
                    live[r] = blk if mask is None else jnp.where(mask, blk, NEG)
                m = functools.reduce(jnp.maximum, [jnp.max(b, axis=0, keepdims=True) for b in live.values()])
                if sinks:
                    sk = jnp.full((1, LANES), sink_ref[n] * LOG2E, jnp.float32)
                    m = jnp.maximum(m, sk)
                e = {r: jnp.exp2(b - m) for r, b in live.items()}
                if sinks:
                    sink_terms.append(jnp.exp2(sk - m))
                zero = jnp.zeros((LANES, LANES), jnp.float32)
                cols.append(jnp.concatenate([e.get(r, zero) for r in range(nsub)], axis=0))
            p = jnp.concatenate(cols, axis=1).astype(MXU_DTYPE)
            pv = _dot_tn(v[n // GROUP], p)
            den = pv[den_row:den_row + 1, :]
            if sinks:
                den = den + jnp.concatenate(sink_terms, axis=1)
            return pv / den

        lookahead = 2
        pending = [_dot_nt(k, heads[n]) for n in range(lookahead)]
        outs = []
        for n in range(nheads):
            if n + lookahead < nheads:
                pending.append(_dot_nt(k, heads[n + lookahead]))
            outs.append(finish(n, pending[n]))
        o_ref[0, t * tq:(t + 1) * tq, :] = _merge_heads(
            [jnp.concatenate(outs[h * GROUP:(h + 1) * GROUP], axis=1) for h in range(KV_HEADS)], tq).astype(o_ref.dtype)

    @pl.when(q0 >= window)
    def _():
        for t in range(tiles):
            attend(t, pl.multiple_of(q0 + t * tq - window, LANES), -wsub)

    @pl.when(q0 < window)
    def _():
        for t in range(tiles):
            if t * tq >= window:
                attend(t, t * tq - window, -wsub)
            else:
                attend(t, 0, -(t * tq // LANES))


def _band_attention(q, kv, k_col, v_col, window, sinks):
    B, S, _ = q.shape
    tq = min(BAND_Q, S)
    tiles = BAND_TILES
    assert tiles * tq >= window and S % (tiles * tq) == 0
    in_specs = [pl.BlockSpec((1, tiles * tq, Q_WIDTH), lambda b, i: (b, i, 0)),
                pl.BlockSpec((1, S, KV_WIDTH), lambda b, i: (b, 0, k_col)),
                pl.BlockSpec((1, S, KV_WIDTH), lambda b, i: (b, 0, v_col))]
    args = [q, kv, kv]
    if sinks is not None:
        in_specs = [pl.BlockSpec(memory_space=pltpu.SMEM)] + in_specs
        args = [sinks] + args
    return pl.pallas_call(
        functools.partial(_band_kernel, tq=tq, tiles=tiles, window=window, sinks=sinks is not None),
        grid=(B, S // (tiles * tq)),
        in_specs=in_specs,
        out_specs=pl.BlockSpec((1, tiles * tq, Q_WIDTH), lambda b, i: (b, i, 0)),
        out_shape=jax.ShapeDtypeStruct((B, S, Q_WIDTH), MXU_DTYPE),
        name="band_sink" if sinks is not None else "band",
        compiler_params=_params("parallel", "parallel"),
    )(*args)


def _mix_kernel(x_ref, oc_ref, os_ref, ow_ref, osw_ref, gl_ref, ge_ref, w_ref, gn_ref, gs_ref, gp_ref, h_ref):
    sig = 1.0 / (1.0 + jnp.exp(-gl_ref[0]))
    hi = sig.astype(MXU_DTYPE)
    lo = (sig - hi.astype(jnp.float32)).astype(MXU_DTYPE)
    gates = jnp.dot(jnp.concatenate([hi, lo], axis=1), ge_ref[...],
                    preferred_element_type=jnp.float32)
    f32 = jnp.float32
    o_nsa = (gates[:, 0:Q_WIDTH] * oc_ref[0].astype(f32) + gates[:, Q_WIDTH:2 * Q_WIDTH] * os_ref[0].astype(f32)
             + gates[:, 2 * Q_WIDTH:3 * Q_WIDTH] * ow_ref[0].astype(f32))
    cat = jnp.concatenate([_rms(o_nsa, gn_ref[...]), _rms(osw_ref[0].astype(f32), gs_ref[...])],
                          axis=1).astype(MXU_DTYPE)
    mixed = jnp.dot(cat, w_ref[...], preferred_element_type=jnp.float32)
    h_ref[0] = x_ref[0] + _rms(mixed, gp_ref[...])


def _mix(x, o_cmp, o_slc, o_win, o_swa, gl, gate_expand, w_out, g_nsa, g_swa, g_post):
    B, S, D = x.shape
    ts = min(MIX_ROWS, S)
    row = lambda width: pl.BlockSpec((1, ts, width), lambda b, i: (b, i, 0))
    full = lambda a: pl.BlockSpec(a.shape, lambda b, i: (0,) * a.ndim)
    return pl.pallas_call(
        _mix_kernel,
        grid=(B, S // ts),
        in_specs=[row(D), row(Q_WIDTH), row(Q_WIDTH), row(Q_WIDTH), row(Q_WIDTH), row(LANES),
                  full(gate_expand), full(w_out), full(g_nsa), full(g_swa), full(g_post)],
        out_specs=row(D),
        out_shape=jax.ShapeDtypeStruct((B, S, D), jnp.float32),
        name="mix",
        compiler_params=_params("parallel", "parallel"),
    )(x, o_cmp, o_slc, o_win, o_swa, gl, gate_expand, w_out, g_nsa, g_swa, g_post)


def _ffn_kernel(h_ref, halo_ref, gpre_ref, wup_ref, cw_ref, cb_ref, wdn_ref, gpost_ref, y_ref, act_scr,
                *, ts, d_ff, chunk):
    h = h_ref[0]
    hn = pltpu.einshape("(sa)d->(as)d", _rms(h, gpre_ref[...]), s=SUBLANES)
    live = (pl.program_id(1) > 0).astype(jnp.float32)
    hn = jnp.concatenate([_rms(halo_ref[0], gpre_ref[...]) * live, hn], axis=0).astype(MXU_DTYPE)
    sub = lax.broadcasted_iota(jnp.int32, (SUBLANES, chunk), 0)

    def conv(col):
        u = jnp.dot(hn, wup_ref[:, col:col + chunk], preferred_element_type=jnp.float32)
        halo, u = u[:HALO], u[HALO:]
        t1 = jnp.where(sub == 0, halo[HALO - 1:HALO], pltpu.roll(u[ts - SUBLANES:], 1, 0))
        t2 = jnp.where(sub == 0, halo[HALO - 2:HALO - 1], pltpu.roll(u[ts - 2 * SUBLANES:ts - SUBLANES], 1, 0))
        tap1 = jnp.concatenate([t1, u[:ts - SUBLANES]], axis=0)
        tap2 = jnp.concatenate([t2, t1, u[:ts - 2 * SUBLANES]], axis=0)
        w = cw_ref[:, col:col + chunk]
        return w[0:1] * tap2 + w[1:2] * tap1 + w[2:3] * u + cb_ref[:, col:col + chunk]

    for col in range(0, d_ff, chunk):
        act_scr[:, col:col + chunk] = (_gelu_tanh(conv(col)) * conv(d_ff + col)).astype(MXU_DTYPE)
    y = jnp.dot(act_scr[...], wdn_ref[...], preferred_element_type=jnp.float32)
    y = pltpu.einshape("(as)d->(sa)d", y, s=SUBLANES)
    y_ref[0] = h + _rms(y, gpost_ref[...])


def _ffn(h, g_pre, w_up, conv_w, conv_b, w_down, g_post):
    B, S, D = h.shape
    ts = min(FFN_ROWS, S)
    d_ff = w_down.shape[0]
    full = lambda a: pl.BlockSpec(a.shape, lambda b, i: (0,) * a.ndim, pipeline_mode=pl.Buffered(1))
    return pl.pallas_call(
        functools.partial(_ffn_kernel, ts=ts, d_ff=d_ff, chunk=FFN_CHUNK),
        grid=(B, S // ts),
        in_specs=[pl.BlockSpec((1, ts, D), lambda b, i: (b, i, 0)),
                  pl.BlockSpec((1, HALO, D), lambda b, i: (b, jnp.maximum(i * (ts // HALO) - 1, 0), 0)),
                  full(g_pre), full(w_up), full(conv_w), full(conv_b), full(w_down), full(g_post)],
        out_specs=pl.BlockSpec((1, ts, D), lambda b, i: (b, i, 0)),
        out_shape=jax.ShapeDtypeStruct((B, S, D), jnp.float32),
        scratch_shapes=[pltpu.VMEM((ts, d_ff), MXU_DTYPE)],
        name="ffn",
        compiler_params=_params("parallel", "parallel"),
    )(h, h, g_pre, w_up, conv_w, conv_b, w_down, g_post)


def _rope_tables(S):
    half = HEAD_DIM // 2
    inv = ROPE_THETA ** (-jnp.arange(half, dtype=jnp.float32) / half)
    ang = jnp.arange(S).astype(jnp.float32)[:, None] * inv[None, :]
    cos, sin = jnp.cos(ang), jnp.sin(ang)
    reps = LANES // HEAD_DIM
    return jnp.tile(jnp.concatenate([cos, cos], axis=1), (1, reps)), jnp.tile(jnp.concatenate([-sin, sin], axis=1), (1, reps))


def _layer(x, w_in, w_out, attn_pre_norm, attn_post_norm, nsa_out_norm, swa_out_norm, cmp_pos, cmp_w1, cmp_w2,
           swa_sinks, ffn_pre_norm, ffn_post_norm, w_up, conv_w, conv_b, w_down):
    B, S, D = x.shape
    f32 = jnp.float32
    perm = _q_perm()
    sizes = [Q_WIDTH] + [KV_WIDTH] * 6 + [KV_HEADS * GROUP * NSA_BRANCHES] + [Q_WIDTH, KV_WIDTH, KV_WIDTH]
    offs = np.concatenate([[0], np.cumsum(sizes)])
    col = lambda i: w_in[:, offs[i]:offs[i + 1]]
    q_n, k_c, v_c, k_s, v_s, k_w, v_w, g_n, q_w, k_sw, v_sw = (col(i) for i in range(11))
    g_pad = jnp.pad(g_n, ((0, 0), (0, LANES - g_n.shape[1])))
    w_all = jnp.concatenate([q_n[:, perm], q_w[:, perm], k_c, k_s, k_w, k_sw, v_c, v_s, v_w, v_sw, g_pad],
                            axis=1).astype(MXU_DTYPE)
    cos, sin = _rope_tables(S)
    row = lambda g: g.reshape(1, -1).astype(f32)

    qn, qw, kc, vc, kv, gl = _proj(x, row(attn_pre_norm), w_all, cos, sin)

    eye = jnp.eye(KV_HEADS, dtype=f32)
    pe = jnp.broadcast_to(cmp_pos.reshape(2, 2, CMP_STRIDE, 1, HEAD_DIM), (2, 2, CMP_STRIDE, KV_HEADS, HEAD_DIM))
    pe = pe.reshape(2, 2, 1, CMP_STRIDE * KV_WIDTH)
    w1 = cmp_w1.reshape(2, 2, CMP_STRIDE, HEAD_DIM, CMP_HIDDEN)
    w1x = jnp.einsum('kapdm,hg->kaphdgm', w1, eye).reshape(2, 2, CMP_STRIDE * KV_WIDTH, KV_HEADS * CMP_HIDDEN)
    w2x = jnp.einsum('kmd,hg->khmgd', cmp_w2, eye).reshape(2, KV_HEADS * CMP_HIDDEN, KV_WIDTH)
    ck, cv = _compress(kc, vc, pe, w1x.astype(MXU_DTYPE), w2x.astype(MXU_DTYPE))

    o_cmp, sel = _cmp_attention(qn, ck, cv)
    o_slc = _slc_attention(qn, kv, sel)
    o_win = _band_attention(qn, kv, 1, 4, NSA_WINDOW, None)
    o_swa = _band_attention(qw, kv, 2, 5, SWA_WINDOW, swa_sinks.astype(f32))

    n = np.arange(NSA_BRANCHES * Q_WIDTH)
    br, g, h = n // Q_WIDTH, (n % Q_WIDTH) // LANES, (n % LANES) // HEAD_DIM
    expand = np.zeros((LANES, NSA_BRANCHES * Q_WIDTH), np.float32)
    expand[(h * GROUP + g) * NSA_BRANCHES + br, n] = 1.0
    expand = np.concatenate([expand, expand], axis=0)
    w_out_p = jnp.concatenate([w_out[perm], w_out[Q_WIDTH + perm]], axis=0).astype(MXU_DTYPE)
    hmid = _mix(x, o_cmp, o_slc, o_win, o_swa, gl, jnp.asarray(expand, MXU_DTYPE), w_out_p,
                row(nsa_out_norm[perm]), row(swa_out_norm[perm]), row(attn_post_norm))

    return _ffn(hmid, row(ffn_pre_norm), w_up.astype(MXU_DTYPE), conv_w.astype(f32), row(conv_b),
                w_down.astype(MXU_DTYPE), row(ffn_post_norm))


def kernel(x, w_in, w_out, attn_pre_norm, attn_post_norm, nsa_out_norm, swa_out_norm, cmp_pos, cmp_w1, cmp_w2,
           swa_sinks, ffn_pre_norm, ffn_post_norm, w_up, conv_w, conv_b, w_down):
    h = x
    for l in range(w_in.shape[0]):
        h = _layer(h, w_in[l], w_out[l], attn_pre_norm[l], attn_post_norm[l], nsa_out_norm[l], swa_out_norm[l],
                   cmp_pos[l], cmp_w1[l], cmp_w2[l], swa_sinks[l], ffn_pre_norm[l], ffn_post_norm[l],
                   w_up[l], conv_w[l], conv_b[l], w_down[l])
    return h
```

```python
import functools

import numpy as np
import jax
import jax.numpy as jnp
from jax import lax
from jax.experimental import pallas as pl
from jax.experimental.pallas import tpu as pltpu

HEAD_DIM = 64
KV_HEADS = 2
GROUP = 4
Q_WIDTH = KV_HEADS * GROUP * HEAD_DIM
KV_WIDTH = KV_HEADS * HEAD_DIM
SWA_WINDOW = 128
NSA_WINDOW = 512
CMP_LEN = 32
CMP_STRIDE = 16
CMP_HIDDEN = 256
SLC_BLOCK = 64
SLC_TOPK = 16
NSA_BRANCHES = 3
D_FF = 2816
ROPE_THETA = 10000.0
RMS_EPS = 1e-6
NEG = -1e30
BIG = 1e9
LOG2E = float(np.log2(np.e))

LANES = 128
SUBLANES = 8
HALO = 8
MXU_DTYPE = jnp.bfloat16
VMEM_LIMIT = 56 * 1024 * 1024

PROJ_ROWS = 512
CMP_Q = 256
SLC_Q = 256
SLC_K = 256
BAND_Q = 256
BAND_TILES = 4
MIX_ROWS = 512
FFN_ROWS = 1024
FFN_PARTS = 2
FFN_CHUNK = 256
RANK_STEP = 16

_N_ROPE_SLABS = 12
_N_QSLABS = 8
PROJ_WIDTH = 2 * Q_WIDTH + 9 * LANES


def _q_perm():
    n = np.arange(Q_WIDTH)
    g, h, d = n // 128, (n % 128) // 64, n % 64
    return (h * GROUP + g) * HEAD_DIM + d


def _params(*sem):
    return pltpu.CompilerParams(dimension_semantics=sem, vmem_limit_bytes=VMEM_LIMIT)


def _rms(xf, g):
    return xf * lax.rsqrt(jnp.mean(xf * xf, axis=-1, keepdims=True) + RMS_EPS) * g


def _gelu_tanh(x):
    return 0.5 * x * (1.0 + jnp.tanh(np.sqrt(2.0 / np.pi).astype(np.float32) * (x + 0.044715 * (x * x * x))))


def _dot_nt(a, b):
    return lax.dot_general(a, b, (((1,), (1,)), ((), ())), preferred_element_type=jnp.float32)


def _dot_tn(a, b):
    return lax.dot_general(a, b, (((0,), (0,)), ((), ())), preferred_element_type=jnp.float32)


def _split_heads(q):
    qs = jnp.concatenate([q[:, g * LANES:(g + 1) * LANES] for g in range(GROUP)], axis=0)
    lane = lax.broadcasted_iota(jnp.int32, qs.shape, 1)
    zero = jnp.zeros_like(qs)
    return [jnp.where((lane >= h * HEAD_DIM) & (lane < (h + 1) * HEAD_DIM), qs, zero) for h in range(KV_HEADS)]


def _values_with_ones(v):
    lane = lax.broadcasted_iota(jnp.int32, v.shape, 1)
    one = jnp.ones_like(v)
    return [jnp.where((lane >= h * HEAD_DIM) & (lane < (h + 1) * HEAD_DIM), v, one) for h in range(KV_HEADS)]


def _merge_heads(o_t, tq):
    sub = lax.broadcasted_iota(jnp.int32, o_t[0].shape, 0)
    both = jnp.where(sub < HEAD_DIM, o_t[0], o_t[1])
    return jnp.concatenate([both[:, g * tq:(g + 1) * tq].T for g in range(GROUP)], axis=1)


def _proj_kernel(x_ref, g_ref, w_ref, cos_ref, sin_ref, qn_ref, qw_ref, kc_ref, vc_ref, kv_ref, gl_ref):
    hn = _rms(x_ref[0], g_ref[...]).astype(MXU_DTYPE)
    p = jnp.dot(hn, w_ref[...], preferred_element_type=jnp.float32)
    cos, sin = cos_ref[...], sin_ref[...]
    lane = lax.broadcasted_iota(jnp.int32, cos.shape, 1)
    first_half = (lane % HEAD_DIM) < (HEAD_DIM // 2)

    def slab(j, rope):
        z = p[:, j * LANES:(j + 1) * LANES]
        if rope:
            swapped = jnp.where(first_half, pltpu.roll(z, LANES - HEAD_DIM // 2, 1), pltpu.roll(z, HEAD_DIM // 2, 1))
            z = z * cos + swapped * sin
        return z

    scale = HEAD_DIM ** -0.5 * LOG2E
    for j in range(4):
        qn_ref[0, :, j * LANES:(j + 1) * LANES] = (slab(j, True) * scale).astype(qn_ref.dtype)
        qw_ref[0, :, j * LANES:(j + 1) * LANES] = (slab(4 + j, True) * scale).astype(qw_ref.dtype)
    kc_ref[0] = slab(8, True)
    vc_ref[0] = slab(12, False)
    for j in range(3):
        kv_ref[0, :, j * LANES:(j + 1) * LANES] = slab(9 + j, True).astype(kv_ref.dtype)
    for j in range(3):
        kv_ref[0, :, (3 + j) * LANES:(4 + j) * LANES] = slab(13 + j, False).astype(kv_ref.dtype)
    gl_ref[0] = slab(16, False)


def _proj(x, gain, w, cos, sin):
    B, S, D = x.shape
    ts = min(PROJ_ROWS, S)
    row = lambda width: pl.BlockSpec((1, ts, width), lambda b, i: (b, i, 0))
    full = lambda a: pl.BlockSpec(a.shape, lambda b, i: (0,) * a.ndim)
    tab = pl.BlockSpec((ts, LANES), lambda b, i: (i, 0))
    f32, bf = jnp.float32, MXU_DTYPE
    return pl.pallas_call(
        _proj_kernel,
        grid=(B, S // ts),
        in_specs=[row(D), full(gain), full(w), tab, tab],
        out_specs=[row(Q_WIDTH), row(Q_WIDTH), row(LANES), row(LANES), row(6 * LANES), row(LANES)],
        out_shape=[jax.ShapeDtypeStruct((B, S, Q_WIDTH), bf), jax.ShapeDtypeStruct((B, S, Q_WIDTH), bf),
                   jax.ShapeDtypeStruct((B, S, LANES), f32), jax.ShapeDtypeStruct((B, S, LANES), f32),
                   jax.ShapeDtypeStruct((B, S, 6 * LANES), bf), jax.ShapeDtypeStruct((B, S, LANES), f32)],
        name="proj",
        compiler_params=_params("parallel", "parallel"),
    )(x, gain, w, cos, sin)


def _compress_kernel(kc_ref, vc_ref, pe_ref, w1_ref, w2_ref, ck_ref, cv_ref):
    n = kc_ref.shape[1] // CMP_STRIDE
    for kv, (src, dst) in enumerate(((kc_ref, ck_ref), (vc_ref, cv_ref))):
        ch = jnp.concatenate([src[0, pl.ds(p, n, stride=CMP_STRIDE), :] for p in range(CMP_STRIDE)], axis=1)
        top = jnp.dot((ch + pe_ref[kv, 0]).astype(MXU_DTYPE), w1_ref[kv, 0], preferred_element_type=jnp.float32)
        bot = jnp.dot((ch + pe_ref[kv, 1]).astype(MXU_DTYPE), w1_ref[kv, 1], preferred_element_type=jnp.float32)
        hid = top + pltpu.roll(bot, n - 1, 0)
        act = _gelu_tanh(hid).astype(MXU_DTYPE)
        dst[0] = jnp.dot(act, w2_ref[kv], preferred_element_type=jnp.float32).astype(dst.dtype)


def _compress(kc, vc, pe, w1, w2):
    B, S, width = kc.shape
    n = S // CMP_STRIDE
    src = pl.BlockSpec((1, S, width), lambda b: (b, 0, 0))
    full = lambda a: pl.BlockSpec(a.shape, lambda b: (0,) * a.ndim)
    dst = pl.BlockSpec((1, n, KV_WIDTH), lambda b: (b, 0, 0))
    return pl.pallas_call(
        _compress_kernel,
        grid=(B,),
        in_specs=[src, src, full(pe), full(w1), full(w2)],
        out_specs=[dst, dst],
        out_shape=[jax.ShapeDtypeStruct((B, n, KV_WIDTH), MXU_DTYPE)] * 2,
        name="compress",
        compiler_params=_params("parallel"),
    )(kc, vc, pe, w1, w2)


def _select_bias(score, *, live):
    tq = score.shape[1]
    groups = [score[SUBLANES * k:SUBLANES * (k + 1)] for k in range(live // SUBLANES)]
    rank = [jnp.zeros((SUBLANES, tq), jnp.int32) for _ in groups]
    sub = lax.broadcasted_iota(jnp.int32, (SUBLANES, tq), 0)
    for i in range(live):
        row = score[i:i + 1, :]
        for k, grp in enumerate(groups):
            if SUBLANES * k > i:
                beats = jnp.where(row >= grp, 1, 0)
            elif SUBLANES * k + SUBLANES - 1 <= i:
                beats = jnp.where(row > grp, 1, 0)
            else:
                beats = jnp.where(sub > i - SUBLANES * k, jnp.where(row >= grp, 1, 0), jnp.where(row > grp, 1, 0))
            rank[k] = rank[k] + beats
    return jnp.where(jnp.concatenate(rank, axis=0) < min(SLC_TOPK, live), 0.0, NEG)


def _cmp_kernel(q_ref, ck_ref, cv_ref, o_ref, sel_ref, *, tq, nsb):
    q0 = pl.program_id(1) * tq
    ncp = ck_ref.shape[1]
    nheads = KV_HEADS * GROUP
    heads = [qs[g * tq:(g + 1) * tq] for qs in _split_heads(q_ref[0]) for g in range(GROUP)]
    has_valid = (q0 + lax.broadcasted_iota(jnp.int32, (1, tq), 1) >= CMP_LEN - 1).astype(jnp.float32)
    ratio = SLC_BLOCK // CMP_STRIDE

    def attend(live):
        ncl = min(live * ratio, ncp)
        ck = ck_ref[0, 0:ncl, :]
        cv = cv_ref[0, 0:ncl, :]
        c = lax.broadcasted_iota(jnp.int32, (ncl, tq), 0)
        t = q0 + lax.broadcasted_iota(jnp.int32, (ncl, tq), 1)
        valid = c * CMP_STRIDE + (CMP_LEN - 1) <= t
        jo = lax.broadcasted_iota(jnp.int32, (live, ncl), 0)
        co = lax.broadcasted_iota(jnp.int32, (live, ncl), 1)
        overlap = ((co * CMP_STRIDE <= jo * SLC_BLOCK + SLC_BLOCK - 1)
                   & (co * CMP_STRIDE + CMP_LEN - 1 >= jo * SLC_BLOCK) & (co < ncp - 1)).astype(MXU_DTYPE)
        jb = lax.broadcasted_iota(jnp.int32, (live, tq), 0)
        tb = q0 + lax.broadcasted_iota(jnp.int32, (live, tq), 1)
        forced = (jb == 0) | (jb == tb // SLC_BLOCK) | (jb == tb // SLC_BLOCK - 1)
        causal = jb * SLC_BLOCK <= tb

        def finish(s):
            s = jnp.where(valid, s, NEG)
            e = jnp.exp2(s - jnp.max(s, axis=0, keepdims=True))
            r = has_valid / jnp.maximum(jnp.sum(e, axis=0, keepdims=True), 1e-30)
            return _dot_tn(cv, e.astype(MXU_DTYPE)) * r, e * r

        lookahead = 2
        pending = [_dot_nt(ck, heads[n]) for n in range(lookahead)]
        outs, probs = [], []
        for n in range(nheads):
            if n + lookahead < nheads:
                pending.append(_dot_nt(ck, heads[n + lookahead]))
            o, p = finish(pending[n])
            outs.append(o)
            probs.append(p)
        sel_t = []
        for h in range(KV_HEADS):
            pg = functools.reduce(jnp.add, probs[h * GROUP:(h + 1) * GROUP])
            hi = pg.astype(MXU_DTYPE)
            lo = (pg - hi.astype(jnp.float32)).astype(MXU_DTYPE)
            imp = (jnp.dot(overlap, hi, preferred_element_type=jnp.float32)
                   + jnp.dot(overlap, lo, preferred_element_type=jnp.float32))
            bias = _select_bias(jnp.where(forced, BIG, jnp.where(causal, imp, NEG)), live=live)
            sel_t += [bias] + ([jnp.full((nsb - live, tq), NEG, jnp.float32)] if live < nsb else [])
        o_ref[0] = _merge_heads([jnp.concatenate(outs[h * GROUP:(h + 1) * GROUP], axis=1)
                                 for h in range(KV_HEADS)], tq).astype(o_ref.dtype)
        pad = [jnp.zeros((LANES - KV_HEADS * nsb, tq), jnp.float32)] if KV_HEADS * nsb < LANES else []
        sel_ref[0] = jnp.concatenate(sel_t + pad, axis=0).T.astype(sel_ref.dtype)

    extents = list(range(RANK_STEP, nsb + 1, RANK_STEP))
    which = jnp.minimum((q0 + tq - 1) // (RANK_STEP * SLC_BLOCK), len(extents) - 1)
    for v, live in enumerate(extents):
        pl.when(which == v)(functools.partial(attend, live))


def _cmp_attention(qn, ck, cv):
    B, S, _ = qn.shape
    tq = min(CMP_Q, S)
    nsb = S // SLC_BLOCK
    ncp = ck.shape[1]
    return pl.pallas_call(
        functools.partial(_cmp_kernel, tq=tq, nsb=nsb),
        grid=(B, S // tq),
        in_specs=[pl.BlockSpec((1, tq, Q_WIDTH), lambda b, i: (b, i, 0)),
                  pl.BlockSpec((1, ncp, KV_WIDTH), lambda b, i: (b, 0, 0)),
                  pl.BlockSpec((1, ncp, KV_WIDTH), lambda b, i: (b, 0, 0))],
        out_specs=[pl.BlockSpec((1, tq, Q_WIDTH), lambda b, i: (b, i, 0)),
                   pl.BlockSpec((1, tq, LANES), lambda b, i: (b, i, 0))],
        out_shape=[jax.ShapeDtypeStruct((B, S, Q_WIDTH), MXU_DTYPE),
                   jax.ShapeDtypeStruct((B, S, LANES), MXU_DTYPE)],
        name="cmp_select",
        compiler_params=_params("parallel", "parallel"),
    )(qn, ck, cv)


def _slc_kernel(q_ref, k_ref, v_ref, sel_ref, qn_ref, kn_ref, seln_ref, o_ref, *scratch, tq, nsb):
    nheads = KV_HEADS * GROUP
    m_scr, acc_scr, s_scr = (scratch[i * nheads:(i + 1) * nheads] for i in range(3))
    qi = pl.program_id(1)
    erow = lax.broadcasted_iota(jnp.int32, (tq, LANES), 0) // SLC_BLOCK
    ecol = lax.broadcasted_iota(jnp.int32, (tq, LANES), 1)
    ecol = jnp.where(ecol < KV_HEADS * nsb, ecol % nsb, -1)

    def queries(q, bias):
        col = lax.broadcasted_iota(jnp.int32, bias.shape, 1)
        zero = jnp.zeros_like(bias)
        out = []
        for h, qs in enumerate(_split_heads(q)):
            bias_h = jnp.where((col >= h * nsb) & (col < (h + 1) * nsb), bias, zero)
            out += [jnp.concatenate([qs[g * tq:(g + 1) * tq], bias_h], axis=1) for g in range(GROUP)]
        return out

    def keys(k, kt):
        onehot = jnp.where(ecol == erow + kt * (tq // SLC_BLOCK), 1.0, 0.0).astype(MXU_DTYPE)
        return jnp.concatenate([k, onehot], axis=1)

    def consume(n, s, v):
        m_prev = m_scr[n][...]
        m_new = jnp.maximum(m_prev, jnp.max(s, axis=0, keepdims=True))
        alpha = jnp.exp2(m_prev - m_new)
        p = jnp.exp2(s - m_new)
        acc_scr[n][...] = alpha * acc_scr[n][...] + _dot_tn(v[n // GROUP], p.astype(MXU_DTYPE))
        m_scr[n][...] = m_new

    def step(kt, q_next, k_next, mask):
        v = _values_with_ones(v_ref[0, pl.ds(pl.multiple_of(kt * tq, tq), tq), :])
        s_cur = s_scr[0][...]
        s_scr[0][...] = _dot_nt(k_next, q_next[0])
        for n in range(nheads):
            s = s_cur
            if n + 1 < nheads:
                s_cur = s_scr[n + 1][...]
                s_scr[n + 1][...] = _dot_nt(k_next, q_next[n + 1])
            consume(n, s if mask is None else jnp.where(mask, s, NEG), v)

    q_aug = queries(q_ref[0], sel_ref[0])
    for n in range(nheads):
        m_scr[n][...] = jnp.full(m_scr[n].shape, NEG, jnp.float32)
        acc_scr[n][...] = jnp.zeros(acc_scr[n].shape, jnp.float32)

    @pl.when((pl.program_id(0) == 0) & (qi == 0))
    def _():
        k_first = keys(k_ref[0, pl.ds(0, tq), :], 0)
        for n in range(nheads):
            s_scr[n][...] = _dot_nt(k_first, q_aug[n])

    def inner(kt):
        step(kt, q_aug, keys(k_ref[0, pl.ds(pl.multiple_of((kt + 1) * tq, tq), tq), :], kt + 1), None)

    def pair(i, carry):
        inner(2 * i)
        inner(2 * i + 1)
        return carry

    lax.fori_loop(0, qi // 2, pair, 0)

    @pl.when(qi % 2 == 1)
    def _():
        inner(qi - 1)

    causal = lax.broadcasted_iota(jnp.int32, (tq, tq), 0) <= lax.broadcasted_iota(jnp.int32, (tq, tq), 1)
    step(qi, queries(qn_ref[0], seln_ref[0]), keys(kn_ref[0], 0), causal)
    o_t = []
    for h in range(KV_HEADS):
        den_row = (1 - h) * HEAD_DIM
        o_t.append(jnp.concatenate([acc_scr[n][...] / acc_scr[n][den_row:den_row + 1, :]
                                    for n in range(h * GROUP, (h + 1) * GROUP)], axis=1))
    o_ref[0] = _merge_heads(o_t, tq).astype(o_ref.dtype)


def _slc_attention(qn, kv, sel):
    B, S, _ = qn.shape
    tq = min(SLC_Q, S)
    nq = S // tq
    nsb = S // SLC_BLOCK

    def following(b, i):
        wrap = i + 1 == nq
        return jnp.where(wrap, jnp.minimum(b + 1, B - 1), b), jnp.where(wrap, 0, i + 1)

    return pl.pallas_call(
        functools.partial(_slc_kernel, tq=tq, nsb=nsb),
        grid=(B, nq),
        in_specs=[pl.BlockSpec((1, tq, Q_WIDTH), lambda b, i: (b, i, 0)),
                  pl.BlockSpec((1, S, KV_WIDTH), lambda b, i: (b, 0, 0)),
                  pl.BlockSpec((1, S, KV_WIDTH), lambda b, i: (b, 0, 3)),
                  pl.BlockSpec((1, tq, LANES), lambda b, i: (b, i, 0)),
                  pl.BlockSpec((1, tq, Q_WIDTH), lambda b, i: (*following(b, i), 0)),
                  pl.BlockSpec((1, tq, KV_WIDTH), lambda b, i: (following(b, i)[0], 0, 0)),
                  pl.BlockSpec((1, tq, LANES), lambda b, i: (*following(b, i), 0))],
        out_specs=pl.BlockSpec((1, tq, Q_WIDTH), lambda b, i: (b, i, 0)),
        out_shape=jax.ShapeDtypeStruct((B, S, Q_WIDTH), MXU_DTYPE),
        scratch_shapes=([pltpu.VMEM((1, tq), jnp.float32)] * (KV_HEADS * GROUP)
                        + [pltpu.VMEM((KV_WIDTH, tq), jnp.float32)] * (KV_HEADS * GROUP)
                        + [pltpu.VMEM((tq, tq), jnp.float32)] * (KV_HEADS * GROUP)),
        name="slc",
        compiler_params=_params("arbitrary", "arbitrary"),
    )(qn, kv, kv, sel, qn, kv, sel)


def _band_kernel(*refs, tq, tiles, window, sinks):
    if sinks:
        sink_ref, q_ref, k_ref, v_ref, o_ref = refs
    else:
        q_ref, k_ref, v_ref, o_ref = refs
    q0 = pl.program_id(1) * (tiles * tq)
    span = window + tq
    nsub, nhalf, wsub = span // LANES, tq // LANES, window // LANES
    nheads = KV_HEADS * GROUP
    kl = lax.broadcasted_iota(jnp.int32, (LANES, LANES), 0)
    ql = lax.broadcasted_iota(jnp.int32, (LANES, LANES), 1)

    def band_mask(delta):
        if delta >= 1 or delta + wsub <= -1:
            return False
        if delta == 0:
            return kl <= ql
        return kl > ql if delta + wsub == 0 else None

    def attend(t, start, shift):
        heads = [qs[g * tq:(g + 1) * tq] for qs in _split_heads(q_ref[0, t * tq:(t + 1) * tq, :])
                 for g in range(GROUP)]
        k = k_ref[0, pl.ds(start, span), :]
        v = _values_with_ones(v_ref[0, pl.ds(start, span), :])

        def finish(n, s):
            cols, sink_terms = [], []
            den_row = (1 - n // GROUP) * HEAD_DIM
            for c in range(nhalf):
                live = {}
                for r in range(nsub):
                    mask = band_mask(r + shift - c)
                    if mask is False:
                        continue
                    blk = s[r * LANES:(r + 1) * LANES, c * LANES:(c + 1) * LANES]
                    live[r] = blk if mask is None else jnp.where(mask, blk, NEG)
                m = functools.reduce(jnp.maximum, [jnp.max(b, axis=0, keepdims=True) for b in live.values()])
                if sinks:
                    sk = jnp.full((1, LANES), sink_ref[n] * LOG2E, jnp.float32)
                    m = jnp.maximum(m, sk)
                e = {r: jnp.exp2(b - m) for r, b in live.items()}
                if sinks:
                    sink_terms.append(jnp.exp2(sk - m))
                zero = jnp.zeros((LANES, LANES), jnp.float32)
                cols.append(jnp.concatenate([e.get(r, zero) for r in range(nsub)], axis=0))
            p = jnp.concatenate(cols, axis=1).astype(MXU_DTYPE)
            pv = _dot_tn(v[n // GROUP], p)
            den = pv[den_row:den_row + 1, :]
            if sinks:
                den = den + jnp.concatenate(sink_terms, axis=1)
            return pv / den

        lookahead = 2
        pending = [_dot_nt(k, heads[n]) for n in range(lookahead)]
        outs = []
        for n in range(nheads):
            if n + lookahead < nheads:
                pending.append(_dot_nt(k, heads[n + lookahead]))
            outs.append(finish(n, pending[n]))
        o_ref[0, t * tq:(t + 1) * tq, :] = _merge_heads(
            [jnp.concatenate(outs[h * GROUP:(h + 1) * GROUP], axis=1) for h in range(KV_HEADS)], tq).astype(o_ref.dtype)

    @pl.when(q0 >= window)
    def _():
        for t in range(tiles):
            attend(t, pl.multiple_of(q0 + t * tq - window, LANES), -wsub)

    @pl.when(q0 < window)
    def _():
        for t in range(tiles):
            if t * tq >= window:
                attend(t, t * tq - window, -wsub)
            else:
                attend(t, 0, -(t * tq // LANES))


def _band_attention(q, kv, k_col, v_col, window, sinks):
    B, S, _ = q.shape
    tq = min(BAND_Q, S)
    tiles = BAND_TILES
    assert tiles * tq >= window and S % (tiles * tq) == 0
    in_specs = [pl.BlockSpec((1, tiles * tq, Q_WIDTH), lambda b, i: (b, i, 0)),
                pl.BlockSpec((1, S, KV_WIDTH), lambda b, i: (b, 0, k_col)),
                pl.BlockSpec((1, S, KV_WIDTH), lambda b, i: (b, 0, v_col))]
    args = [q, kv, kv]
    if sinks is not None:
        in_specs = [pl.BlockSpec(memory_space=pltpu.SMEM)] + in_specs
        args = [sinks] + args
    return pl.pallas_call(
        functools.partial(_band_kernel, tq=tq, tiles=tiles, window=window, sinks=sinks is not None),
        grid=(B, S // (tiles * tq)),
        in_specs=in_specs,
        out_specs=pl.BlockSpec((1, tiles * tq, Q_WIDTH), lambda b, i: (b, i, 0)),
        out_shape=jax.ShapeDtypeStruct((B, S, Q_WIDTH), MXU_DTYPE),
        name="band_sink" if sinks is not None else "band",
        compiler_params=_params("parallel", "parallel"),
    )(*args)


def _mix_kernel(x_ref, oc_ref, os_ref, ow_ref, osw_ref, gl_ref, ge_ref, w_ref, gn_ref, gs_ref, gp_ref, h_ref):
    sig = 1.0 / (1.0 + jnp.exp(-gl_ref[0]))
    hi = sig.astype(MXU_DTYPE)
    lo = (sig - hi.astype(jnp.float32)).astype(MXU_DTYPE)
    gates = jnp.dot(jnp.concatenate([hi, lo], axis=1), ge_ref[...],
                    preferred_element_type=jnp.float32)
    f32 = jnp.float32
    o_nsa = (gates[:, 0:Q_WIDTH] * oc_ref[0].astype(f32) + gates[:, Q_WIDTH:2 * Q_WIDTH] * os_ref[0].astype(f32)
             + gates[:, 2 * Q_WIDTH:3 * Q_WIDTH] * ow_ref[0].astype(f32))
    cat = jnp.concatenate([_rms(o_nsa, gn_ref[...]), _rms(osw_ref[0].astype(f32), gs_ref[...])],
                          axis=1).astype(MXU_DTYPE)
    mixed = jnp.dot(cat, w_ref[...], preferred_element_type=jnp.float32)
    h_ref[0] = x_ref[0] + _rms(mixed, gp_ref[...])


def _mix(x, o_cmp, o_slc, o_win, o_swa, gl, gate_expand, w_out, g_nsa, g_swa, g_post):
    B, S, D = x.shape
    ts = min(MIX_ROWS, S)
    row = lambda width: pl.BlockSpec((1, ts, width), lambda b, i: (b, i, 0))
    full = lambda a: pl.BlockSpec(a.shape, lambda b, i: (0,) * a.ndim)
    return pl.pallas_call(
        _mix_kernel,
        grid=(B, S // ts),
        in_specs=[row(D), row(Q_WIDTH), row(Q_WIDTH), row(Q_WIDTH), row(Q_WIDTH), row(LANES),
                  full(gate_expand), full(w_out), full(g_nsa), full(g_swa), full(g_post)],
        out_specs=row(D),
        out_shape=jax.ShapeDtypeStruct((B, S, D), jnp.float32),
        name="mix",
        compiler_params=_params("parallel", "parallel"),
    )(x, o_cmp, o_slc, o_win, o_swa, gl, gate_expand, w_out, g_nsa, g_swa, g_post)


def _ffn_kernel(h_ref, halo_ref, gpre_ref, wup_ref, cw_ref, cb_ref, wdn_ref, gpost_ref, y_ref, act_scr,
                *, ts, d_ff, chunk):
    rows = ts // FFN_PARTS
    sub = lax.broadcasted_iota(jnp.int32, (SUBLANES, chunk), 0)
    live = (pl.program_id(1) > 0).astype(jnp.float32)
    halo_hn = _rms(halo_ref[0], gpre_ref[...]) * live
    lhs = []
    for part in range(FFN_PARTS):
        hn = _rms(h_ref[0, part * rows:(part + 1) * rows, :], gpre_ref[...])
        lhs.append(jnp.concatenate([halo_hn, pltpu.einshape("(sa)d->(as)d", hn, s=SUBLANES)],
                                   axis=0).astype(MXU_DTYPE))
        halo_hn = hn[rows - HALO:]

    def conv(hn, col):
        u = jnp.dot(hn, wup_ref[:, col:col + chunk], preferred_element_type=jnp.float32)
        halo, u = u[:HALO], u[HALO:]
        t1 = jnp.where(sub == 0, halo[HALO - 1:HALO], pltpu.roll(u[rows - SUBLANES:], 1, 0))
        t2 = jnp.where(sub == 0, halo[HALO - 2:HALO - 1], pltpu.roll(u[rows - 2 * SUBLANES:rows - SUBLANES], 1, 0))
        tap1 = jnp.concatenate([t1, u[:rows - SUBLANES]], axis=0)
        tap2 = jnp.concatenate([t2, t1, u[:rows - 2 * SUBLANES]], axis=0)
        w = cw_ref[:, col:col + chunk]
        return w[0:1] * tap2 + w[1:2] * tap1 + w[2:3] * u + cb_ref[:, col:col + chunk]

    for part in range(FFN_PARTS):
        for col in range(0, d_ff, chunk):
            act = _gelu_tanh(conv(lhs[part], col)) * conv(lhs[part], d_ff + col)
            act_scr[part * rows:(part + 1) * rows, col:col + chunk] = act.astype(MXU_DTYPE)
    ys = [jnp.dot(act_scr[part * rows:(part + 1) * rows, :], wdn_ref[...], preferred_element_type=jnp.float32)
          for part in range(FFN_PARTS)]
    for part in range(FFN_PARTS):
        y = pltpu.einshape("(as)d->(sa)d", ys[part], s=SUBLANES)
        y_ref[0, part * rows:(part + 1) * rows, :] = (h_ref[0, part * rows:(part + 1) * rows, :]
                                                      + _rms(y, gpost_ref[...]))


def _ffn(h, g_pre, w_up, conv_w, conv_b, w_down, g_post):
    B, S, D = h.shape
    ts = min(FFN_ROWS, S)
    d_ff = w_down.shape[0]
    full = lambda a: pl.BlockSpec(a.shape, lambda b, i: (0,) * a.ndim, pipeline_mode=pl.Buffered(1))
    return pl.pallas_call(
        functools.partial(_ffn_kernel, ts=ts, d_ff=d_ff, chunk=FFN_CHUNK),
        grid=(B, S // ts),
        in_specs=[pl.BlockSpec((1, ts, D), lambda b, i: (b, i, 0)),
                  pl.BlockSpec((1, HALO, D), lambda b, i: (b, jnp.maximum(i * (ts // HALO) - 1, 0), 0)),
                  full(g_pre), full(w_up), full(conv_w), full(conv_b), full(w_down), full(g_post)],
        out_specs=pl.BlockSpec((1, ts, D), lambda b, i: (b, i, 0)),
        out_shape=jax.ShapeDtypeStruct((B, S, D), jnp.float32),
        scratch_shapes=[pltpu.VMEM((ts, d_ff), MXU_DTYPE)],
        name="ffn",
        compiler_params=_params("parallel", "parallel"),
    )(h, h, g_pre, w_up, conv_w, conv_b, w_down, g_post)


def _rope_tables(S):
    half = HEAD_DIM // 2
    inv = ROPE_THETA ** (-jnp.arange(half, dtype=jnp.float32) / half)
    ang = jnp.arange(S).astype(jnp.float32)[:, None] * inv[None, :]
    cos, sin = jnp.cos(ang), jnp.sin(ang)
    reps = LANES // HEAD_DIM
    return jnp.tile(jnp.concatenate([cos, cos], axis=1), (1, reps)), jnp.tile(jnp.concatenate([-sin, sin], axis=1), (1, reps))


def _layer(x, w_in, w_out, attn_pre_norm, attn_post_norm, nsa_out_norm, swa_out_norm, cmp_pos, cmp_w1, cmp_w2,
           swa_sinks, ffn_pre_norm, ffn_post_norm, w_up, conv_w, conv_b, w_down):
    B, S, D = x.shape
    f32 = jnp.float32
    perm = _q_perm()
    sizes = [Q_WIDTH] + [KV_WIDTH] * 6 + [KV_HEADS * GROUP * NSA_BRANCHES] + [Q_WIDTH, KV_WIDTH, KV_WIDTH]
    offs = np.concatenate([[0], np.cumsum(sizes)])
    col = lambda i: w_in[:, offs[i]:offs[i + 1]]
    q_n, k_c, v_c, k_s, v_s, k_w, v_w, g_n, q_w, k_sw, v_sw = (col(i) for i in range(11))
    g_pad = jnp.pad(g_n, ((0, 0), (0, LANES - g_n.shape[1])))
    w_all = jnp.concatenate([q_n[:, perm], q_w[:, perm], k_c, k_s, k_w, k_sw, v_c, v_s, v_w, v_sw, g_pad],
                            axis=1).astype(MXU_DTYPE)
    cos, sin = _rope_tables(S)
    row = lambda g: g.reshape(1, -1).astype(f32)

    qn, qw, kc, vc, kv, gl = _proj(x, row(attn_pre_norm), w_all, cos, sin)

    pe = jnp.broadcast_to(cmp_pos.reshape(2, 2, CMP_STRIDE, 1, HEAD_DIM), (2, 2, CMP_STRIDE, KV_HEADS, HEAD_DIM))
    pe = pe.reshape(2, 2, 1, CMP_STRIDE * KV_WIDTH)
    w1 = cmp_w1.astype(MXU_DTYPE).reshape(2, 2, CMP_STRIDE, HEAD_DIM, CMP_HIDDEN)
    w1x = jnp.stack([jnp.concatenate([w1, jnp.zeros_like(w1)], axis=-1),
                     jnp.concatenate([jnp.zeros_like(w1), w1], axis=-1)], axis=3)
    w1x = w1x.reshape(2, 2, CMP_STRIDE * KV_WIDTH, KV_HEADS * CMP_HIDDEN)
    w2 = cmp_w2.astype(MXU_DTYPE)
    w2x = jnp.concatenate([jnp.concatenate([w2, jnp.zeros_like(w2)], axis=-1),
                           jnp.concatenate([jnp.zeros_like(w2), w2], axis=-1)], axis=1)
    ck, cv = _compress(kc, vc, pe, w1x, w2x)

    o_cmp, sel = _cmp_attention(qn, ck, cv)
    o_slc = _slc_attention(qn, kv, sel)
    o_win = _band_attention(qn, kv, 1, 4, NSA_WINDOW, None)
    o_swa = _band_attention(qw, kv, 2, 5, SWA_WINDOW, swa_sinks.astype(f32))

    n = np.arange(NSA_BRANCHES * Q_WIDTH)
    br, g, h = n // Q_WIDTH, (n % Q_WIDTH) // LANES, (n % LANES) // HEAD_DIM
    expand = np.zeros((LANES, NSA_BRANCHES * Q_WIDTH), np.float32)
    expand[(h * GROUP + g) * NSA_BRANCHES + br, n] = 1.0
    expand = np.concatenate([expand, expand], axis=0)
    w_out_p = jnp.concatenate([w_out[perm], w_out[Q_WIDTH + perm]], axis=0).astype(MXU_DTYPE)
    hmid = _mix(x, o_cmp, o_slc, o_win, o_swa, gl, jnp.asarray(expand, MXU_DTYPE), w_out_p,
                row(nsa_out_norm[perm]), row(swa_out_norm[perm]), row(attn_post_norm))

    return _ffn(hmid, row(ffn_pre_norm), w_up.astype(MXU_DTYPE), conv_w.astype(f32), row(conv_b),
                w_down.astype(MXU_DTYPE), row(ffn_post_norm))


def kernel(x, w_in, w_out, attn_pre_norm, attn_post_norm, nsa_out_norm, swa_out_norm, cmp_pos, cmp_w1, cmp_w2,
           swa_sinks, ffn_pre_norm, ffn_post_norm, w_up, conv_w, conv_b, w_down):
    h = x
    for l in range(w_in.shape[0]):
        h = _layer(h, w_in[l], w_out[l], attn_pre_norm[l], attn_post_norm[l], nsa_out_norm[l], swa_out_norm[l],
                   cmp_pos[l], cmp_w1[l], cmp_w2[l], swa_sinks[l], ffn_pre_norm[l], ffn_post_norm[l],
                   w_up[l], conv_w[l], conv_b[l], w_down[l])
    return h
```

```python
import functools

import numpy as np
import jax
import jax.numpy as jnp
from jax import lax
from jax.experimental import pallas as pl
from jax.experimental.pallas import tpu as pltpu

HEAD_DIM = 64
KV_HEADS = 2
GROUP = 4
Q_WIDTH = KV_HEADS * GROUP * HEAD_DIM
KV_WIDTH = KV_HEADS * HEAD_DIM
SWA_WINDOW = 128
NSA_WINDOW = 512
CMP_LEN = 32
CMP_STRIDE = 16
CMP_HIDDEN = 256
SLC_BLOCK = 64
SLC_TOPK = 16
NSA_BRANCHES = 3
D_FF = 2816
ROPE_THETA = 10000.0
RMS_EPS = 1e-6
NEG = -1e30
BIG = 1e9
LOG2E = float(np.log2(np.e))

LANES = 128
SUBLANES = 8
HALO = 8
MXU_DTYPE = jnp.bfloat16
VMEM_LIMIT = 56 * 1024 * 1024

PROJ_ROWS = 512
CMP_Q = 256
SLC_Q = 256
SLC_UNROLL = 4
BAND_Q = 256
BAND_TILES = 4
MIX_ROWS = 512
FFN_ROWS = 1024
FFN_PARTS = 2
FFN_CHUNK = 256
RANK_STEP = 16

_N_ROPE_SLABS = 12
_N_QSLABS = 8
PROJ_WIDTH = 2 * Q_WIDTH + 9 * LANES


def _q_perm():
    n = np.arange(Q_WIDTH)
    g, h, d = n // 128, (n % 128) // 64, n % 64
    return (h * GROUP + g) * HEAD_DIM + d


def _params(*sem):
    return pltpu.CompilerParams(dimension_semantics=sem, vmem_limit_bytes=VMEM_LIMIT)


def _rms(xf, g):
    return xf * lax.rsqrt(jnp.mean(xf * xf, axis=-1, keepdims=True) + RMS_EPS) * g


def _gelu_tanh(x):
    return 0.5 * x * (1.0 + jnp.tanh(np.sqrt(2.0 / np.pi).astype(np.float32) * (x + 0.044715 * (x * x * x))))


def _dot_nt(a, b):
    return lax.dot_general(a, b, (((1,), (1,)), ((), ())), preferred_element_type=jnp.float32)


def _dot_tn(a, b):
    return lax.dot_general(a, b, (((0,), (0,)), ((), ())), preferred_element_type=jnp.float32)


def _split_heads(q):
    qs = jnp.concatenate([q[:, g * LANES:(g + 1) * LANES] for g in range(GROUP)], axis=0)
    lane = lax.broadcasted_iota(jnp.int32, qs.shape, 1)
    zero = jnp.zeros_like(qs)
    return [jnp.where((lane >= h * HEAD_DIM) & (lane < (h + 1) * HEAD_DIM), qs, zero) for h in range(KV_HEADS)]


def _values_with_ones(v):
    lane = lax.broadcasted_iota(jnp.int32, v.shape, 1)
    one = jnp.ones_like(v)
    return [jnp.where((lane >= h * HEAD_DIM) & (lane < (h + 1) * HEAD_DIM), v, one) for h in range(KV_HEADS)]


def _merge_heads(o_t, tq):
    sub = lax.broadcasted_iota(jnp.int32, o_t[0].shape, 0)
    both = jnp.where(sub < HEAD_DIM, o_t[0], o_t[1])
    return jnp.concatenate([both[:, g * tq:(g + 1) * tq].T for g in range(GROUP)], axis=1)


def _proj_kernel(x_ref, g_ref, w_ref, cos_ref, sin_ref, qn_ref, qw_ref, kc_ref, vc_ref, kv_ref, gl_ref):
    hn = _rms(x_ref[0], g_ref[...]).astype(MXU_DTYPE)
    p = jnp.dot(hn, w_ref[...], preferred_element_type=jnp.float32)
    cos, sin = cos_ref[...], sin_ref[...]
    lane = lax.broadcasted_iota(jnp.int32, cos.shape, 1)
    first_half = (lane % HEAD_DIM) < (HEAD_DIM // 2)

    def slab(j, rope):
        z = p[:, j * LANES:(j + 1) * LANES]
        if rope:
            swapped = jnp.where(first_half, pltpu.roll(z, LANES - HEAD_DIM // 2, 1), pltpu.roll(z, HEAD_DIM // 2, 1))
            z = z * cos + swapped * sin
        return z

    scale = HEAD_DIM ** -0.5 * LOG2E
    for j in range(4):
        qn_ref[0, :, j * LANES:(j + 1) * LANES] = (slab(j, True) * scale).astype(qn_ref.dtype)
        qw_ref[0, :, j * LANES:(j + 1) * LANES] = (slab(4 + j, True) * scale).astype(qw_ref.dtype)
    kc_ref[0] = slab(8, True)
    vc_ref[0] = slab(12, False)
    for j in range(3):
        kv_ref[0, :, j * LANES:(j + 1) * LANES] = slab(9 + j, True).astype(kv_ref.dtype)
    for j in range(3):
        kv_ref[0, :, (3 + j) * LANES:(4 + j) * LANES] = slab(13 + j, False).astype(kv_ref.dtype)
    gl_ref[0] = slab(16, False)


def _proj(x, gain, w, cos, sin):
    B, S, D = x.shape
    ts = min(PROJ_ROWS, S)
    row = lambda width: pl.BlockSpec((1, ts, width), lambda b, i: (b, i, 0))
    full = lambda a: pl.BlockSpec(a.shape, lambda b, i: (0,) * a.ndim)
    tab = pl.BlockSpec((ts, LANES), lambda b, i: (i, 0))
    f32, bf = jnp.float32, MXU_DTYPE
    return pl.pallas_call(
        _proj_kernel,
        grid=(B, S // ts),
        in_specs=[row(D), full(gain), full(w), tab, tab],
        out_specs=[row(Q_WIDTH), row(Q_WIDTH), row(LANES), row(LANES), row(6 * LANES), row(LANES)],
        out_shape=[jax.ShapeDtypeStruct((B, S, Q_WIDTH), bf), jax.ShapeDtypeStruct((B, S, Q_WIDTH), bf),
                   jax.ShapeDtypeStruct((B, S, LANES), f32), jax.ShapeDtypeStruct((B, S, LANES), f32),
                   jax.ShapeDtypeStruct((B, S, 6 * LANES), bf), jax.ShapeDtypeStruct((B, S, LANES), f32)],
        name="proj",
        compiler_params=_params("parallel", "parallel"),
    )(x, gain, w, cos, sin)


def _compress_kernel(kc_ref, vc_ref, pe_ref, w1_ref, w2_ref, ck_ref, cv_ref):
    n = kc_ref.shape[1] // CMP_STRIDE
    for kv, (src, dst) in enumerate(((kc_ref, ck_ref), (vc_ref, cv_ref))):
        ch = jnp.concatenate([src[0, pl.ds(p, n, stride=CMP_STRIDE), :] for p in range(CMP_STRIDE)], axis=1)
        top = jnp.dot((ch + pe_ref[kv, 0]).astype(MXU_DTYPE), w1_ref[kv, 0], preferred_element_type=jnp.float32)
        bot = jnp.dot((ch + pe_ref[kv, 1]).astype(MXU_DTYPE), w1_ref[kv, 1], preferred_element_type=jnp.float32)
        hid = top + pltpu.roll(bot, n - 1, 0)
        act = _gelu_tanh(hid).astype(MXU_DTYPE)
        dst[0] = jnp.dot(act, w2_ref[kv], preferred_element_type=jnp.float32).astype(dst.dtype)


def _compress(kc, vc, pe, w1, w2):
    B, S, width = kc.shape
    n = S // CMP_STRIDE
    src = pl.BlockSpec((1, S, width), lambda b: (b, 0, 0))
    full = lambda a: pl.BlockSpec(a.shape, lambda b: (0,) * a.ndim)
    dst = pl.BlockSpec((1, n, KV_WIDTH), lambda b: (b, 0, 0))
    return pl.pallas_call(
        _compress_kernel,
        grid=(B,),
        in_specs=[src, src, full(pe), full(w1), full(w2)],
        out_specs=[dst, dst],
        out_shape=[jax.ShapeDtypeStruct((B, n, KV_WIDTH), MXU_DTYPE)] * 2,
        name="compress",
        compiler_params=_params("parallel"),
    )(kc, vc, pe, w1, w2)


def _select_bias(score, *, live):
    tq = score.shape[1]
    groups = [score[SUBLANES * k:SUBLANES * (k + 1)] for k in range(live // SUBLANES)]
    rank = [jnp.zeros((SUBLANES, tq), jnp.int32) for _ in groups]
    sub = lax.broadcasted_iota(jnp.int32, (SUBLANES, tq), 0)
    for i in range(live):
        row = score[i:i + 1, :]
        for k, grp in enumerate(groups):
            if SUBLANES * k > i:
                beats = jnp.where(row >= grp, 1, 0)
            elif SUBLANES * k + SUBLANES - 1 <= i:
                beats = jnp.where(row > grp, 1, 0)
            else:
                beats = jnp.where(sub > i - SUBLANES * k, jnp.where(row >= grp, 1, 0), jnp.where(row > grp, 1, 0))
            rank[k] = rank[k] + beats
    return jnp.where(jnp.concatenate(rank, axis=0) < min(SLC_TOPK, live), 0.0, NEG)


def _cmp_kernel(q_ref, ck_ref, cv_ref, o_ref, sel_ref, *, tq, nsb):
    q0 = pl.program_id(1) * tq
    ncp = ck_ref.shape[1]
    nheads = KV_HEADS * GROUP
    heads = [qs[g * tq:(g + 1) * tq] for qs in _split_heads(q_ref[0]) for g in range(GROUP)]
    has_valid = (q0 + lax.broadcasted_iota(jnp.int32, (1, tq), 1) >= CMP_LEN - 1).astype(jnp.float32)
    ratio = SLC_BLOCK // CMP_STRIDE

    def attend(live):
        ncl = min(live * ratio, ncp)
        ck = ck_ref[0, 0:ncl, :]
        cv = cv_ref[0, 0:ncl, :]
        c = lax.broadcasted_iota(jnp.int32, (ncl, tq), 0)
        t = q0 + lax.broadcasted_iota(jnp.int32, (ncl, tq), 1)
        valid = c * CMP_STRIDE + (CMP_LEN - 1) <= t
        jo = lax.broadcasted_iota(jnp.int32, (live, ncl), 0)
        co = lax.broadcasted_iota(jnp.int32, (live, ncl), 1)
        overlap = ((co * CMP_STRIDE <= jo * SLC_BLOCK + SLC_BLOCK - 1)
                   & (co * CMP_STRIDE + CMP_LEN - 1 >= jo * SLC_BLOCK) & (co < ncp - 1)).astype(MXU_DTYPE)
        jb = lax.broadcasted_iota(jnp.int32, (live, tq), 0)
        tb = q0 + lax.broadcasted_iota(jnp.int32, (live, tq), 1)
        forced = (jb == 0) | (jb == tb // SLC_BLOCK) | (jb == tb // SLC_BLOCK - 1)
        causal = jb * SLC_BLOCK <= tb

        def finish(s):
            s = jnp.where(valid, s, NEG)
            e = jnp.exp2(s - jnp.max(s, axis=0, keepdims=True))
            r = has_valid / jnp.maximum(jnp.sum(e, axis=0, keepdims=True), 1e-30)
            return _dot_tn(cv, e.astype(MXU_DTYPE)) * r, e * r

        lookahead = 4 if ncl <= LANES else 2
        pending = [_dot_nt(ck, heads[n]) for n in range(lookahead)]
        outs, probs = [], []
        for n in range(nheads):
            if n + lookahead < nheads:
                pending.append(_dot_nt(ck, heads[n + lookahead]))
            o, p = finish(pending[n])
            outs.append(o)
            probs.append(p)
        sel_t = []
        for h in range(KV_HEADS):
            pg = functools.reduce(jnp.add, probs[h * GROUP:(h + 1) * GROUP])
            hi = pg.astype(MXU_DTYPE)
            lo = (pg - hi.astype(jnp.float32)).astype(MXU_DTYPE)
            imp = (jnp.dot(overlap, hi, preferred_element_type=jnp.float32)
                   + jnp.dot(overlap, lo, preferred_element_type=jnp.float32))
            bias = _select_bias(jnp.where(forced, BIG, jnp.where(causal, imp, NEG)), live=live)
            sel_t += [bias] + ([jnp.full((nsb - live, tq), NEG, jnp.float32)] if live < nsb else [])
        o_ref[0] = _merge_heads([jnp.concatenate(outs[h * GROUP:(h + 1) * GROUP], axis=1)
                                 for h in range(KV_HEADS)], tq).astype(o_ref.dtype)
        pad = [jnp.zeros((LANES - KV_HEADS * nsb, tq), jnp.float32)] if KV_HEADS * nsb < LANES else []
        sel_ref[0] = jnp.concatenate(sel_t + pad, axis=0).T.astype(sel_ref.dtype)

    extents = list(range(RANK_STEP, nsb + 1, RANK_STEP))
    which = jnp.minimum((q0 + tq - 1) // (RANK_STEP * SLC_BLOCK), len(extents) - 1)
    for v, live in enumerate(extents):
        pl.when(which == v)(functools.partial(attend, live))


def _cmp_attention(qn, ck, cv):
    B, S, _ = qn.shape
    tq = min(CMP_Q, S)
    nsb = S // SLC_BLOCK
    ncp = ck.shape[1]
    return pl.pallas_call(
        functools.partial(_cmp_kernel, tq=tq, nsb=nsb),
        grid=(B, S // tq),
        in_specs=[pl.BlockSpec((1, tq, Q_WIDTH), lambda b, i: (b, i, 0)),
                  pl.BlockSpec((1, ncp, KV_WIDTH), lambda b, i: (b, 0, 0)),
                  pl.BlockSpec((1, ncp, KV_WIDTH), lambda b, i: (b, 0, 0))],
        out_specs=[pl.BlockSpec((1, tq, Q_WIDTH), lambda b, i: (b, i, 0)),
                   pl.BlockSpec((1, tq, LANES), lambda b, i: (b, i, 0))],
        out_shape=[jax.ShapeDtypeStruct((B, S, Q_WIDTH), MXU_DTYPE),
                   jax.ShapeDtypeStruct((B, S, LANES), MXU_DTYPE)],
        name="cmp_select",
        compiler_params=_params("parallel", "parallel"),
    )(qn, ck, cv)


def _slc_kernel(q_ref, k_ref, v_ref, sel_ref, qn_ref, kn_ref, seln_ref, o_ref, *scratch, tq, nsb):
    nheads = KV_HEADS * GROUP
    m_scr, acc_scr, s_scr = (scratch[i * nheads:(i + 1) * nheads] for i in range(3))
    qi = pl.program_id(1)
    erow = lax.broadcasted_iota(jnp.int32, (tq, LANES), 0) // SLC_BLOCK
    ecol = lax.broadcasted_iota(jnp.int32, (tq, LANES), 1)
    ecol = jnp.where(ecol < KV_HEADS * nsb, ecol % nsb, -1)

    def queries(q, bias):
        col = lax.broadcasted_iota(jnp.int32, bias.shape, 1)
        zero = jnp.zeros_like(bias)
        out = []
        for h, qs in enumerate(_split_heads(q)):
            bias_h = jnp.where((col >= h * nsb) & (col < (h + 1) * nsb), bias, zero)
            out += [jnp.concatenate([qs[g * tq:(g + 1) * tq], bias_h], axis=1) for g in range(GROUP)]
        return out

    def keys(k, kt):
        onehot = jnp.where(ecol == erow + kt * (tq // SLC_BLOCK), 1.0, 0.0).astype(MXU_DTYPE)
        return jnp.concatenate([k, onehot], axis=1)

    def consume(n, s, v):
        m_prev = m_scr[n][...]
        m_new = jnp.maximum(m_prev, jnp.max(s, axis=0, keepdims=True))
        alpha = jnp.exp2(m_prev - m_new)
        p = jnp.exp2(s - m_new)
        acc_scr[n][...] = alpha * acc_scr[n][...] + _dot_tn(v[n // GROUP], p.astype(MXU_DTYPE))
        m_scr[n][...] = m_new

    def step(kt, q_next, k_next, mask):
        v = _values_with_ones(v_ref[0, pl.ds(pl.multiple_of(kt * tq, tq), tq), :])
        s_cur = s_scr[0][...]
        s_scr[0][...] = _dot_nt(k_next, q_next[0])
        for n in range(nheads):
            s = s_cur
            if n + 1 < nheads:
                s_cur = s_scr[n + 1][...]
                s_scr[n + 1][...] = _dot_nt(k_next, q_next[n + 1])
            consume(n, s if mask is None else jnp.where(mask, s, NEG), v)

    q_aug = queries(q_ref[0], sel_ref[0])
    for n in range(nheads):
        m_scr[n][...] = jnp.full(m_scr[n].shape, NEG, jnp.float32)
        acc_scr[n][...] = jnp.zeros(acc_scr[n].shape, jnp.float32)

    @pl.when((pl.program_id(0) == 0) & (qi == 0))
    def _():
        k_first = keys(k_ref[0, pl.ds(0, tq), :], 0)
        for n in range(nheads):
            s_scr[n][...] = _dot_nt(k_first, q_aug[n])

    def inner(kt):
        step(kt, q_aug, keys(k_ref[0, pl.ds(pl.multiple_of((kt + 1) * tq, tq), tq), :], kt + 1), None)

    def trip(i, carry):
        for j in range(SLC_UNROLL):
            inner(SLC_UNROLL * i + j)
        return carry

    lax.fori_loop(0, qi // SLC_UNROLL, trip, 0)
    for j in range(1, SLC_UNROLL):
        pl.when(qi % SLC_UNROLL >= j)(functools.partial(lambda j: inner(qi - qi % SLC_UNROLL + j - 1), j))

    causal = lax.broadcasted_iota(jnp.int32, (tq, tq), 0) <= lax.broadcasted_iota(jnp.int32, (tq, tq), 1)
    step(qi, queries(qn_ref[0], seln_ref[0]), keys(kn_ref[0], 0), causal)
    o_t = []
    for h in range(KV_HEADS):
        den_row = (1 - h) * HEAD_DIM
        o_t.append(jnp.concatenate([acc_scr[n][...] / acc_scr[n][den_row:den_row + 1, :]
                                    for n in range(h * GROUP, (h + 1) * GROUP)], axis=1))
    o_ref[0] = _merge_heads(o_t, tq).astype(o_ref.dtype)


def _slc_attention(qn, kv, sel):
    B, S, _ = qn.shape
    tq = min(SLC_Q, S)
    nq = S // tq
    nsb = S // SLC_BLOCK

    def following(b, i):
        wrap = i + 1 == nq
        return jnp.where(wrap, jnp.minimum(b + 1, B - 1), b), jnp.where(wrap, 0, i + 1)

    return pl.pallas_call(
        functools.partial(_slc_kernel, tq=tq, nsb=nsb),
        grid=(B, nq),
        in_specs=[pl.BlockSpec((1, tq, Q_WIDTH), lambda b, i: (b, i, 0)),
                  pl.BlockSpec((1, S, KV_WIDTH), lambda b, i: (b, 0, 0)),
                  pl.BlockSpec((1, S, KV_WIDTH), lambda b, i: (b, 0, 3)),
                  pl.BlockSpec((1, tq, LANES), lambda b, i: (b, i, 0)),
                  pl.BlockSpec((1, tq, Q_WIDTH), lambda b, i: (*following(b, i), 0)),
                  pl.BlockSpec((1, tq, KV_WIDTH), lambda b, i: (following(b, i)[0], 0, 0)),
                  pl.BlockSpec((1, tq, LANES), lambda b, i: (*following(b, i), 0))],
        out_specs=pl.BlockSpec((1, tq, Q_WIDTH), lambda b, i: (b, i, 0)),
        out_shape=jax.ShapeDtypeStruct((B, S, Q_WIDTH), MXU_DTYPE),
        scratch_shapes=([pltpu.VMEM((1, tq), jnp.float32)] * (KV_HEADS * GROUP)
                        + [pltpu.VMEM((KV_WIDTH, tq), jnp.float32)] * (KV_HEADS * GROUP)
                        + [pltpu.VMEM((tq, tq), jnp.float32)] * (KV_HEADS * GROUP)),
        name="slc",
        compiler_params=_params("arbitrary", "arbitrary"),
    )(qn, kv, kv, sel, qn, kv, sel)


def _band_kernel(*refs, tq, tiles, window, sinks):
    if sinks:
        sink_ref, q_ref, k_ref, v_ref, o_ref = refs
    else:
        q_ref, k_ref, v_ref, o_ref = refs
    q0 = pl.program_id(1) * (tiles * tq)
    span = window + tq
    nsub, nhalf, wsub = span // LANES, tq // LANES, window // LANES
    nheads = KV_HEADS * GROUP
    kl = lax.broadcasted_iota(jnp.int32, (LANES, LANES), 0)
    ql = lax.broadcasted_iota(jnp.int32, (LANES, LANES), 1)

    def band_mask(delta):
        if delta >= 1 or delta + wsub <= -1:
            return False
        if delta == 0:
            return kl <= ql
        return kl > ql if delta + wsub == 0 else None

    def attend(t, start, shift):
        heads = [qs[g * tq:(g + 1) * tq] for qs in _split_heads(q_ref[0, t * tq:(t + 1) * tq, :])
                 for g in range(GROUP)]
        k = k_ref[0, pl.ds(start, span), :]
        v = _values_with_ones(v_ref[0, pl.ds(start, span), :])

        def finish(n, s):
            cols, sink_terms = [], []
            den_row = (1 - n // GROUP) * HEAD_DIM
            for c in range(nhalf):
                live = {}
                for r in range(nsub):
                    mask = band_mask(r + shift - c)
                    if mask is False:
                        continue
                    blk = s[r * LANES:(r + 1) * LANES, c * LANES:(c + 1) * LANES]
                    live[r] = blk if mask is None else jnp.where(mask, blk, NEG)
                m = functools.reduce(jnp.maximum, [jnp.max(b, axis=0, keepdims=True) for b in live.values()])
                if sinks:
                    sk = jnp.full((1, LANES), sink_ref[n] * LOG2E, jnp.float32)
                    m = jnp.maximum(m, sk)
                e = {r: jnp.exp2(b - m) for r, b in live.items()}
                if sinks:
                    sink_terms.append(jnp.exp2(sk - m))
                zero = jnp.zeros((LANES, LANES), jnp.float32)
                cols.append(jnp.concatenate([e.get(r, zero) for r in range(nsub)], axis=0))
            p = jnp.concatenate(cols, axis=1).astype(MXU_DTYPE)
            pv = _dot_tn(v[n // GROUP], p)
            den = pv[den_row:den_row + 1, :]
            if sinks:
                den = den + jnp.concatenate(sink_terms, axis=1)
            return pv / den

        lookahead = 2
        pending = [_dot_nt(k, heads[n]) for n in range(lookahead)]
        outs = []
        for n in range(nheads):
            if n + lookahead < nheads:
                pending.append(_dot_nt(k, heads[n + lookahead]))
            outs.append(finish(n, pending[n]))
        o_ref[0, t * tq:(t + 1) * tq, :] = _merge_heads(
            [jnp.concatenate(outs[h * GROUP:(h + 1) * GROUP], axis=1) for h in range(KV_HEADS)], tq).astype(o_ref.dtype)

    @pl.when(q0 >= window)
    def _():
        for t in range(tiles):
            attend(t, pl.multiple_of(q0 + t * tq - window, LANES), -wsub)

    @pl.when(q0 < window)
    def _():
        for t in range(tiles):
            if t * tq >= window:
                attend(t, t * tq - window, -wsub)
            else:
                attend(t, 0, -(t * tq // LANES))


def _band_attention(q, kv, k_col, v_col, window, sinks):
    B, S, _ = q.shape
    tq = min(BAND_Q, S)
    tiles = BAND_TILES
    assert tiles * tq >= window and S % (tiles * tq) == 0
    in_specs = [pl.BlockSpec((1, tiles * tq, Q_WIDTH), lambda b, i: (b, i, 0)),
                pl.BlockSpec((1, S, KV_WIDTH), lambda b, i: (b, 0, k_col)),
                pl.BlockSpec((1, S, KV_WIDTH), lambda b, i: (b, 0, v_col))]
    args = [q, kv, kv]
    if sinks is not None:
        in_specs = [pl.BlockSpec(memory_space=pltpu.SMEM)] + in_specs
        args = [sinks] + args
    return pl.pallas_call(
        functools.partial(_band_kernel, tq=tq, tiles=tiles, window=window, sinks=sinks is not None),
        grid=(B, S // (tiles * tq)),
        in_specs=in_specs,
        out_specs=pl.BlockSpec((1, tiles * tq, Q_WIDTH), lambda b, i: (b, i, 0)),
        out_shape=jax.ShapeDtypeStruct((B, S, Q_WIDTH), MXU_DTYPE),
        name="band_sink" if sinks is not None else "band",
        compiler_params=_params("parallel", "parallel"),
    )(*args)


def _mix_kernel(x_ref, oc_ref, os_ref, ow_ref, osw_ref, gl_ref, ge_ref, w_ref, gn_ref, gs_ref, gp_ref, h_ref):
    sig = 1.0 / (1.0 + jnp.exp(-gl_ref[0]))
    hi = sig.astype(MXU_DTYPE)
    lo = (sig - hi.astype(jnp.float32)).astype(MXU_DTYPE)
    gates = jnp.dot(jnp.concatenate([hi, lo], axis=1), ge_ref[...],
                    preferred_element_type=jnp.float32)
    f32 = jnp.float32
    o_nsa = (gates[:, 0:Q_WIDTH] * oc_ref[0].astype(f32) + gates[:, Q_WIDTH:2 * Q_WIDTH] * os_ref[0].astype(f32)
             + gates[:, 2 * Q_WIDTH:3 * Q_WIDTH] * ow_ref[0].astype(f32))
    cat = jnp.concatenate([_rms(o_nsa, gn_ref[...]), _rms(osw_ref[0].astype(f32), gs_ref[...])],
                          axis=1).astype(MXU_DTYPE)
    mixed = jnp.dot(cat, w_ref[...], preferred_element_type=jnp.float32)
    h_ref[0] = x_ref[0] + _rms(mixed, gp_ref[...])


def _mix(x, o_cmp, o_slc, o_win, o_swa, gl, gate_expand, w_out, g_nsa, g_swa, g_post):
    B, S, D = x.shape
    ts = min(MIX_ROWS, S)
    row = lambda width: pl.BlockSpec((1, ts, width), lambda b, i: (b, i, 0))
    full = lambda a: pl.BlockSpec(a.shape, lambda b, i: (0,) * a.ndim)
    return pl.pallas_call(
        _mix_kernel,
        grid=(B, S // ts),
        in_specs=[row(D), row(Q_WIDTH), row(Q_WIDTH), row(Q_WIDTH), row(Q_WIDTH), row(LANES),
                  full(gate_expand), full(w_out), full(g_nsa), full(g_swa), full(g_post)],
        out_specs=row(D),
        out_shape=jax.ShapeDtypeStruct((B, S, D), jnp.float32),
        name="mix",
        compiler_params=_params("parallel", "parallel"),
    )(x, o_cmp, o_slc, o_win, o_swa, gl, gate_expand, w_out, g_nsa, g_swa, g_post)


def _ffn_kernel(h_ref, halo_ref, gpre_ref, wup_ref, cw_ref, cb_ref, wdn_ref, gpost_ref, y_ref, act_scr,
                *, ts, d_ff, chunk):
    rows = ts // FFN_PARTS
    sub = lax.broadcasted_iota(jnp.int32, (SUBLANES, chunk), 0)
    live = (pl.program_id(1) > 0).astype(jnp.float32)
    halo_hn = _rms(halo_ref[0], gpre_ref[...]) * live
    lhs = []
    for part in range(FFN_PARTS):
        hn = _rms(h_ref[0, part * rows:(part + 1) * rows, :], gpre_ref[...])
        lhs.append(jnp.concatenate([halo_hn, pltpu.einshape("(sa)d->(as)d", hn, s=SUBLANES)],
                                   axis=0).astype(MXU_DTYPE))
        halo_hn = hn[rows - HALO:]

    def conv(hn, col):
        u = jnp.dot(hn, wup_ref[:, col:col + chunk], preferred_element_type=jnp.float32)
        halo, u = u[:HALO], u[HALO:]
        t1 = jnp.where(sub == 0, halo[HALO - 1:HALO], pltpu.roll(u[rows - SUBLANES:], 1, 0))
        t2 = jnp.where(sub == 0, halo[HALO - 2:HALO - 1], pltpu.roll(u[rows - 2 * SUBLANES:rows - SUBLANES], 1, 0))
        tap1 = jnp.concatenate([t1, u[:rows - SUBLANES]], axis=0)
        tap2 = jnp.concatenate([t2, t1, u[:rows - 2 * SUBLANES]], axis=0)
        w = cw_ref[:, col:col + chunk]
        return w[0:1] * tap2 + w[1:2] * tap1 + w[2:3] * u + cb_ref[:, col:col + chunk]

    for part in range(FFN_PARTS):
        for col in range(0, d_ff, chunk):
            act = _gelu_tanh(conv(lhs[part], col)) * conv(lhs[part], d_ff + col)
            act_scr[part * rows:(part + 1) * rows, col:col + chunk] = act.astype(MXU_DTYPE)
    ys = [jnp.dot(act_scr[part * rows:(part + 1) * rows, :], wdn_ref[...], preferred_element_type=jnp.float32)
          for part in range(FFN_PARTS)]
    for part in range(FFN_PARTS):
        y = pltpu.einshape("(as)d->(sa)d", ys[part], s=SUBLANES)
        y_ref[0, part * rows:(part + 1) * rows, :] = (h_ref[0, part * rows:(part + 1) * rows, :]
                                                      + _rms(y, gpost_ref[...]))


def _ffn(h, g_pre, w_up, conv_w, conv_b, w_down, g_post):
    B, S, D = h.shape
    ts = min(FFN_ROWS, S)
    d_ff = w_down.shape[0]
    full = lambda a: pl.BlockSpec(a.shape, lambda b, i: (0,) * a.ndim, pipeline_mode=pl.Buffered(1))
    return pl.pallas_call(
        functools.partial(_ffn_kernel, ts=ts, d_ff=d_ff, chunk=FFN_CHUNK),
        grid=(B, S // ts),
        in_specs=[pl.BlockSpec((1, ts, D), lambda b, i: (b, i, 0)),
                  pl.BlockSpec((1, HALO, D), lambda b, i: (b, jnp.maximum(i * (ts // HALO) - 1, 0), 0)),
                  full(g_pre), full(w_up), full(conv_w), full(conv_b), full(w_down), full(g_post)],
        out_specs=pl.BlockSpec((1, ts, D), lambda b, i: (b, i, 0)),
        out_shape=jax.ShapeDtypeStruct((B, S, D), jnp.float32),
        scratch_shapes=[pltpu.VMEM((ts, d_ff), MXU_DTYPE)],
        name="ffn",
        compiler_params=_params("parallel", "parallel"),
    )(h, h, g_pre, w_up, conv_w, conv_b, w_down, g_post)


def _rope_tables(S):
    half = HEAD_DIM // 2
    inv = ROPE_THETA ** (-jnp.arange(half, dtype=jnp.float32) / half)
    ang = jnp.arange(S).astype(jnp.float32)[:, None] * inv[None, :]
    cos, sin = jnp.cos(ang), jnp.sin(ang)
    reps = LANES // HEAD_DIM
    return jnp.tile(jnp.concatenate([cos, cos], axis=1), (1, reps)), jnp.tile(jnp.concatenate([-sin, sin], axis=1), (1, reps))


def _layer(x, w_in, w_out, attn_pre_norm, attn_post_norm, nsa_out_norm, swa_out_norm, cmp_pos, cmp_w1, cmp_w2,
           swa_sinks, ffn_pre_norm, ffn_post_norm, w_up, conv_w, conv_b, w_down):
    B, S, D = x.shape
    f32 = jnp.float32
    perm = _q_perm()
    n_gate = KV_HEADS * GROUP * NSA_BRANCHES
    sizes = [Q_WIDTH] + [KV_WIDTH] * 6 + [n_gate] + [Q_WIDTH, KV_WIDTH, KV_WIDTH]
    q_n, k_c, v_c, k_s, v_s, k_w, v_w, g_n, q_w, k_sw, v_sw = np.cumsum([0] + sizes)[:-1]
    slab = np.arange(KV_WIDTH)
    cols = np.concatenate([q_n + perm, q_w + perm] + [o + slab for o in (k_c, k_s, k_w, k_sw, v_c, v_s, v_w, v_sw)]
                          + [g_n + np.arange(n_gate), np.zeros(LANES - n_gate, np.int64)])
    live = np.arange(cols.size) < cols.size - (LANES - n_gate)
    w_all = jnp.where(live, jnp.take(w_in, jnp.asarray(cols, jnp.int32), axis=1), 0.0).astype(MXU_DTYPE)
    cos, sin = _rope_tables(S)
    row = lambda g: g.reshape(1, -1).astype(f32)

    qn, qw, kc, vc, kv, gl = _proj(x, row(attn_pre_norm), w_all, cos, sin)

    pe = jnp.broadcast_to(cmp_pos.reshape(2, 2, CMP_STRIDE, 1, HEAD_DIM), (2, 2, CMP_STRIDE, KV_HEADS, HEAD_DIM))
    pe = pe.reshape(2, 2, 1, CMP_STRIDE * KV_WIDTH)
    w1 = cmp_w1.astype(MXU_DTYPE).reshape(2, 2, CMP_STRIDE, HEAD_DIM, CMP_HIDDEN)
    w1x = jnp.stack([jnp.concatenate([w1, jnp.zeros_like(w1)], axis=-1),
                     jnp.concatenate([jnp.zeros_like(w1), w1], axis=-1)], axis=3)
    w1x = w1x.reshape(2, 2, CMP_STRIDE * KV_WIDTH, KV_HEADS * CMP_HIDDEN)
    w2 = cmp_w2.astype(MXU_DTYPE)
    w2x = jnp.concatenate([jnp.concatenate([w2, jnp.zeros_like(w2)], axis=-1),
                           jnp.concatenate([jnp.zeros_like(w2), w2], axis=-1)], axis=1)
    ck, cv = _compress(kc, vc, pe, w1x, w2x)

    o_cmp, sel = _cmp_attention(qn, ck, cv)
    o_slc = _slc_attention(qn, kv, sel)
    o_win = _band_attention(qn, kv, 1, 4, NSA_WINDOW, None)
    o_swa = _band_attention(qw, kv, 2, 5, SWA_WINDOW, swa_sinks.astype(f32))

    n = np.arange(NSA_BRANCHES * Q_WIDTH)
    br, g, h = n // Q_WIDTH, (n % Q_WIDTH) // LANES, (n % LANES) // HEAD_DIM
    expand = np.zeros((LANES, NSA_BRANCHES * Q_WIDTH), np.float32)
    expand[(h * GROUP + g) * NSA_BRANCHES + br, n] = 1.0
    expand = np.concatenate([expand, expand], axis=0)
    w_out_p = jnp.take(w_out, jnp.asarray(np.concatenate([perm, Q_WIDTH + perm]), jnp.int32),
                       axis=0).astype(MXU_DTYPE)
    hmid = _mix(x, o_cmp, o_slc, o_win, o_swa, gl, jnp.asarray(expand, MXU_DTYPE), w_out_p,
                row(nsa_out_norm[perm]), row(swa_out_norm[perm]), row(attn_post_norm))

    return _ffn(hmid, row(ffn_pre_norm), w_up.astype(MXU_DTYPE), conv_w.astype(f32), row(conv_b),
                w_down.astype(MXU_DTYPE), row(ffn_post_norm))


def kernel(x, w_in, w_out, attn_pre_norm, attn_post_norm, nsa_out_norm, swa_out_norm, cmp_pos, cmp_w1, cmp_w2,
           swa_sinks, ffn_pre_norm, ffn_post_norm, w_up, conv_w, conv_b, w_down):
    h = x
    for l in range(w_in.shape[0]):
        h = _layer(h, w_in[l], w_out[l], attn_pre_norm[l], attn_post_norm[l], nsa_out_norm[l], swa_out_norm[l],
                   cmp_pos[l], cmp_w1[l], cmp_w2[l], swa_sinks[l], ffn_pre_norm[l], ffn_post_norm[l],
                   w_up[l], conv_w[l], conv_b[l], w_down[l])
    return h
```

```python
import functools

import numpy as np
import jax
import jax.numpy as jnp
from jax import lax
from jax.experimental import pallas as pl
from jax.experimental.pallas import tpu as pltpu

HEAD_DIM = 64
KV_HEADS = 2
GROUP = 4
Q_WIDTH = KV_HEADS * GROUP * HEAD_DIM
KV_WIDTH = KV_HEADS * HEAD_DIM
SWA_WINDOW = 128
NSA_WINDOW = 512
CMP_LEN = 32
CMP_STRIDE = 16
CMP_HIDDEN = 256
SLC_BLOCK = 64
SLC_TOPK = 16
NSA_BRANCHES = 3
D_FF = 2816
ROPE_THETA = 10000.0
RMS_EPS = 1e-6
NEG = -1e30
BIG = 1e9
LOG2E = float(np.log2(np.e))

LANES = 128
SUBLANES = 8
HALO = 8
MXU_DTYPE = jnp.bfloat16
VMEM_LIMIT = 56 * 1024 * 1024

PROJ_ROWS = 512
CMP_Q = 256
SLC_Q = 256
SLC_UNROLL = 4
BAND_Q = 256
BAND_TILES = 4
MIX_ROWS = 512
FFN_ROWS = 1024
FFN_PARTS = 2
FFN_CHUNK = 256
RANK_STEP = 16

_N_ROPE_SLABS = 12
_N_QSLABS = 8
PROJ_WIDTH = 2 * Q_WIDTH + 9 * LANES


def _q_perm():
    n = np.arange(Q_WIDTH)
    g, h, d = n // 128, (n % 128) // 64, n % 64
    return (h * GROUP + g) * HEAD_DIM + d


def _params(*sem):
    return pltpu.CompilerParams(dimension_semantics=sem, vmem_limit_bytes=VMEM_LIMIT)


def _rms(xf, g):
    return xf * lax.rsqrt(jnp.mean(xf * xf, axis=-1, keepdims=True) + RMS_EPS) * g


def _gelu_tanh(x):
    return 0.5 * x * (1.0 + jnp.tanh(np.sqrt(2.0 / np.pi).astype(np.float32) * (x + 0.044715 * (x * x * x))))


def _dot_nt(a, b):
    return lax.dot_general(a, b, (((1,), (1,)), ((), ())), preferred_element_type=jnp.float32)


def _dot_tn(a, b):
    return lax.dot_general(a, b, (((0,), (0,)), ((), ())), preferred_element_type=jnp.float32)


def _split_heads(q):
    qs = jnp.concatenate([q[:, g * LANES:(g + 1) * LANES] for g in range(GROUP)], axis=0)
    lane = lax.broadcasted_iota(jnp.int32, qs.shape, 1)
    zero = jnp.zeros_like(qs)
    return [jnp.where((lane >= h * HEAD_DIM) & (lane < (h + 1) * HEAD_DIM), qs, zero) for h in range(KV_HEADS)]


def _values_with_ones(v):
    lane = lax.broadcasted_iota(jnp.int32, v.shape, 1)
    one = jnp.ones_like(v)
    return [jnp.where((lane >= h * HEAD_DIM) & (lane < (h + 1) * HEAD_DIM), v, one) for h in range(KV_HEADS)]


def _merge_heads(o_t, tq):
    sub = lax.broadcasted_iota(jnp.int32, o_t[0].shape, 0)
    both = jnp.where(sub < HEAD_DIM, o_t[0], o_t[1])
    return jnp.concatenate([both[:, g * tq:(g + 1) * tq].T for g in range(GROUP)], axis=1)


def _proj_kernel(x_ref, g_ref, w_ref, cos_ref, sin_ref, qn_ref, qw_ref, kc_ref, vc_ref, kv_ref, gl_ref):
    hn = _rms(x_ref[0], g_ref[...]).astype(MXU_DTYPE)
    p = jnp.dot(hn, w_ref[...], preferred_element_type=jnp.float32)
    cos, sin = cos_ref[...], sin_ref[...]
    lane = lax.broadcasted_iota(jnp.int32, cos.shape, 1)
    first_half = (lane % HEAD_DIM) < (HEAD_DIM // 2)

    def slab(j, rope):
        z = p[:, j * LANES:(j + 1) * LANES]
        if rope:
            swapped = jnp.where(first_half, pltpu.roll(z, LANES - HEAD_DIM // 2, 1), pltpu.roll(z, HEAD_DIM // 2, 1))
            z = z * cos + swapped * sin
        return z

    scale = HEAD_DIM ** -0.5 * LOG2E
    for j in range(4):
        qn_ref[0, :, j * LANES:(j + 1) * LANES] = (slab(j, True) * scale).astype(qn_ref.dtype)
        qw_ref[0, :, j * LANES:(j + 1) * LANES] = (slab(4 + j, True) * scale).astype(qw_ref.dtype)
    kc_ref[0] = slab(8, True)
    vc_ref[0] = slab(12, False)
    for j in range(3):
        kv_ref[0, :, j * LANES:(j + 1) * LANES] = slab(9 + j, True).astype(kv_ref.dtype)
    for j in range(3):
        kv_ref[0, :, (3 + j) * LANES:(4 + j) * LANES] = slab(13 + j, False).astype(kv_ref.dtype)
    gl_ref[0] = slab(16, False)


def _proj(x, gain, w, cos, sin):
    B, S, D = x.shape
    ts = min(PROJ_ROWS, S)
    row = lambda width: pl.BlockSpec((1, ts, width), lambda b, i: (b, i, 0))
    full = lambda a: pl.BlockSpec(a.shape, lambda b, i: (0,) * a.ndim)
    tab = pl.BlockSpec((ts, LANES), lambda b, i: (i, 0))
    f32, bf = jnp.float32, MXU_DTYPE
    return pl.pallas_call(
        _proj_kernel,
        grid=(B, S // ts),
        in_specs=[row(D), full(gain), full(w), tab, tab],
        out_specs=[row(Q_WIDTH), row(Q_WIDTH), row(LANES), row(LANES), row(6 * LANES), row(LANES)],
        out_shape=[jax.ShapeDtypeStruct((B, S, Q_WIDTH), bf), jax.ShapeDtypeStruct((B, S, Q_WIDTH), bf),
                   jax.ShapeDtypeStruct((B, S, LANES), f32), jax.ShapeDtypeStruct((B, S, LANES), f32),
                   jax.ShapeDtypeStruct((B, S, 6 * LANES), bf), jax.ShapeDtypeStruct((B, S, LANES), f32)],
        name="proj",
        compiler_params=_params("parallel", "parallel"),
    )(x, gain, w, cos, sin)


def _compress_kernel(kc_ref, vc_ref, pe_ref, w1_ref, w2_ref, ck_ref, cv_ref):
    n = kc_ref.shape[1] // CMP_STRIDE
    for kv, (src, dst) in enumerate(((kc_ref, ck_ref), (vc_ref, cv_ref))):
        ch = jnp.concatenate([src[0, pl.ds(p, n, stride=CMP_STRIDE), :] for p in range(CMP_STRIDE)], axis=1)
        top = jnp.dot((ch + pe_ref[kv, 0]).astype(MXU_DTYPE), w1_ref[kv, 0], preferred_element_type=jnp.float32)
        bot = jnp.dot((ch + pe_ref[kv, 1]).astype(MXU_DTYPE), w1_ref[kv, 1], preferred_element_type=jnp.float32)
        hid = top + pltpu.roll(bot, n - 1, 0)
        act = _gelu_tanh(hid).astype(MXU_DTYPE)
        dst[0] = jnp.dot(act, w2_ref[kv], preferred_element_type=jnp.float32).astype(dst.dtype)


def _compress(kc, vc, pe, w1, w2):
    B, S, width = kc.shape
    n = S // CMP_STRIDE
    src = pl.BlockSpec((1, S, width), lambda b: (b, 0, 0))
    full = lambda a: pl.BlockSpec(a.shape, lambda b: (0,) * a.ndim)
    dst = pl.BlockSpec((1, n, KV_WIDTH), lambda b: (b, 0, 0))
    return pl.pallas_call(
        _compress_kernel,
        grid=(B,),
        in_specs=[src, src, full(pe), full(w1), full(w2)],
        out_specs=[dst, dst],
        out_shape=[jax.ShapeDtypeStruct((B, n, KV_WIDTH), MXU_DTYPE)] * 2,
        name="compress",
        compiler_params=_params("parallel"),
    )(kc, vc, pe, w1, w2)


def _select_bias(score, *, live):
    tq = score.shape[1]
    groups = [score[SUBLANES * k:SUBLANES * (k + 1)] for k in range(live // SUBLANES)]
    rank = [jnp.zeros((SUBLANES, tq), jnp.int32) for _ in groups]
    sub = lax.broadcasted_iota(jnp.int32, (SUBLANES, tq), 0)
    for i in range(live):
        row = score[i:i + 1, :]
        for k, grp in enumerate(groups):
            if SUBLANES * k > i:
                beats = jnp.where(row >= grp, 1, 0)
            elif SUBLANES * k + SUBLANES - 1 <= i:
                beats = jnp.where(row > grp, 1, 0)
            else:
                beats = jnp.where(sub > i - SUBLANES * k, jnp.where(row >= grp, 1, 0), jnp.where(row > grp, 1, 0))
            rank[k] = rank[k] + beats
    return jnp.where(jnp.concatenate(rank, axis=0) < min(SLC_TOPK, live), 0.0, NEG)


def _cmp_kernel(q_ref, ck_ref, cv_ref, o_ref, sel_ref, *, tq, nsb):
    q0 = pl.program_id(1) * tq
    ncp = ck_ref.shape[1]
    nheads = KV_HEADS * GROUP
    heads = [qs[g * tq:(g + 1) * tq] for qs in _split_heads(q_ref[0]) for g in range(GROUP)]
    has_valid = (q0 + lax.broadcasted_iota(jnp.int32, (1, tq), 1) >= CMP_LEN - 1).astype(jnp.float32)
    ratio = SLC_BLOCK // CMP_STRIDE

    def attend(live):
        ncl = min(live * ratio, ncp)
        ck = ck_ref[0, 0:ncl, :]
        cv = cv_ref[0, 0:ncl, :]
        c = lax.broadcasted_iota(jnp.int32, (ncl, tq), 0)
        t = q0 + lax.broadcasted_iota(jnp.int32, (ncl, tq), 1)
        valid = c * CMP_STRIDE + (CMP_LEN - 1) <= t
        jo = lax.broadcasted_iota(jnp.int32, (live, ncl), 0)
        co = lax.broadcasted_iota(jnp.int32, (live, ncl), 1)
        overlap = ((co * CMP_STRIDE <= jo * SLC_BLOCK + SLC_BLOCK - 1)
                   & (co * CMP_STRIDE + CMP_LEN - 1 >= jo * SLC_BLOCK) & (co < ncp - 1)).astype(MXU_DTYPE)
        jb = lax.broadcasted_iota(jnp.int32, (live, tq), 0)
        tb = q0 + lax.broadcasted_iota(jnp.int32, (live, tq), 1)
        forced = (jb == 0) | (jb == tb // SLC_BLOCK) | (jb == tb // SLC_BLOCK - 1)
        causal = jb * SLC_BLOCK <= tb

        def finish(s):
            s = jnp.where(valid, s, NEG)
            e = jnp.exp2(s - jnp.max(s, axis=0, keepdims=True))
            r = has_valid / jnp.maximum(jnp.sum(e, axis=0, keepdims=True), 1e-30)
            return _dot_tn(cv, e.astype(MXU_DTYPE)) * r, e * r

        lookahead = 4 if ncl <= LANES else 3
        pending = [_dot_nt(ck, heads[n]) for n in range(lookahead)]
        outs, probs = [], []
        for n in range(nheads):
            if n + lookahead < nheads:
                pending.append(_dot_nt(ck, heads[n + lookahead]))
            o, p = finish(pending[n])
            outs.append(o)
            probs.append(p)
        sel_t = []
        for h in range(KV_HEADS):
            pg = functools.reduce(jnp.add, probs[h * GROUP:(h + 1) * GROUP])
            hi = pg.astype(MXU_DTYPE)
            lo = (pg - hi.astype(jnp.float32)).astype(MXU_DTYPE)
            imp = (jnp.dot(overlap, hi, preferred_element_type=jnp.float32)
                   + jnp.dot(overlap, lo, preferred_element_type=jnp.float32))
            bias = _select_bias(jnp.where(forced, BIG, jnp.where(causal, imp, NEG)), live=live)
            sel_t += [bias] + ([jnp.full((nsb - live, tq), NEG, jnp.float32)] if live < nsb else [])
        o_ref[0] = _merge_heads([jnp.concatenate(outs[h * GROUP:(h + 1) * GROUP], axis=1)
                                 for h in range(KV_HEADS)], tq).astype(o_ref.dtype)
        pad = [jnp.zeros((LANES - KV_HEADS * nsb, tq), jnp.float32)] if KV_HEADS * nsb < LANES else []
        sel_ref[0] = jnp.concatenate(sel_t + pad, axis=0).T.astype(sel_ref.dtype)

    extents = list(range(RANK_STEP, nsb + 1, RANK_STEP))
    which = jnp.minimum((q0 + tq - 1) // (RANK_STEP * SLC_BLOCK), len(extents) - 1)
    for v, live in enumerate(extents):
        pl.when(which == v)(functools.partial(attend, live))


def _cmp_attention(qn, ck, cv):
    B, S, _ = qn.shape
    tq = min(CMP_Q, S)
    nsb = S // SLC_BLOCK
    ncp = ck.shape[1]
    return pl.pallas_call(
        functools.partial(_cmp_kernel, tq=tq, nsb=nsb),
        grid=(B, S // tq),
        in_specs=[pl.BlockSpec((1, tq, Q_WIDTH), lambda b, i: (b, i, 0)),
                  pl.BlockSpec((1, ncp, KV_WIDTH), lambda b, i: (b, 0, 0)),
                  pl.BlockSpec((1, ncp, KV_WIDTH), lambda b, i: (b, 0, 0))],
        out_specs=[pl.BlockSpec((1, tq, Q_WIDTH), lambda b, i: (b, i, 0)),
                   pl.BlockSpec((1, tq, LANES), lambda b, i: (b, i, 0))],
        out_shape=[jax.ShapeDtypeStruct((B, S, Q_WIDTH), MXU_DTYPE),
                   jax.ShapeDtypeStruct((B, S, LANES), MXU_DTYPE)],
        name="cmp_select",
        compiler_params=_params("parallel", "parallel"),
    )(qn, ck, cv)


def _slc_kernel(q_ref, k_ref, v_ref, sel_ref, qn_ref, kn_ref, seln_ref, o_ref, *scratch, tq, nsb):
    nheads = KV_HEADS * GROUP
    m_scr, acc_scr, s_scr = (scratch[i * nheads:(i + 1) * nheads] for i in range(3))
    qi = pl.program_id(1)
    erow = lax.broadcasted_iota(jnp.int32, (tq, LANES), 0) // SLC_BLOCK
    ecol = lax.broadcasted_iota(jnp.int32, (tq, LANES), 1)
    ecol = jnp.where(ecol < KV_HEADS * nsb, ecol % nsb, -1)

    def queries(q, bias):
        col = lax.broadcasted_iota(jnp.int32, bias.shape, 1)
        zero = jnp.zeros_like(bias)
        out = []
        for h, qs in enumerate(_split_heads(q)):
            bias_h = jnp.where((col >= h * nsb) & (col < (h + 1) * nsb), bias, zero)
            out += [jnp.concatenate([qs[g * tq:(g + 1) * tq], bias_h], axis=1) for g in range(GROUP)]
        return out

    def keys(k, kt):
        onehot = jnp.where(ecol == erow + kt * (tq // SLC_BLOCK), 1.0, 0.0).astype(MXU_DTYPE)
        return jnp.concatenate([k, onehot], axis=1)

    def consume(n, s, v):
        m_prev = m_scr[n][...]
        m_new = jnp.maximum(m_prev, jnp.max(s, axis=0, keepdims=True))
        alpha = jnp.exp2(m_prev - m_new)
        p = jnp.exp2(s - m_new)
        acc_scr[n][...] = alpha * acc_scr[n][...] + _dot_tn(v[n // GROUP], p.astype(MXU_DTYPE))
        m_scr[n][...] = m_new

    def step(kt, q_next, k_next, mask):
        v = _values_with_ones(v_ref[0, pl.ds(pl.multiple_of(kt * tq, tq), tq), :])
        s_cur = s_scr[0][...]
        s_scr[0][...] = _dot_nt(k_next, q_next[0])
        for n in range(nheads):
            s = s_cur
            if n + 1 < nheads:
                s_cur = s_scr[n + 1][...]
                s_scr[n + 1][...] = _dot_nt(k_next, q_next[n + 1])
            consume(n, s if mask is None else jnp.where(mask, s, NEG), v)

    q_aug = queries(q_ref[0], sel_ref[0])
    for n in range(nheads):
        m_scr[n][...] = jnp.full(m_scr[n].shape, NEG, jnp.float32)
        acc_scr[n][...] = jnp.zeros(acc_scr[n].shape, jnp.float32)

    @pl.when((pl.program_id(0) == 0) & (qi == 0))
    def _():
        k_first = keys(k_ref[0, pl.ds(0, tq), :], 0)
        for n in range(nheads):
            s_scr[n][...] = _dot_nt(k_first, q_aug[n])

    def inner(kt):
        step(kt, q_aug, keys(k_ref[0, pl.ds(pl.multiple_of((kt + 1) * tq, tq), tq), :], kt + 1), None)

    def trip(i, carry):
        for j in range(SLC_UNROLL):
            inner(SLC_UNROLL * i + j)
        return carry

    lax.fori_loop(0, qi // SLC_UNROLL, trip, 0)
    for j in range(1, SLC_UNROLL):
        pl.when(qi % SLC_UNROLL >= j)(functools.partial(lambda j: inner(qi - qi % SLC_UNROLL + j - 1), j))

    causal = lax.broadcasted_iota(jnp.int32, (tq, tq), 0) <= lax.broadcasted_iota(jnp.int32, (tq, tq), 1)
    step(qi, queries(qn_ref[0], seln_ref[0]), keys(kn_ref[0], 0), causal)
    o_t = []
    for h in range(KV_HEADS):
        den_row = (1 - h) * HEAD_DIM
        o_t.append(jnp.concatenate([acc_scr[n][...] / acc_scr[n][den_row:den_row + 1, :]
                                    for n in range(h * GROUP, (h + 1) * GROUP)], axis=1))
    o_ref[0] = _merge_heads(o_t, tq).astype(o_ref.dtype)


def _slc_attention(qn, kv, sel):
    B, S, _ = qn.shape
    tq = min(SLC_Q, S)
    nq = S // tq
    nsb = S // SLC_BLOCK

    def following(b, i):
        wrap = i + 1 == nq
        return jnp.where(wrap, jnp.minimum(b + 1, B - 1), b), jnp.where(wrap, 0, i + 1)

    return pl.pallas_call(
        functools.partial(_slc_kernel, tq=tq, nsb=nsb),
        grid=(B, nq),
        in_specs=[pl.BlockSpec((1, tq, Q_WIDTH), lambda b, i: (b, i, 0)),
                  pl.BlockSpec((1, S, KV_WIDTH), lambda b, i: (b, 0, 0)),
                  pl.BlockSpec((1, S, KV_WIDTH), lambda b, i: (b, 0, 3)),
                  pl.BlockSpec((1, tq, LANES), lambda b, i: (b, i, 0)),
                  pl.BlockSpec((1, tq, Q_WIDTH), lambda b, i: (*following(b, i), 0)),
                  pl.BlockSpec((1, tq, KV_WIDTH), lambda b, i: (following(b, i)[0], 0, 0)),
                  pl.BlockSpec((1, tq, LANES), lambda b, i: (*following(b, i), 0))],
        out_specs=pl.BlockSpec((1, tq, Q_WIDTH), lambda b, i: (b, i, 0)),
        out_shape=jax.ShapeDtypeStruct((B, S, Q_WIDTH), MXU_DTYPE),
        scratch_shapes=([pltpu.VMEM((1, tq), jnp.float32)] * (KV_HEADS * GROUP)
                        + [pltpu.VMEM((KV_WIDTH, tq), jnp.float32)] * (KV_HEADS * GROUP)
                        + [pltpu.VMEM((tq, tq), jnp.float32)] * (KV_HEADS * GROUP)),
        name="slc",
        compiler_params=_params("arbitrary", "arbitrary"),
    )(qn, kv, kv, sel, qn, kv, sel)


def _band_kernel(*refs, tq, tiles, window, sinks):
    if sinks:
        sink_ref, q_ref, k_ref, v_ref, o_ref = refs
    else:
        q_ref, k_ref, v_ref, o_ref = refs
    q0 = pl.program_id(1) * (tiles * tq)
    span = window + tq
    nsub, nhalf, wsub = span // LANES, tq // LANES, window // LANES
    nheads = KV_HEADS * GROUP
    kl = lax.broadcasted_iota(jnp.int32, (LANES, LANES), 0)
    ql = lax.broadcasted_iota(jnp.int32, (LANES, LANES), 1)

    def band_mask(delta):
        if delta >= 1 or delta + wsub <= -1:
            return False
        if delta == 0:
            return kl <= ql
        return kl > ql if delta + wsub == 0 else None

    def attend(spans):
        heads = [[qs[g * tq:(g + 1) * tq] for qs in _split_heads(q_ref[0, t * tq:(t + 1) * tq, :])
                  for g in range(GROUP)] for t in range(tiles)]
        keys = [k_ref[0, pl.ds(start, span), :] for start, _ in spans]
        values = [_values_with_ones(v_ref[0, pl.ds(start, span), :]) for start, _ in spans]

        def finish(t, n, s):
            shift, v = spans[t][1], values[t]
            cols, sink_terms = [], []
            den_row = (1 - n // GROUP) * HEAD_DIM
            for c in range(nhalf):
                live = {}
                for r in range(nsub):
                    mask = band_mask(r + shift - c)
                    if mask is False:
                        continue
                    blk = s[r * LANES:(r + 1) * LANES, c * LANES:(c + 1) * LANES]
                    live[r] = blk if mask is None else jnp.where(mask, blk, NEG)
                m = functools.reduce(jnp.maximum, [jnp.max(b, axis=0, keepdims=True) for b in live.values()])
                if sinks:
                    sk = jnp.full((1, LANES), sink_ref[n] * LOG2E, jnp.float32)
                    m = jnp.maximum(m, sk)
                e = {r: jnp.exp2(b - m) for r, b in live.items()}
                if sinks:
                    sink_terms.append(jnp.exp2(sk - m))
                zero = jnp.zeros((LANES, LANES), jnp.float32)
                cols.append(jnp.concatenate([e.get(r, zero) for r in range(nsub)], axis=0))
            p = jnp.concatenate(cols, axis=1).astype(MXU_DTYPE)
            pv = _dot_tn(v[n // GROUP], p)
            den = pv[den_row:den_row + 1, :]
            if sinks:
                den = den + jnp.concatenate(sink_terms, axis=1)
            return pv / den

        jobs = [(t, n) for t in range(tiles) for n in range(nheads)]
        lookahead = 3
        pending = [_dot_nt(keys[t], heads[t][n]) for t, n in jobs[:lookahead]]
        outs = []
        for i, (t, n) in enumerate(jobs):
            if i + lookahead < len(jobs):
                t2, n2 = jobs[i + lookahead]
                pending.append(_dot_nt(keys[t2], heads[t2][n2]))
            outs.append(finish(t, n, pending[i]))
            if n == nheads - 1:
                done, outs = outs, []
                o_ref[0, t * tq:(t + 1) * tq, :] = _merge_heads(
                    [jnp.concatenate(done[h * GROUP:(h + 1) * GROUP], axis=1) for h in range(KV_HEADS)],
                    tq).astype(o_ref.dtype)

    @pl.when(q0 >= window)
    def _():
        attend([(pl.multiple_of(q0 + t * tq - window, LANES), -wsub) for t in range(tiles)])

    @pl.when(q0 < window)
    def _():
        attend([(t * tq - window, -wsub) if t * tq >= window else (0, -(t * tq // LANES))
                for t in range(tiles)])


def _band_attention(q, kv, k_col, v_col, window, sinks):
    B, S, _ = q.shape
    tq = min(BAND_Q, S)
    tiles = BAND_TILES
    assert tiles * tq >= window and S % (tiles * tq) == 0
    in_specs = [pl.BlockSpec((1, tiles * tq, Q_WIDTH), lambda b, i: (b, i, 0)),
                pl.BlockSpec((1, S, KV_WIDTH), lambda b, i: (b, 0, k_col)),
                pl.BlockSpec((1, S, KV_WIDTH), lambda b, i: (b, 0, v_col))]
    args = [q, kv, kv]
    if sinks is not None:
        in_specs = [pl.BlockSpec(memory_space=pltpu.SMEM)] + in_specs
        args = [sinks] + args
    return pl.pallas_call(
        functools.partial(_band_kernel, tq=tq, tiles=tiles, window=window, sinks=sinks is not None),
        grid=(B, S // (tiles * tq)),
        in_specs=in_specs,
        out_specs=pl.BlockSpec((1, tiles * tq, Q_WIDTH), lambda b, i: (b, i, 0)),
        out_shape=jax.ShapeDtypeStruct((B, S, Q_WIDTH), MXU_DTYPE),
        name="band_sink" if sinks is not None else "band",
        compiler_params=_params("parallel", "parallel"),
    )(*args)


def _mix_kernel(x_ref, oc_ref, os_ref, ow_ref, osw_ref, gl_ref, ge_ref, w_ref, gn_ref, gs_ref, gp_ref, h_ref):
    sig = 1.0 / (1.0 + jnp.exp(-gl_ref[0]))
    hi = sig.astype(MXU_DTYPE)
    lo = (sig - hi.astype(jnp.float32)).astype(MXU_DTYPE)
    gates = jnp.dot(jnp.concatenate([hi, lo], axis=1), ge_ref[...],
                    preferred_element_type=jnp.float32)
    f32 = jnp.float32
    o_nsa = (gates[:, 0:Q_WIDTH] * oc_ref[0].astype(f32) + gates[:, Q_WIDTH:2 * Q_WIDTH] * os_ref[0].astype(f32)
             + gates[:, 2 * Q_WIDTH:3 * Q_WIDTH] * ow_ref[0].astype(f32))
    cat = jnp.concatenate([_rms(o_nsa, gn_ref[...]), _rms(osw_ref[0].astype(f32), gs_ref[...])],
                          axis=1).astype(MXU_DTYPE)
    mixed = jnp.dot(cat, w_ref[...], preferred_element_type=jnp.float32)
    h_ref[0] = x_ref[0] + _rms(mixed, gp_ref[...])


def _mix(x, o_cmp, o_slc, o_win, o_swa, gl, gate_expand, w_out, g_nsa, g_swa, g_post):
    B, S, D = x.shape
    ts = min(MIX_ROWS, S)
    row = lambda width: pl.BlockSpec((1, ts, width), lambda b, i: (b, i, 0))
    full = lambda a: pl.BlockSpec(a.shape, lambda b, i: (0,) * a.ndim)
    return pl.pallas_call(
        _mix_kernel,
        grid=(B, S // ts),
        in_specs=[row(D), row(Q_WIDTH), row(Q_WIDTH), row(Q_WIDTH), row(Q_WIDTH), row(LANES),
                  full(gate_expand), full(w_out), full(g_nsa), full(g_swa), full(g_post)],
        out_specs=row(D),
        out_shape=jax.ShapeDtypeStruct((B, S, D), jnp.float32),
        name="mix",
        compiler_params=_params("parallel", "parallel"),
    )(x, o_cmp, o_slc, o_win, o_swa, gl, gate_expand, w_out, g_nsa, g_swa, g_post)


def _ffn_kernel(h_ref, halo_ref, gpre_ref, wup_ref, cw_ref, cb_ref, wdn_ref, gpost_ref, y_ref, act_scr,
                *, ts, d_ff, chunk):
    rows = ts // FFN_PARTS
    sub = lax.broadcasted_iota(jnp.int32, (SUBLANES, chunk), 0)
    live = (pl.program_id(1) > 0).astype(jnp.float32)
    halo_hn = _rms(halo_ref[0], gpre_ref[...]) * live
    lhs = []
    for part in range(FFN_PARTS):
        hn = _rms(h_ref[0, part * rows:(part + 1) * rows, :], gpre_ref[...])
        lhs.append(jnp.concatenate([halo_hn, pltpu.einshape("(sa)d->(as)d", hn, s=SUBLANES)],
                                   axis=0).astype(MXU_DTYPE))
        halo_hn = hn[rows - HALO:]

    def conv(hn, col):
        u = jnp.dot(hn, wup_ref[:, col:col + chunk], preferred_element_type=jnp.float32)
        halo, u = u[:HALO], u[HALO:]
        t1 = jnp.where(sub == 0, halo[HALO - 1:HALO], pltpu.roll(u[rows - SUBLANES:], 1, 0))
        t2 = jnp.where(sub == 0, halo[HALO - 2:HALO - 1], pltpu.roll(u[rows - 2 * SUBLANES:rows - SUBLANES], 1, 0))
        tap1 = jnp.concatenate([t1, u[:rows - SUBLANES]], axis=0)
        tap2 = jnp.concatenate([t2, t1, u[:rows - 2 * SUBLANES]], axis=0)
        w = cw_ref[:, col:col + chunk]
        return w[0:1] * tap2 + w[1:2] * tap1 + w[2:3] * u + cb_ref[:, col:col + chunk]

    for part in range(FFN_PARTS):
        for col in range(0, d_ff, chunk):
            act = _gelu_tanh(conv(lhs[part], col)) * conv(lhs[part], d_ff + col)
            act_scr[part * rows:(part + 1) * rows, col:col + chunk] = act.astype(MXU_DTYPE)
    ys = [jnp.dot(act_scr[part * rows:(part + 1) * rows, :], wdn_ref[...], preferred_element_type=jnp.float32)
          for part in range(FFN_PARTS)]
    for part in range(FFN_PARTS):
        y = pltpu.einshape("(as)d->(sa)d", ys[part], s=SUBLANES)
        y_ref[0, part * rows:(part + 1) * rows, :] = (h_ref[0, part * rows:(part + 1) * rows, :]
                                                      + _rms(y, gpost_ref[...]))


def _ffn(h, g_pre, w_up, conv_w, conv_b, w_down, g_post):
    B, S, D = h.shape
    ts = min(FFN_ROWS, S)
    d_ff = w_down.shape[0]
    full = lambda a: pl.BlockSpec(a.shape, lambda b, i: (0,) * a.ndim, pipeline_mode=pl.Buffered(1))
    return pl.pallas_call(
        functools.partial(_ffn_kernel, ts=ts, d_ff=d_ff, chunk=FFN_CHUNK),
        grid=(B, S // ts),
        in_specs=[pl.BlockSpec((1, ts, D), lambda b, i: (b, i, 0)),
                  pl.BlockSpec((1, HALO, D), lambda b, i: (b, jnp.maximum(i * (ts // HALO) - 1, 0), 0)),
                  full(g_pre), full(w_up), full(conv_w), full(conv_b), full(w_down), full(g_post)],
        out_specs=pl.BlockSpec((1, ts, D), lambda b, i: (b, i, 0)),
        out_shape=jax.ShapeDtypeStruct((B, S, D), jnp.float32),
        scratch_shapes=[pltpu.VMEM((ts, d_ff), MXU_DTYPE)],
        name="ffn",
        compiler_params=_params("parallel", "parallel"),
    )(h, h, g_pre, w_up, conv_w, conv_b, w_down, g_post)


def _rope_tables(S):
    half = HEAD_DIM // 2
    inv = ROPE_THETA ** (-jnp.arange(half, dtype=jnp.float32) / half)
    ang = jnp.arange(S).astype(jnp.float32)[:, None] * inv[None, :]
    cos, sin = jnp.cos(ang), jnp.sin(ang)
    reps = LANES // HEAD_DIM
    return jnp.tile(jnp.concatenate([cos, cos], axis=1), (1, reps)), jnp.tile(jnp.concatenate([-sin, sin], axis=1), (1, reps))


def _layer(x, w_in, w_out, attn_pre_norm, attn_post_norm, nsa_out_norm, swa_out_norm, cmp_pos, cmp_w1, cmp_w2,
           swa_sinks, ffn_pre_norm, ffn_post_norm, w_up, conv_w, conv_b, w_down):
    B, S, D = x.shape
    f32 = jnp.float32
    perm = _q_perm()
    n_gate = KV_HEADS * GROUP * NSA_BRANCHES
    sizes = [Q_WIDTH] + [KV_WIDTH] * 6 + [n_gate] + [Q_WIDTH, KV_WIDTH, KV_WIDTH]
    q_n, k_c, v_c, k_s, v_s, k_w, v_w, g_n, q_w, k_sw, v_sw = np.cumsum([0] + sizes)[:-1]
    slab = np.arange(KV_WIDTH)
    cols = np.concatenate([q_n + perm, q_w + perm] + [o + slab for o in (k_c, k_s, k_w, k_sw, v_c, v_s, v_w, v_sw)]
                          + [g_n + np.arange(n_gate), np.zeros(LANES - n_gate, np.int64)])
    live = np.arange(cols.size) < cols.size - (LANES - n_gate)
    w_all = jnp.where(live, jnp.take(w_in, jnp.asarray(cols, jnp.int32), axis=1), 0.0).astype(MXU_DTYPE)
    cos, sin = _rope_tables(S)
    row = lambda g: g.reshape(1, -1).astype(f32)

    qn, qw, kc, vc, kv, gl = _proj(x, row(attn_pre_norm), w_all, cos, sin)

    pe = jnp.broadcast_to(cmp_pos.reshape(2, 2, CMP_STRIDE, 1, HEAD_DIM), (2, 2, CMP_STRIDE, KV_HEADS, HEAD_DIM))
    pe = pe.reshape(2, 2, 1, CMP_STRIDE * KV_WIDTH)
    w1 = cmp_w1.astype(MXU_DTYPE).reshape(2, 2, CMP_STRIDE, HEAD_DIM, CMP_HIDDEN)
    w1x = jnp.stack([jnp.concatenate([w1, jnp.zeros_like(w1)], axis=-1),
                     jnp.concatenate([jnp.zeros_like(w1), w1], axis=-1)], axis=3)
    w1x = w1x.reshape(2, 2, CMP_STRIDE * KV_WIDTH, KV_HEADS * CMP_HIDDEN)
    w2 = cmp_w2.astype(MXU_DTYPE)
    w2x = jnp.concatenate([jnp.concatenate([w2, jnp.zeros_like(w2)], axis=-1),
                           jnp.concatenate([jnp.zeros_like(w2), w2], axis=-1)], axis=1)
    ck, cv = _compress(kc, vc, pe, w1x, w2x)

    o_cmp, sel = _cmp_attention(qn, ck, cv)
    o_slc = _slc_attention(qn, kv, sel)
    o_win = _band_attention(qn, kv, 1, 4, NSA_WINDOW, None)
    o_swa = _band_attention(qw, kv, 2, 5, SWA_WINDOW, swa_sinks.astype(f32))

    n = np.arange(NSA_BRANCHES * Q_WIDTH)
    br, g, h = n // Q_WIDTH, (n % Q_WIDTH) // LANES, (n % LANES) // HEAD_DIM
    expand = np.zeros((LANES, NSA_BRANCHES * Q_WIDTH), np.float32)
    expand[(h * GROUP + g) * NSA_BRANCHES + br, n] = 1.0
    expand = np.concatenate([expand, expand], axis=0)
    w_out_p = jnp.take(w_out, jnp.asarray(np.concatenate([perm, Q_WIDTH + perm]), jnp.int32),
                       axis=0).astype(MXU_DTYPE)
    hmid = _mix(x, o_cmp, o_slc, o_win, o_swa, gl, jnp.asarray(expand, MXU_DTYPE), w_out_p,
                row(nsa_out_norm[perm]), row(swa_out_norm[perm]), row(attn_post_norm))

    return _ffn(hmid, row(ffn_pre_norm), w_up.astype(MXU_DTYPE), conv_w.astype(f32), row(conv_b),
                w_down.astype(MXU_DTYPE), row(ffn_post_norm))


def kernel(x, w_in, w_out, attn_pre_norm, attn_post_norm, nsa_out_norm, swa_out_norm, cmp_pos, cmp_w1, cmp_w2,
           swa_sinks, ffn_pre_norm, ffn_post_norm, w_up, conv_w, conv_b, w_down):
    h = x
    for l in range(w_in.shape[0]):
        h = _layer(h, w_in[l], w_out[l], attn_pre_norm[l], attn_post_norm[l], nsa_out_norm[l], swa_out_norm[l],
                   cmp_pos[l], cmp_w1[l], cmp_w2[l], swa_sinks[l], ffn_pre_norm[l], ffn_post_norm[l],
                   w_up[l], conv_w[l], conv_b[l], w_down[l])
    return h
```

```python
import functools

import numpy as np
import jax
import jax.numpy as jnp
from jax import lax
from jax.experimental import pallas as pl
from jax.experimental.pallas import tpu as pltpu

HEAD_DIM = 64
KV_HEADS = 2
GROUP = 4
Q_WIDTH = KV_HEADS * GROUP * HEAD_DIM
KV_WIDTH = KV_HEADS * HEAD_DIM
SWA_WINDOW = 128
NSA_WINDOW = 512
CMP_LEN = 32
CMP_STRIDE = 16
CMP_HIDDEN = 256
SLC_BLOCK = 64
SLC_TOPK = 16
NSA_BRANCHES = 3
D_FF = 2816
ROPE_THETA = 10000.0
RMS_EPS = 1e-6
NEG = -1e30
BIG = 1e9
LOG2E = float(np.log2(np.e))

LANES = 128
SUBLANES = 8
HALO = 8
MXU_DTYPE = jnp.bfloat16
VMEM_LIMIT = 56 * 1024 * 1024

PROJ_ROWS = 512
CMP_Q = 512
SLC_Q = 256
SLC_UNROLL = 4
BAND_Q = 256
BAND_TILES = 4
MIX_ROWS = 512
FFN_ROWS = 1024
FFN_PARTS = 2
FFN_CHUNK = 256
RANK_STEP = 16

_N_ROPE_SLABS = 12
_N_QSLABS = 8
PROJ_WIDTH = 2 * Q_WIDTH + 9 * LANES


def _q_perm():
    n = np.arange(Q_WIDTH)
    g, h, d = n // 128, (n % 128) // 64, n % 64
    return (h * GROUP + g) * HEAD_DIM + d


def _params(*sem):
    return pltpu.CompilerParams(dimension_semantics=sem, vmem_limit_bytes=VMEM_LIMIT)


def _rms(xf, g):
    return xf * lax.rsqrt(jnp.mean(xf * xf, axis=-1, keepdims=True) + RMS_EPS) * g


def _gelu_tanh(x):
    return 0.5 * x * (1.0 + jnp.tanh(np.sqrt(2.0 / np.pi).astype(np.float32) * (x + 0.044715 * (x * x * x))))


def _dot_nt(a, b):
    return lax.dot_general(a, b, (((1,), (1,)), ((), ())), preferred_element_type=jnp.float32)


def _dot_tn(a, b):
    return lax.dot_general(a, b, (((0,), (0,)), ((), ())), preferred_element_type=jnp.float32)


def _split_heads(q):
    qs = jnp.concatenate([q[:, g * LANES:(g + 1) * LANES] for g in range(GROUP)], axis=0)
    lane = lax.broadcasted_iota(jnp.int32, qs.shape, 1)
    zero = jnp.zeros_like(qs)
    return [jnp.where((lane >= h * HEAD_DIM) & (lane < (h + 1) * HEAD_DIM), qs, zero) for h in range(KV_HEADS)]


def _values_with_ones(v):
    lane = lax.broadcasted_iota(jnp.int32, v.shape, 1)
    one = jnp.ones_like(v)
    return [jnp.where((lane >= h * HEAD_DIM) & (lane < (h + 1) * HEAD_DIM), v, one) for h in range(KV_HEADS)]


def _merge_heads(o_t, tq):
    sub = lax.broadcasted_iota(jnp.int32, o_t[0].shape, 0)
    both = jnp.where(sub < HEAD_DIM, o_t[0], o_t[1])
    return jnp.concatenate([both[:, g * tq:(g + 1) * tq].T for g in range(GROUP)], axis=1)


def _proj_kernel(x_ref, g_ref, w_ref, cos_ref, sin_ref, qn_ref, qw_ref, kc_ref, vc_ref, kv_ref, gl_ref):
    hn = _rms(x_ref[0], g_ref[...]).astype(MXU_DTYPE)
    p = jnp.dot(hn, w_ref[...], preferred_element_type=jnp.float32)
    cos, sin = cos_ref[...], sin_ref[...]
    lane = lax.broadcasted_iota(jnp.int32, cos.shape, 1)
    first_half = (lane % HEAD_DIM) < (HEAD_DIM // 2)

    def slab(j, rope):
        z = p[:, j * LANES:(j + 1) * LANES]
        if rope:
            swapped = jnp.where(first_half, pltpu.roll(z, LANES - HEAD_DIM // 2, 1), pltpu.roll(z, HEAD_DIM // 2, 1))
            z = z * cos + swapped * sin
        return z

    scale = HEAD_DIM ** -0.5 * LOG2E
    for j in range(4):
        qn_ref[0, :, j * LANES:(j + 1) * LANES] = (slab(j, True) * scale).astype(qn_ref.dtype)
        qw_ref[0, :, j * LANES:(j + 1) * LANES] = (slab(4 + j, True) * scale).astype(qw_ref.dtype)
    kc_ref[0] = slab(8, True)
    vc_ref[0] = slab(12, False)
    for j in range(3):
        kv_ref[0, :, j * LANES:(j + 1) * LANES] = slab(9 + j, True).astype(kv_ref.dtype)
    for j in range(3):
        kv_ref[0, :, (3 + j) * LANES:(4 + j) * LANES] = slab(13 + j, False).astype(kv_ref.dtype)
    gl_ref[0] = slab(16, False)


def _proj(x, gain, w, cos, sin):
    B, S, D = x.shape
    ts = min(PROJ_ROWS, S)
    row = lambda width: pl.BlockSpec((1, ts, width), lambda b, i: (b, i, 0))
    full = lambda a: pl.BlockSpec(a.shape, lambda b, i: (0,) * a.ndim)
    tab = pl.BlockSpec((ts, LANES), lambda b, i: (i, 0))
    f32, bf = jnp.float32, MXU_DTYPE
    return pl.pallas_call(
        _proj_kernel,
        grid=(B, S // ts),
        in_specs=[row(D), full(gain), full(w), tab, tab],
        out_specs=[row(Q_WIDTH), row(Q_WIDTH), row(LANES), row(LANES), row(6 * LANES), row(LANES)],
        out_shape=[jax.ShapeDtypeStruct((B, S, Q_WIDTH), bf), jax.ShapeDtypeStruct((B, S, Q_WIDTH), bf),
                   jax.ShapeDtypeStruct((B, S, LANES), f32), jax.ShapeDtypeStruct((B, S, LANES), f32),
                   jax.ShapeDtypeStruct((B, S, 6 * LANES), bf), jax.ShapeDtypeStruct((B, S, LANES), f32)],
        name="proj",
        compiler_params=_params("parallel", "parallel"),
    )(x, gain, w, cos, sin)


def _compress_kernel(kc_ref, vc_ref, pe_ref, w1_ref, w2_ref, ck_ref, cv_ref):
    n = kc_ref.shape[1] // CMP_STRIDE
    for kv, (src, dst) in enumerate(((kc_ref, ck_ref), (vc_ref, cv_ref))):
        ch = jnp.concatenate([src[0, pl.ds(p, n, stride=CMP_STRIDE), :] for p in range(CMP_STRIDE)], axis=1)
        top = jnp.dot((ch + pe_ref[kv, 0]).astype(MXU_DTYPE), w1_ref[kv, 0], preferred_element_type=jnp.float32)
        bot = jnp.dot((ch + pe_ref[kv, 1]).astype(MXU_DTYPE), w1_ref[kv, 1], preferred_element_type=jnp.float32)
        hid = top + pltpu.roll(bot, n - 1, 0)
        act = _gelu_tanh(hid).astype(MXU_DTYPE)
        dst[0] = jnp.dot(act, w2_ref[kv], preferred_element_type=jnp.float32).astype(dst.dtype)


def _compress(kc, vc, pe, w1, w2):
    B, S, width = kc.shape
    n = S // CMP_STRIDE
    src = pl.BlockSpec((1, S, width), lambda b: (b, 0, 0))
    full = lambda a: pl.BlockSpec(a.shape, lambda b: (0,) * a.ndim)
    dst = pl.BlockSpec((1, n, KV_WIDTH), lambda b: (b, 0, 0))
    return pl.pallas_call(
        _compress_kernel,
        grid=(B,),
        in_specs=[src, src, full(pe), full(w1), full(w2)],
        out_specs=[dst, dst],
        out_shape=[jax.ShapeDtypeStruct((B, n, KV_WIDTH), MXU_DTYPE)] * 2,
        name="compress",
        compiler_params=_params("parallel"),
    )(kc, vc, pe, w1, w2)


def _select_bias(score, *, live):
    tq = score.shape[1]
    groups = [score[SUBLANES * k:SUBLANES * (k + 1)] for k in range(live // SUBLANES)]
    rank = [jnp.zeros((SUBLANES, tq), jnp.int32) for _ in groups]
    sub = lax.broadcasted_iota(jnp.int32, (SUBLANES, tq), 0)
    for i in range(live):
        row = score[i:i + 1, :]
        for k, grp in enumerate(groups):
            if SUBLANES * k > i:
                beats = jnp.where(row >= grp, 1, 0)
            elif SUBLANES * k + SUBLANES - 1 <= i:
                beats = jnp.where(row > grp, 1, 0)
            else:
                beats = jnp.where(sub > i - SUBLANES * k, jnp.where(row >= grp, 1, 0), jnp.where(row > grp, 1, 0))
            rank[k] = rank[k] + beats
    return jnp.where(jnp.concatenate(rank, axis=0) < min(SLC_TOPK, live), 0.0, NEG)


def _cmp_kernel(q_ref, ck_ref, cv_ref, o_ref, sel_ref, *, tq, nsb):
    q0 = pl.program_id(1) * tq
    ncp = ck_ref.shape[1]
    nheads = KV_HEADS * GROUP
    heads = [qs[g * tq:(g + 1) * tq] for qs in _split_heads(q_ref[0]) for g in range(GROUP)]
    has_valid = (q0 + lax.broadcasted_iota(jnp.int32, (1, tq), 1) >= CMP_LEN - 1).astype(jnp.float32)
    ratio = SLC_BLOCK // CMP_STRIDE

    def attend(live):
        ncl = min(live * ratio, ncp)
        ck = ck_ref[0, 0:ncl, :]
        cv = cv_ref[0, 0:ncl, :]
        c = lax.broadcasted_iota(jnp.int32, (ncl, tq), 0)
        t = q0 + lax.broadcasted_iota(jnp.int32, (ncl, tq), 1)
        valid = c * CMP_STRIDE + (CMP_LEN - 1) <= t
        jo = lax.broadcasted_iota(jnp.int32, (live, ncl), 0)
        co = lax.broadcasted_iota(jnp.int32, (live, ncl), 1)
        overlap = ((co * CMP_STRIDE <= jo * SLC_BLOCK + SLC_BLOCK - 1)
                   & (co * CMP_STRIDE + CMP_LEN - 1 >= jo * SLC_BLOCK) & (co < ncp - 1)).astype(MXU_DTYPE)
        jb = lax.broadcasted_iota(jnp.int32, (live, tq), 0)
        tb = q0 + lax.broadcasted_iota(jnp.int32, (live, tq), 1)
        forced = (jb == 0) | (jb == tb // SLC_BLOCK) | (jb == tb // SLC_BLOCK - 1)
        causal = jb * SLC_BLOCK <= tb

        def finish(s):
            s = jnp.where(valid, s, NEG)
            e = jnp.exp2(s - jnp.max(s, axis=0, keepdims=True))
            r = has_valid / jnp.maximum(jnp.sum(e, axis=0, keepdims=True), 1e-30)
            return _dot_tn(cv, e.astype(MXU_DTYPE)) * r, e * r

        lookahead = 4 if ncl <= LANES else 3
        pending = [_dot_nt(ck, heads[n]) for n in range(lookahead)]
        outs, probs = [], []
        for n in range(nheads):
            if n + lookahead < nheads:
                pending.append(_dot_nt(ck, heads[n + lookahead]))
            o, p = finish(pending[n])
            outs.append(o)
            probs.append(p)
        sel_t = []
        for h in range(KV_HEADS):
            pg = functools.reduce(jnp.add, probs[h * GROUP:(h + 1) * GROUP])
            hi = pg.astype(MXU_DTYPE)
            lo = (pg - hi.astype(jnp.float32)).astype(MXU_DTYPE)
            imp = (jnp.dot(overlap, hi, preferred_element_type=jnp.float32)
                   + jnp.dot(overlap, lo, preferred_element_type=jnp.float32))
            bias = _select_bias(jnp.where(forced, BIG, jnp.where(causal, imp, NEG)), live=live)
            sel_t += [bias] + ([jnp.full((nsb - live, tq), NEG, jnp.float32)] if live < nsb else [])
        o_ref[0] = _merge_heads([jnp.concatenate(outs[h * GROUP:(h + 1) * GROUP], axis=1)
                                 for h in range(KV_HEADS)], tq).astype(o_ref.dtype)
        pad = [jnp.zeros((LANES - KV_HEADS * nsb, tq), jnp.float32)] if KV_HEADS * nsb < LANES else []
        sel_ref[0] = jnp.concatenate(sel_t + pad, axis=0).T.astype(sel_ref.dtype)

    extents = list(range(RANK_STEP, nsb + 1, RANK_STEP))
    which = jnp.minimum((q0 + tq - 1) // (RANK_STEP * SLC_BLOCK), len(extents) - 1)
    for v, live in enumerate(extents):
        pl.when(which == v)(functools.partial(attend, live))


def _cmp_attention(qn, ck, cv):
    B, S, _ = qn.shape
    tq = min(CMP_Q, S)
    nsb = S // SLC_BLOCK
    ncp = ck.shape[1]
    return pl.pallas_call(
        functools.partial(_cmp_kernel, tq=tq, nsb=nsb),
        grid=(B, S // tq),
        in_specs=[pl.BlockSpec((1, tq, Q_WIDTH), lambda b, i: (b, i, 0)),
                  pl.BlockSpec((1, ncp, KV_WIDTH), lambda b, i: (b, 0, 0)),
                  pl.BlockSpec((1, ncp, KV_WIDTH), lambda b, i: (b, 0, 0))],
        out_specs=[pl.BlockSpec((1, tq, Q_WIDTH), lambda b, i: (b, i, 0)),
                   pl.BlockSpec((1, tq, LANES), lambda b, i: (b, i, 0))],
        out_shape=[jax.ShapeDtypeStruct((B, S, Q_WIDTH), MXU_DTYPE),
                   jax.ShapeDtypeStruct((B, S, LANES), MXU_DTYPE)],
        name="cmp_select",
        compiler_params=_params("parallel", "parallel"),
    )(qn, ck, cv)


def _slc_kernel(q_ref, k_ref, v_ref, sel_ref, qn_ref, kn_ref, seln_ref, o_ref, *scratch, tq, nsb):
    nheads = KV_HEADS * GROUP
    m_scr, acc_scr, s_scr = (scratch[i * nheads:(i + 1) * nheads] for i in range(3))
    qi = pl.program_id(1)
    erow = lax.broadcasted_iota(jnp.int32, (tq, LANES), 0) // SLC_BLOCK
    ecol = lax.broadcasted_iota(jnp.int32, (tq, LANES), 1)
    ecol = jnp.where(ecol < KV_HEADS * nsb, ecol % nsb, -1)

    def queries(q, bias):
        col = lax.broadcasted_iota(jnp.int32, bias.shape, 1)
        zero = jnp.zeros_like(bias)
        out = []
        for h, qs in enumerate(_split_heads(q)):
            bias_h = jnp.where((col >= h * nsb) & (col < (h + 1) * nsb), bias, zero)
            out += [jnp.concatenate([qs[g * tq:(g + 1) * tq], bias_h], axis=1) for g in range(GROUP)]
        return out

    def keys(k, kt):
        onehot = jnp.where(ecol == erow + kt * (tq // SLC_BLOCK), 1.0, 0.0).astype(MXU_DTYPE)
        return jnp.concatenate([k, onehot], axis=1)

    def consume(n, s, v):
        m_prev = m_scr[n][...]
        m_new = jnp.maximum(m_prev, jnp.max(s, axis=0, keepdims=True))
        alpha = jnp.exp2(m_prev - m_new)
        p = jnp.exp2(s - m_new)
        acc_scr[n][...] = alpha * acc_scr[n][...] + _dot_tn(v[n // GROUP], p.astype(MXU_DTYPE))
        m_scr[n][...] = m_new

    def step(kt, q_next, k_next, mask):
        v = _values_with_ones(v_ref[0, pl.ds(pl.multiple_of(kt * tq, tq), tq), :])
        s_cur = s_scr[0][...]
        s_scr[0][...] = _dot_nt(k_next, q_next[0])
        for n in range(nheads):
            s = s_cur
            if n + 1 < nheads:
                s_cur = s_scr[n + 1][...]
                s_scr[n + 1][...] = _dot_nt(k_next, q_next[n + 1])
            consume(n, s if mask is None else jnp.where(mask, s, NEG), v)

    q_aug = queries(q_ref[0], sel_ref[0])
    for n in range(nheads):
        m_scr[n][...] = jnp.full(m_scr[n].shape, NEG, jnp.float32)
        acc_scr[n][...] = jnp.zeros(acc_scr[n].shape, jnp.float32)

    @pl.when((pl.program_id(0) == 0) & (qi == 0))
    def _():
        k_first = keys(k_ref[0, pl.ds(0, tq), :], 0)
        for n in range(nheads):
            s_scr[n][...] = _dot_nt(k_first, q_aug[n])

    def inner(kt):
        step(kt, q_aug, keys(k_ref[0, pl.ds(pl.multiple_of((kt + 1) * tq, tq), tq), :], kt + 1), None)

    def trip(i, carry):
        for j in range(SLC_UNROLL):
            inner(SLC_UNROLL * i + j)
        return carry

    lax.fori_loop(0, qi // SLC_UNROLL, trip, 0)
    for j in range(1, SLC_UNROLL):
        pl.when(qi % SLC_UNROLL >= j)(functools.partial(lambda j: inner(qi - qi % SLC_UNROLL + j - 1), j))

    causal = lax.broadcasted_iota(jnp.int32, (tq, tq), 0) <= lax.broadcasted_iota(jnp.int32, (tq, tq), 1)
    step(qi, queries(qn_ref[0], seln_ref[0]), keys(kn_ref[0], 0), causal)
    o_t = []
    for h in range(KV_HEADS):
        den_row = (1 - h) * HEAD_DIM
        o_t.append(jnp.concatenate([acc_scr[n][...] / acc_scr[n][den_row:den_row + 1, :]
                                    for n in range(h * GROUP, (h + 1) * GROUP)], axis=1))
    o_ref[0] = _merge_heads(o_t, tq).astype(o_ref.dtype)


def _slc_attention(qn, kv, sel):
    B, S, _ = qn.shape
    tq = min(SLC_Q, S)
    nq = S // tq
    nsb = S // SLC_BLOCK

    def following(b, i):
        wrap = i + 1 == nq
        return jnp.where(wrap, jnp.minimum(b + 1, B - 1), b), jnp.where(wrap, 0, i + 1)

    return pl.pallas_call(
        functools.partial(_slc_kernel, tq=tq, nsb=nsb),
        grid=(B, nq),
        in_specs=[pl.BlockSpec((1, tq, Q_WIDTH), lambda b, i: (b, i, 0)),
                  pl.BlockSpec((1, S, KV_WIDTH), lambda b, i: (b, 0, 0)),
                  pl.BlockSpec((1, S, KV_WIDTH), lambda b, i: (b, 0, 3)),
                  pl.BlockSpec((1, tq, LANES), lambda b, i: (b, i, 0)),
                  pl.BlockSpec((1, tq, Q_WIDTH), lambda b, i: (*following(b, i), 0)),
                  pl.BlockSpec((1, tq, KV_WIDTH), lambda b, i: (following(b, i)[0], 0, 0)),
                  pl.BlockSpec((1, tq, LANES), lambda b, i: (*following(b, i), 0))],
        out_specs=pl.BlockSpec((1, tq, Q_WIDTH), lambda b, i: (b, i, 0)),
        out_shape=jax.ShapeDtypeStruct((B, S, Q_WIDTH), MXU_DTYPE),
        scratch_shapes=([pltpu.VMEM((1, tq), jnp.float32)] * (KV_HEADS * GROUP)
                        + [pltpu.VMEM((KV_WIDTH, tq), jnp.float32)] * (KV_HEADS * GROUP)
                        + [pltpu.VMEM((tq, tq), jnp.float32)] * (KV_HEADS * GROUP)),
        name="slc",
        compiler_params=_params("arbitrary", "arbitrary"),
    )(qn, kv, kv, sel, qn, kv, sel)


def _band_kernel(*refs, tq, tiles, window, sinks):
    if sinks:
        sink_ref, q_ref, k_ref, v_ref, o_ref = refs
    else:
        q_ref, k_ref, v_ref, o_ref = refs
    q0 = pl.program_id(1) * (tiles * tq)
    span = window + tq
    nsub, nhalf, wsub = span // LANES, tq // LANES, window // LANES
    nheads = KV_HEADS * GROUP
    kl = lax.broadcasted_iota(jnp.int32, (LANES, LANES), 0)
    ql = lax.broadcasted_iota(jnp.int32, (LANES, LANES), 1)

    def band_mask(delta):
        if delta >= 1 or delta + wsub <= -1:
            return False
        if delta == 0:
            return kl <= ql
        return kl > ql if delta + wsub == 0 else None

    def attend(spans):
        heads = [[qs[g * tq:(g + 1) * tq] for qs in _split_heads(q_ref[0, t * tq:(t + 1) * tq, :])
                  for g in range(GROUP)] for t in range(tiles)]
        keys = [k_ref[0, pl.ds(start, span), :] for start, _ in spans]
        values = [_values_with_ones(v_ref[0, pl.ds(start, span), :]) for start, _ in spans]

        def finish(t, n, s):
            shift, v = spans[t][1], values[t]
            cols, sink_terms = [], []
            den_row = (1 - n // GROUP) * HEAD_DIM
            for c in range(nhalf):
                live = {}
                for r in range(nsub):
                    mask = band_mask(r + shift - c)
                    if mask is False:
                        continue
                    blk = s[r * LANES:(r + 1) * LANES, c * LANES:(c + 1) * LANES]
                    live[r] = blk if mask is None else jnp.where(mask, blk, NEG)
                m = functools.reduce(jnp.maximum, [jnp.max(b, axis=0, keepdims=True) for b in live.values()])
                if sinks:
                    sk = jnp.full((1, LANES), sink_ref[n] * LOG2E, jnp.float32)
                    m = jnp.maximum(m, sk)
                e = {r: jnp.exp2(b - m) for r, b in live.items()}
                if sinks:
                    sink_terms.append(jnp.exp2(sk - m))
                zero = jnp.zeros((LANES, LANES), jnp.float32)
                cols.append(jnp.concatenate([e.get(r, zero) for r in range(nsub)], axis=0))
            p = jnp.concatenate(cols, axis=1).astype(MXU_DTYPE)
            pv = _dot_tn(v[n // GROUP], p)
            den = pv[den_row:den_row + 1, :]
            if sinks:
                den = den + jnp.concatenate(sink_terms, axis=1)
            return pv / den

        jobs = [(t, n) for t in range(tiles) for n in range(nheads)]
        lookahead = 3
        pending = [_dot_nt(keys[t], heads[t][n]) for t, n in jobs[:lookahead]]
        outs = []
        for i, (t, n) in enumerate(jobs):
            if i + lookahead < len(jobs):
                t2, n2 = jobs[i + lookahead]
                pending.append(_dot_nt(keys[t2], heads[t2][n2]))
            outs.append(finish(t, n, pending[i]))
            if n == nheads - 1:
                done, outs = outs, []
                o_ref[0, t * tq:(t + 1) * tq, :] = _merge_heads(
                    [jnp.concatenate(done[h * GROUP:(h + 1) * GROUP], axis=1) for h in range(KV_HEADS)],
                    tq).astype(o_ref.dtype)

    @pl.when(q0 >= window)
    def _():
        attend([(pl.multiple_of(q0 + t * tq - window, LANES), -wsub) for t in range(tiles)])

    @pl.when(q0 < window)
    def _():
        attend([(t * tq - window, -wsub) if t * tq >= window else (0, -(t * tq // LANES))
                for t in range(tiles)])


def _band_attention(q, kv, k_col, v_col, window, sinks):
    B, S, _ = q.shape
    tq = min(BAND_Q, S)
    tiles = BAND_TILES
    assert tiles * tq >= window and S % (tiles * tq) == 0
    in_specs = [pl.BlockSpec((1, tiles * tq, Q_WIDTH), lambda b, i: (b, i, 0)),
                pl.BlockSpec((1, S, KV_WIDTH), lambda b, i: (b, 0, k_col)),
                pl.BlockSpec((1, S, KV_WIDTH), lambda b, i: (b, 0, v_col))]
    args = [q, kv, kv]
    if sinks is not None:
        in_specs = [pl.BlockSpec(memory_space=pltpu.SMEM)] + in_specs
        args = [sinks] + args
    return pl.pallas_call(
        functools.partial(_band_kernel, tq=tq, tiles=tiles, window=window, sinks=sinks is not None),
        grid=(B, S // (tiles * tq)),
        in_specs=in_specs,
        out_specs=pl.BlockSpec((1, tiles * tq, Q_WIDTH), lambda b, i: (b, i, 0)),
        out_shape=jax.ShapeDtypeStruct((B, S, Q_WIDTH), MXU_DTYPE),
        name="band_sink" if sinks is not None else "band",
        compiler_params=_params("parallel", "parallel"),
    )(*args)


def _mix_kernel(x_ref, oc_ref, os_ref, ow_ref, osw_ref, gl_ref, ge_ref, w_ref, gn_ref, gs_ref, gp_ref, h_ref):
    sig = 1.0 / (1.0 + jnp.exp(-gl_ref[0]))
    hi = sig.astype(MXU_DTYPE)
    lo = (sig - hi.astype(jnp.float32)).astype(MXU_DTYPE)
    gates = jnp.dot(jnp.concatenate([hi, lo], axis=1), ge_ref[...],
                    preferred_element_type=jnp.float32)
    f32 = jnp.float32
    o_nsa = (gates[:, 0:Q_WIDTH] * oc_ref[0].astype(f32) + gates[:, Q_WIDTH:2 * Q_WIDTH] * os_ref[0].astype(f32)
             + gates[:, 2 * Q_WIDTH:3 * Q_WIDTH] * ow_ref[0].astype(f32))
    cat = jnp.concatenate([_rms(o_nsa, gn_ref[...]), _rms(osw_ref[0].astype(f32), gs_ref[...])],
                          axis=1).astype(MXU_DTYPE)
    mixed = jnp.dot(cat, w_ref[...], preferred_element_type=jnp.float32)
    h_ref[0] = x_ref[0] + _rms(mixed, gp_ref[...])


def _mix(x, o_cmp, o_slc, o_win, o_swa, gl, gate_expand, w_out, g_nsa, g_swa, g_post):
    B, S, D = x.shape
    ts = min(MIX_ROWS, S)
    row = lambda width: pl.BlockSpec((1, ts, width), lambda b, i: (b, i, 0))
    full = lambda a: pl.BlockSpec(a.shape, lambda b, i: (0,) * a.ndim)
    return pl.pallas_call(
        _mix_kernel,
        grid=(B, S // ts),
        in_specs=[row(D), row(Q_WIDTH), row(Q_WIDTH), row(Q_WIDTH), row(Q_WIDTH), row(LANES),
                  full(gate_expand), full(w_out), full(g_nsa), full(g_swa), full(g_post)],
        out_specs=row(D),
        out_shape=jax.ShapeDtypeStruct((B, S, D), jnp.float32),
        name="mix",
        compiler_params=_params("parallel", "parallel"),
    )(x, o_cmp, o_slc, o_win, o_swa, gl, gate_expand, w_out, g_nsa, g_swa, g_post)


def _ffn_kernel(h_ref, halo_ref, gpre_ref, wup_ref, cw_ref, cb_ref, wdn_ref, gpost_ref, y_ref, act_scr,
                *, ts, d_ff, chunk):
    rows = ts // FFN_PARTS
    sub = lax.broadcasted_iota(jnp.int32, (SUBLANES, chunk), 0)
    live = (pl.program_id(1) > 0).astype(jnp.float32)
    halo_hn = _rms(halo_ref[0], gpre_ref[...]) * live
    lhs = []
    for part in range(FFN_PARTS):
        hn = _rms(h_ref[0, part * rows:(part + 1) * rows, :], gpre_ref[...])
        lhs.append(jnp.concatenate([halo_hn, pltpu.einshape("(sa)d->(as)d", hn, s=SUBLANES)],
                                   axis=0).astype(MXU_DTYPE))
        halo_hn = hn[rows - HALO:]

    def conv(hn, col):
        u = jnp.dot(hn, wup_ref[:, col:col + chunk], preferred_element_type=jnp.float32)
        halo, u = u[:HALO], u[HALO:]
        t1 = jnp.where(sub == 0, halo[HALO - 1:HALO], pltpu.roll(u[rows - SUBLANES:], 1, 0))
        t2 = jnp.where(sub == 0, halo[HALO - 2:HALO - 1], pltpu.roll(u[rows - 2 * SUBLANES:rows - SUBLANES], 1, 0))
        tap1 = jnp.concatenate([t1, u[:rows - SUBLANES]], axis=0)
        tap2 = jnp.concatenate([t2, t1, u[:rows - 2 * SUBLANES]], axis=0)
        w = cw_ref[:, col:col + chunk]
        return w[0:1] * tap2 + w[1:2] * tap1 + w[2:3] * u + cb_ref[:, col:col + chunk]

    for part in range(FFN_PARTS):
        for col in range(0, d_ff, chunk):
            act = _gelu_tanh(conv(lhs[part], col)) * conv(lhs[part], d_ff + col)
            act_scr[part * rows:(part + 1) * rows, col:col + chunk] = act.astype(MXU_DTYPE)
    ys = [jnp.dot(act_scr[part * rows:(part + 1) * rows, :], wdn_ref[...], preferred_element_type=jnp.float32)
          for part in range(FFN_PARTS)]
    for part in range(FFN_PARTS):
        y = pltpu.einshape("(as)d->(sa)d", ys[part], s=SUBLANES)
        y_ref[0, part * rows:(part + 1) * rows, :] = (h_ref[0, part * rows:(part + 1) * rows, :]
                                                      + _rms(y, gpost_ref[...]))


def _ffn(h, g_pre, w_up, conv_w, conv_b, w_down, g_post):
    B, S, D = h.shape
    ts = min(FFN_ROWS, S)
    d_ff = w_down.shape[0]
    full = lambda a: pl.BlockSpec(a.shape, lambda b, i: (0,) * a.ndim, pipeline_mode=pl.Buffered(1))
    return pl.pallas_call(
        functools.partial(_ffn_kernel, ts=ts, d_ff=d_ff, chunk=FFN_CHUNK),
        grid=(B, S // ts),
        in_specs=[pl.BlockSpec((1, ts, D), lambda b, i: (b, i, 0)),
                  pl.BlockSpec((1, HALO, D), lambda b, i: (b, jnp.maximum(i * (ts // HALO) - 1, 0), 0)),
                  full(g_pre), full(w_up), full(conv_w), full(conv_b), full(w_down), full(g_post)],
        out_specs=pl.BlockSpec((1, ts, D), lambda b, i: (b, i, 0)),
        out_shape=jax.ShapeDtypeStruct((B, S, D), jnp.float32),
        scratch_shapes=[pltpu.VMEM((ts, d_ff), MXU_DTYPE)],
        name="ffn",
        compiler_params=_params("parallel", "parallel"),
    )(h, h, g_pre, w_up, conv_w, conv_b, w_down, g_post)


def _rope_tables(S):
    half = HEAD_DIM // 2
    inv = ROPE_THETA ** (-jnp.arange(half, dtype=jnp.float32) / half)
    ang = jnp.arange(S).astype(jnp.float32)[:, None] * inv[None, :]
    cos, sin = jnp.cos(ang), jnp.sin(ang)
    reps = LANES // HEAD_DIM
    return jnp.tile(jnp.concatenate([cos, cos], axis=1), (1, reps)), jnp.tile(jnp.concatenate([-sin, sin], axis=1), (1, reps))


def _layer(x, w_in, w_out, attn_pre_norm, attn_post_norm, nsa_out_norm, swa_out_norm, cmp_pos, cmp_w1, cmp_w2,
           swa_sinks, ffn_pre_norm, ffn_post_norm, w_up, conv_w, conv_b, w_down):
    B, S, D = x.shape
    f32 = jnp.float32
    perm = _q_perm()
    n_gate = KV_HEADS * GROUP * NSA_BRANCHES
    sizes = [Q_WIDTH] + [KV_WIDTH] * 6 + [n_gate] + [Q_WIDTH, KV_WIDTH, KV_WIDTH]
    q_n, k_c, v_c, k_s, v_s, k_w, v_w, g_n, q_w, k_sw, v_sw = np.cumsum([0] + sizes)[:-1]
    slab = np.arange(KV_WIDTH)
    cols = np.concatenate([q_n + perm, q_w + perm] + [o + slab for o in (k_c, k_s, k_w, k_sw, v_c, v_s, v_w, v_sw)]
                          + [g_n + np.arange(n_gate), np.zeros(LANES - n_gate, np.int64)])
    live = np.arange(cols.size) < cols.size - (LANES - n_gate)
    w_all = jnp.where(live, jnp.take(w_in, jnp.asarray(cols, jnp.int32), axis=1), 0.0).astype(MXU_DTYPE)
    cos, sin = _rope_tables(S)
    row = lambda g: g.reshape(1, -1).astype(f32)

    qn, qw, kc, vc, kv, gl = _proj(x, row(attn_pre_norm), w_all, cos, sin)

    pe = jnp.broadcast_to(cmp_pos.reshape(2, 2, CMP_STRIDE, 1, HEAD_DIM), (2, 2, CMP_STRIDE, KV_HEADS, HEAD_DIM))
    pe = pe.reshape(2, 2, 1, CMP_STRIDE * KV_WIDTH)
    w1 = cmp_w1.astype(MXU_DTYPE).reshape(2, 2, CMP_STRIDE, HEAD_DIM, CMP_HIDDEN)
    w1x = jnp.stack([jnp.concatenate([w1, jnp.zeros_like(w1)], axis=-1),
                     jnp.concatenate([jnp.zeros_like(w1), w1], axis=-1)], axis=3)
    w1x = w1x.reshape(2, 2, CMP_STRIDE * KV_WIDTH, KV_HEADS * CMP_HIDDEN)
    w2 = cmp_w2.astype(MXU_DTYPE)
    w2x = jnp.concatenate([jnp.concatenate([w2, jnp.zeros_like(w2)], axis=-1),
                           jnp.concatenate([jnp.zeros_like(w2), w2], axis=-1)], axis=1)
    ck, cv = _compress(kc, vc, pe, w1x, w2x)

    o_cmp, sel = _cmp_attention(qn, ck, cv)
    o_slc = _slc_attention(qn, kv, sel)
    o_win = _band_attention(qn, kv, 1, 4, NSA_WINDOW, None)
    o_swa = _band_attention(qw, kv, 2, 5, SWA_WINDOW, swa_sinks.astype(f32))

    n = np.arange(NSA_BRANCHES * Q_WIDTH)
    br, g, h = n // Q_WIDTH, (n % Q_WIDTH) // LANES, (n % LANES) // HEAD_DIM
    expand = np.zeros((LANES, NSA_BRANCHES * Q_WIDTH), np.float32)
    expand[(h * GROUP + g) * NSA_BRANCHES + br, n] = 1.0
    expand = np.concatenate([expand, expand], axis=0)
    w_out_p = jnp.take(w_out, jnp.asarray(np.concatenate([perm, Q_WIDTH + perm]), jnp.int32),
                       axis=0).astype(MXU_DTYPE)
    hmid = _mix(x, o_cmp, o_slc, o_win, o_swa, gl, jnp.asarray(expand, MXU_DTYPE), w_out_p,
                row(nsa_out_norm[perm]), row(swa_out_norm[perm]), row(attn_post_norm))

    return _ffn(hmid, row(ffn_pre_norm), w_up.astype(MXU_DTYPE), conv_w.astype(f32), row(conv_b),
                w_down.astype(MXU_DTYPE), row(ffn_post_norm))


def kernel(x, w_in, w_out, attn_pre_norm, attn_post_norm, nsa_out_norm, swa_out_norm, cmp_pos, cmp_w1, cmp_w2,
           swa_sinks, ffn_pre_norm, ffn_post_norm, w_up, conv_w, conv_b, w_down):
    h = x
    for l in range(w_in.shape[0]):
        h = _layer(h, w_in[l], w_out[l], attn_pre_norm[l], attn_post_norm[l], nsa_out_norm[l], swa_out_norm[l],
                   cmp_pos[l], cmp_w1[l], cmp_w2[l], swa_sinks[l], ffn_pre_norm[l], ffn_post_norm[l],
                   w_up[l], conv_w[l], conv_b[l], w_down[l])
    return h
```

```python
import functools

import numpy as np
import jax
import jax.numpy as jnp
from jax import lax
from jax.experimental import pallas as pl
from jax.experimental.pallas import tpu as pltpu

HEAD_DIM = 64
KV_HEADS = 2
GROUP = 4
Q_WIDTH = KV_HEADS * GROUP * HEAD_DIM
KV_WIDTH = KV_HEADS * HEAD_DIM
SWA_WINDOW = 128
NSA_WINDOW = 512
CMP_LEN = 32
CMP_STRIDE = 16
CMP_HIDDEN = 256
SLC_BLOCK = 64
SLC_TOPK = 16
NSA_BRANCHES = 3
D_FF = 2816
ROPE_THETA = 10000.0
RMS_EPS = 1e-6
NEG = -1e30
BIG = 1e9
LOG2E = float(np.log2(np.e))

LANES = 128
SUBLANES = 8
HALO = 8
MXU_DTYPE = jnp.bfloat16
VMEM_LIMIT = 56 * 1024 * 1024

PROJ_ROWS = 1024
CMP_Q = 512
SLC_Q = 256
SLC_UNROLL = 4
BAND_Q = 256
BAND_TILES = 4
MIX_ROWS = 1024
FFN_ROWS = 1024
FFN_PARTS = 2
FFN_CHUNK = 256
RANK_STEP = 16

_N_ROPE_SLABS = 12
_N_QSLABS = 8
PROJ_WIDTH = 2 * Q_WIDTH + 9 * LANES


def _q_perm():
    n = np.arange(Q_WIDTH)
    g, h, d = n // 128, (n % 128) // 64, n % 64
    return (h * GROUP + g) * HEAD_DIM + d


def _params(*sem):
    return pltpu.CompilerParams(dimension_semantics=sem, vmem_limit_bytes=VMEM_LIMIT)


def _rms(xf, g):
    return xf * lax.rsqrt(jnp.mean(xf * xf, axis=-1, keepdims=True) + RMS_EPS) * g


def _gelu_tanh(x):
    return 0.5 * x * (1.0 + jnp.tanh(np.sqrt(2.0 / np.pi).astype(np.float32) * (x + 0.044715 * (x * x * x))))


def _dot_nt(a, b):
    return lax.dot_general(a, b, (((1,), (1,)), ((), ())), preferred_element_type=jnp.float32)


def _dot_tn(a, b):
    return lax.dot_general(a, b, (((0,), (0,)), ((), ())), preferred_element_type=jnp.float32)


def _split_heads(q):
    qs = jnp.concatenate([q[:, g * LANES:(g + 1) * LANES] for g in range(GROUP)], axis=0)
    lane = lax.broadcasted_iota(jnp.int32, qs.shape, 1)
    zero = jnp.zeros_like(qs)
    return [jnp.where((lane >= h * HEAD_DIM) & (lane < (h + 1) * HEAD_DIM), qs, zero) for h in range(KV_HEADS)]


def _values_with_ones(v):
    lane = lax.broadcasted_iota(jnp.int32, v.shape, 1)
    one = jnp.ones_like(v)
    return [jnp.where((lane >= h * HEAD_DIM) & (lane < (h + 1) * HEAD_DIM), v, one) for h in range(KV_HEADS)]


def _merge_heads(o_t, tq):
    sub = lax.broadcasted_iota(jnp.int32, o_t[0].shape, 0)
    both = jnp.where(sub < HEAD_DIM, o_t[0], o_t[1])
    return jnp.concatenate([both[:, g * tq:(g + 1) * tq].T for g in range(GROUP)], axis=1)


def _proj_kernel(x_ref, g_ref, w_ref, cos_ref, sin_ref, qn_ref, qw_ref, kc_ref, vc_ref, kv_ref, gl_ref):
    hn = _rms(x_ref[0], g_ref[...]).astype(MXU_DTYPE)
    p = jnp.dot(hn, w_ref[...], preferred_element_type=jnp.float32)
    cos, sin = cos_ref[...], sin_ref[...]
    lane = lax.broadcasted_iota(jnp.int32, cos.shape, 1)
    first_half = (lane % HEAD_DIM) < (HEAD_DIM // 2)

    def slab(j, rope):
        z = p[:, j * LANES:(j + 1) * LANES]
        if rope:
            swapped = jnp.where(first_half, pltpu.roll(z, LANES - HEAD_DIM // 2, 1), pltpu.roll(z, HEAD_DIM // 2, 1))
            z = z * cos + swapped * sin
        return z

    scale = HEAD_DIM ** -0.5 * LOG2E
    for j in range(4):
        qn_ref[0, :, j * LANES:(j + 1) * LANES] = (slab(j, True) * scale).astype(qn_ref.dtype)
        qw_ref[0, :, j * LANES:(j + 1) * LANES] = (slab(4 + j, True) * scale).astype(qw_ref.dtype)
    kc_ref[0] = slab(8, True)
    vc_ref[0] = slab(12, False)
    for j in range(3):
        kv_ref[0, :, j * LANES:(j + 1) * LANES] = slab(9 + j, True).astype(kv_ref.dtype)
    for j in range(3):
        kv_ref[0, :, (3 + j) * LANES:(4 + j) * LANES] = slab(13 + j, False).astype(kv_ref.dtype)
    gl_ref[0] = slab(16, False)


def _proj(x, gain, w, cos, sin):
    B, S, D = x.shape
    ts = min(PROJ_ROWS, S)
    row = lambda width: pl.BlockSpec((1, ts, width), lambda b, i: (b, i, 0))
    full = lambda a: pl.BlockSpec(a.shape, lambda b, i: (0,) * a.ndim)
    tab = pl.BlockSpec((ts, LANES), lambda b, i: (i, 0))
    f32, bf = jnp.float32, MXU_DTYPE
    return pl.pallas_call(
        _proj_kernel,
        grid=(B, S // ts),
        in_specs=[row(D), full(gain), full(w), tab, tab],
        out_specs=[row(Q_WIDTH), row(Q_WIDTH), row(LANES), row(LANES), row(6 * LANES), row(LANES)],
        out_shape=[jax.ShapeDtypeStruct((B, S, Q_WIDTH), bf), jax.ShapeDtypeStruct((B, S, Q_WIDTH), bf),
                   jax.ShapeDtypeStruct((B, S, LANES), f32), jax.ShapeDtypeStruct((B, S, LANES), f32),
                   jax.ShapeDtypeStruct((B, S, 6 * LANES), bf), jax.ShapeDtypeStruct((B, S, LANES), f32)],
        name="proj",
        compiler_params=_params("parallel", "parallel"),
    )(x, gain, w, cos, sin)


def _compress_kernel(kc_ref, vc_ref, pe_ref, w1_ref, w2_ref, ck_ref, cv_ref):
    n = kc_ref.shape[1] // CMP_STRIDE
    for kv, (src, dst) in enumerate(((kc_ref, ck_ref), (vc_ref, cv_ref))):
        ch = jnp.concatenate([src[0, pl.ds(p, n, stride=CMP_STRIDE), :] for p in range(CMP_STRIDE)], axis=1)
        top = jnp.dot((ch + pe_ref[kv, 0]).astype(MXU_DTYPE), w1_ref[kv, 0], preferred_element_type=jnp.float32)
        bot = jnp.dot((ch + pe_ref[kv, 1]).astype(MXU_DTYPE), w1_ref[kv, 1], preferred_element_type=jnp.float32)
        hid = top + pltpu.roll(bot, n - 1, 0)
        act = _gelu_tanh(hid).astype(MXU_DTYPE)
        dst[0] = jnp.dot(act, w2_ref[kv], preferred_element_type=jnp.float32).astype(dst.dtype)


def _compress(kc, vc, pe, w1, w2):
    B, S, width = kc.shape
    n = S // CMP_STRIDE
    src = pl.BlockSpec((1, S, width), lambda b: (b, 0, 0))
    full = lambda a: pl.BlockSpec(a.shape, lambda b: (0,) * a.ndim)
    dst = pl.BlockSpec((1, n, KV_WIDTH), lambda b: (b, 0, 0))
    return pl.pallas_call(
        _compress_kernel,
        grid=(B,),
        in_specs=[src, src, full(pe), full(w1), full(w2)],
        out_specs=[dst, dst],
        out_shape=[jax.ShapeDtypeStruct((B, n, KV_WIDTH), MXU_DTYPE)] * 2,
        name="compress",
        compiler_params=_params("parallel"),
    )(kc, vc, pe, w1, w2)


def _select_bias(score, *, live):
    tq = score.shape[1]
    groups = [score[SUBLANES * k:SUBLANES * (k + 1)] for k in range(live // SUBLANES)]
    rank = [jnp.zeros((SUBLANES, tq), jnp.int32) for _ in groups]
    sub = lax.broadcasted_iota(jnp.int32, (SUBLANES, tq), 0)
    for i in range(live):
        row = score[i:i + 1, :]
        for k, grp in enumerate(groups):
            if SUBLANES * k > i:
                beats = jnp.where(row >= grp, 1, 0)
            elif SUBLANES * k + SUBLANES - 1 <= i:
                beats = jnp.where(row > grp, 1, 0)
            else:
                beats = jnp.where(sub > i - SUBLANES * k, jnp.where(row >= grp, 1, 0), jnp.where(row > grp, 1, 0))
            rank[k] = rank[k] + beats
    return jnp.where(jnp.concatenate(rank, axis=0) < min(SLC_TOPK, live), 0.0, NEG)


def _cmp_kernel(q_ref, ck_ref, cv_ref, o_ref, sel_ref, *, tq, nsb):
    q0 = pl.program_id(1) * tq
    ncp = ck_ref.shape[1]
    nheads = KV_HEADS * GROUP
    heads = [qs[g * tq:(g + 1) * tq] for qs in _split_heads(q_ref[0]) for g in range(GROUP)]
    has_valid = (q0 + lax.broadcasted_iota(jnp.int32, (1, tq), 1) >= CMP_LEN - 1).astype(jnp.float32)
    ratio = SLC_BLOCK // CMP_STRIDE

    def attend(live):
        ncl = min(live * ratio, ncp)
        ck = ck_ref[0, 0:ncl, :]
        cv = cv_ref[0, 0:ncl, :]
        c = lax.broadcasted_iota(jnp.int32, (ncl, tq), 0)
        t = q0 + lax.broadcasted_iota(jnp.int32, (ncl, tq), 1)
        valid = c * CMP_STRIDE + (CMP_LEN - 1) <= t
        jo = lax.broadcasted_iota(jnp.int32, (live, ncl), 0)
        co = lax.broadcasted_iota(jnp.int32, (live, ncl), 1)
        overlap = ((co * CMP_STRIDE <= jo * SLC_BLOCK + SLC_BLOCK - 1)
                   & (co * CMP_STRIDE + CMP_LEN - 1 >= jo * SLC_BLOCK) & (co < ncp - 1)).astype(MXU_DTYPE)
        jb = lax.broadcasted_iota(jnp.int32, (live, tq), 0)
        tb = q0 + lax.broadcasted_iota(jnp.int32, (live, tq), 1)
        forced = (jb == 0) | (jb == tb // SLC_BLOCK) | (jb == tb // SLC_BLOCK - 1)
        causal = jb * SLC_BLOCK <= tb

        def finish(s):
            s = jnp.where(valid, s, NEG)
            e = jnp.exp2(s - jnp.max(s, axis=0, keepdims=True))
            r = has_valid / jnp.maximum(jnp.sum(e, axis=0, keepdims=True), 1e-30)
            return _dot_tn(cv, e.astype(MXU_DTYPE)) * r, e * r

        lookahead = 4 if ncl <= LANES else 3
        pending = [_dot_nt(ck, heads[n]) for n in range(lookahead)]
        outs, probs = [], []
        for n in range(nheads):
            if n + lookahead < nheads:
                pending.append(_dot_nt(ck, heads[n + lookahead]))
            o, p = finish(pending[n])
            outs.append(o)
            probs.append(p)
        sel_t = []
        for h in range(KV_HEADS):
            pg = functools.reduce(jnp.add, probs[h * GROUP:(h + 1) * GROUP])
            hi = pg.astype(MXU_DTYPE)
            lo = (pg - hi.astype(jnp.float32)).astype(MXU_DTYPE)
            imp = (jnp.dot(overlap, hi, preferred_element_type=jnp.float32)
                   + jnp.dot(overlap, lo, preferred_element_type=jnp.float32))
            bias = _select_bias(jnp.where(forced, BIG, jnp.where(causal, imp, NEG)), live=live)
            sel_t += [bias] + ([jnp.full((nsb - live, tq), NEG, jnp.float32)] if live < nsb else [])
        o_ref[0] = _merge_heads([jnp.concatenate(outs[h * GROUP:(h + 1) * GROUP], axis=1)
                                 for h in range(KV_HEADS)], tq).astype(o_ref.dtype)
        pad = [jnp.zeros((LANES - KV_HEADS * nsb, tq), jnp.float32)] if KV_HEADS * nsb < LANES else []
        sel_ref[0] = jnp.concatenate(sel_t + pad, axis=0).T.astype(sel_ref.dtype)

    extents = list(range(RANK_STEP, nsb + 1, RANK_STEP))
    which = jnp.minimum((q0 + tq - 1) // (RANK_STEP * SLC_BLOCK), len(extents) - 1)
    for v, live in enumerate(extents):
        pl.when(which == v)(functools.partial(attend, live))


def _cmp_attention(qn, ck, cv):
    B, S, _ = qn.shape
    tq = min(CMP_Q, S)
    nsb = S // SLC_BLOCK
    ncp = ck.shape[1]
    return pl.pallas_call(
        functools.partial(_cmp_kernel, tq=tq, nsb=nsb),
        grid=(B, S // tq),
        in_specs=[pl.BlockSpec((1, tq, Q_WIDTH), lambda b, i: (b, i, 0)),
                  pl.BlockSpec((1, ncp, KV_WIDTH), lambda b, i: (b, 0, 0)),
                  pl.BlockSpec((1, ncp, KV_WIDTH), lambda b, i: (b, 0, 0))],
        out_specs=[pl.BlockSpec((1, tq, Q_WIDTH), lambda b, i: (b, i, 0)),
                   pl.BlockSpec((1, tq, LANES), lambda b, i: (b, i, 0))],
        out_shape=[jax.ShapeDtypeStruct((B, S, Q_WIDTH), MXU_DTYPE),
                   jax.ShapeDtypeStruct((B, S, LANES), MXU_DTYPE)],
        name="cmp_select",
        compiler_params=_params("parallel", "parallel"),
    )(qn, ck, cv)


def _slc_kernel(q_ref, k_ref, v_ref, sel_ref, qn_ref, kn_ref, seln_ref, o_ref, *scratch, tq, nsb):
    nheads = KV_HEADS * GROUP
    m_scr, acc_scr, s_scr = (scratch[i * nheads:(i + 1) * nheads] for i in range(3))
    qi = pl.program_id(1)
    erow = lax.broadcasted_iota(jnp.int32, (tq, LANES), 0) // SLC_BLOCK
    ecol = lax.broadcasted_iota(jnp.int32, (tq, LANES), 1)
    ecol = jnp.where(ecol < KV_HEADS * nsb, ecol % nsb, -1)

    def queries(q, bias):
        col = lax.broadcasted_iota(jnp.int32, bias.shape, 1)
        zero = jnp.zeros_like(bias)
        out = []
        for h, qs in enumerate(_split_heads(q)):
            bias_h = jnp.where((col >= h * nsb) & (col < (h + 1) * nsb), bias, zero)
            out += [jnp.concatenate([qs[g * tq:(g + 1) * tq], bias_h], axis=1) for g in range(GROUP)]
        return out

    def keys(k, kt):
        onehot = jnp.where(ecol == erow + kt * (tq // SLC_BLOCK), 1.0, 0.0).astype(MXU_DTYPE)
        return jnp.concatenate([k, onehot], axis=1)

    def consume(n, s, v):
        m_prev = m_scr[n][...]
        m_new = jnp.maximum(m_prev, jnp.max(s, axis=0, keepdims=True))
        alpha = jnp.exp2(m_prev - m_new)
        p = jnp.exp2(s - m_new)
        acc_scr[n][...] = alpha * acc_scr[n][...] + _dot_tn(v[n // GROUP], p.astype(MXU_DTYPE))
        m_scr[n][...] = m_new

    def step(kt, q_next, k_next, mask):
        v = _values_with_ones(v_ref[0, pl.ds(pl.multiple_of(kt * tq, tq), tq), :])
        s_cur = s_scr[0][...]
        s_scr[0][...] = _dot_nt(k_next, q_next[0])
        for n in range(nheads):
            s = s_cur
            if n + 1 < nheads:
                s_cur = s_scr[n + 1][...]
                s_scr[n + 1][...] = _dot_nt(k_next, q_next[n + 1])
            consume(n, s if mask is None else jnp.where(mask, s, NEG), v)

    q_aug = queries(q_ref[0], sel_ref[0])
    for n in range(nheads):
        m_scr[n][...] = jnp.full(m_scr[n].shape, NEG, jnp.float32)
        acc_scr[n][...] = jnp.zeros(acc_scr[n].shape, jnp.float32)

    @pl.when((pl.program_id(0) == 0) & (qi == 0))
    def _():
        k_first = keys(k_ref[0, pl.ds(0, tq), :], 0)
        for n in range(nheads):
            s_scr[n][...] = _dot_nt(k_first, q_aug[n])

    def inner(kt):
        step(kt, q_aug, keys(k_ref[0, pl.ds(pl.multiple_of((kt + 1) * tq, tq), tq), :], kt + 1), None)

    def trip(i, carry):
        for j in range(SLC_UNROLL):
            inner(SLC_UNROLL * i + j)
        return carry

    lax.fori_loop(0, qi // SLC_UNROLL, trip, 0)
    for j in range(1, SLC_UNROLL):
        pl.when(qi % SLC_UNROLL >= j)(functools.partial(lambda j: inner(qi - qi % SLC_UNROLL + j - 1), j))

    causal = lax.broadcasted_iota(jnp.int32, (tq, tq), 0) <= lax.broadcasted_iota(jnp.int32, (tq, tq), 1)
    step(qi, queries(qn_ref[0], seln_ref[0]), keys(kn_ref[0], 0), causal)
    o_t = []
    for h in range(KV_HEADS):
        den_row = (1 - h) * HEAD_DIM
        o_t.append(jnp.concatenate([acc_scr[n][...] / acc_scr[n][den_row:den_row + 1, :]
                                    for n in range(h * GROUP, (h + 1) * GROUP)], axis=1))
    o_ref[0] = _merge_heads(o_t, tq).astype(o_ref.dtype)


def _slc_attention(qn, kv, sel):
    B, S, _ = qn.shape
    tq = min(SLC_Q, S)
    nq = S // tq
    nsb = S // SLC_BLOCK

    def following(b, i):
        wrap = i + 1 == nq
        return jnp.where(wrap, jnp.minimum(b + 1, B - 1), b), jnp.where(wrap, 0, i + 1)

    return pl.pallas_call(
        functools.partial(_slc_kernel, tq=tq, nsb=nsb),
        grid=(B, nq),
        in_specs=[pl.BlockSpec((1, tq, Q_WIDTH), lambda b, i: (b, i, 0)),
                  pl.BlockSpec((1, S, KV_WIDTH), lambda b, i: (b, 0, 0)),
                  pl.BlockSpec((1, S, KV_WIDTH), lambda b, i: (b, 0, 3)),
                  pl.BlockSpec((1, tq, LANES), lambda b, i: (b, i, 0)),
                  pl.BlockSpec((1, tq, Q_WIDTH), lambda b, i: (*following(b, i), 0)),
                  pl.BlockSpec((1, tq, KV_WIDTH), lambda b, i: (following(b, i)[0], 0, 0)),
                  pl.BlockSpec((1, tq, LANES), lambda b, i: (*following(b, i), 0))],
        out_specs=pl.BlockSpec((1, tq, Q_WIDTH), lambda b, i: (b, i, 0)),
        out_shape=jax.ShapeDtypeStruct((B, S, Q_WIDTH), MXU_DTYPE),
        scratch_shapes=([pltpu.VMEM((1, tq), jnp.float32)] * (KV_HEADS * GROUP)
                        + [pltpu.VMEM((KV_WIDTH, tq), jnp.float32)] * (KV_HEADS * GROUP)
                        + [pltpu.VMEM((tq, tq), jnp.float32)] * (KV_HEADS * GROUP)),
        name="slc",
        compiler_params=_params("arbitrary", "arbitrary"),
    )(qn, kv, kv, sel, qn, kv, sel)


def _band_kernel(*refs, tq, tiles, window, sinks):
    if sinks:
        sink_ref, q_ref, k_ref, v_ref, o_ref = refs
    else:
        q_ref, k_ref, v_ref, o_ref = refs
    q0 = pl.program_id(1) * (tiles * tq)
    span = window + tq
    nsub, nhalf, wsub = span // LANES, tq // LANES, window // LANES
    nheads = KV_HEADS * GROUP
    kl = lax.broadcasted_iota(jnp.int32, (LANES, LANES), 0)
    ql = lax.broadcasted_iota(jnp.int32, (LANES, LANES), 1)

    def band_mask(delta):
        if delta >= 1 or delta + wsub <= -1:
            return False
        if delta == 0:
            return kl <= ql
        return kl > ql if delta + wsub == 0 else None

    def attend(spans):
        heads = [[qs[g * tq:(g + 1) * tq] for qs in _split_heads(q_ref[0, t * tq:(t + 1) * tq, :])
                  for g in range(GROUP)] for t in range(tiles)]
        keys = [k_ref[0, pl.ds(start, span), :] for start, _ in spans]
        values = [_values_with_ones(v_ref[0, pl.ds(start, span), :]) for start, _ in spans]

        def finish(t, n, s):
            shift, v = spans[t][1], values[t]
            cols, sink_terms = [], []
            den_row = (1 - n // GROUP) * HEAD_DIM
            for c in range(nhalf):
                live = {}
                for r in range(nsub):
                    mask = band_mask(r + shift - c)
                    if mask is False:
                        continue
                    blk = s[r * LANES:(r + 1) * LANES, c * LANES:(c + 1) * LANES]
                    live[r] = blk if mask is None else jnp.where(mask, blk, NEG)
                m = functools.reduce(jnp.maximum, [jnp.max(b, axis=0, keepdims=True) for b in live.values()])
                if sinks:
                    sk = jnp.full((1, LANES), sink_ref[n] * LOG2E, jnp.float32)
                    m = jnp.maximum(m, sk)
                e = {r: jnp.exp2(b - m) for r, b in live.items()}
                if sinks:
                    sink_terms.append(jnp.exp2(sk - m))
                zero = jnp.zeros((LANES, LANES), jnp.float32)
                cols.append(jnp.concatenate([e.get(r, zero) for r in range(nsub)], axis=0))
            p = jnp.concatenate(cols, axis=1).astype(MXU_DTYPE)
            pv = _dot_tn(v[n // GROUP], p)
            den = pv[den_row:den_row + 1, :]
            if sinks:
                den = den + jnp.concatenate(sink_terms, axis=1)
            return pv / den

        jobs = [(t, n) for t in range(tiles) for n in range(nheads)]
        lookahead = 3
        pending = [_dot_nt(keys[t], heads[t][n]) for t, n in jobs[:lookahead]]
        outs = []
        for i, (t, n) in enumerate(jobs):
            if i + lookahead < len(jobs):
                t2, n2 = jobs[i + lookahead]
                pending.append(_dot_nt(keys[t2], heads[t2][n2]))
            outs.append(finish(t, n, pending[i]))
            if n == nheads - 1:
                done, outs = outs, []
                o_ref[0, t * tq:(t + 1) * tq, :] = _merge_heads(
                    [jnp.concatenate(done[h * GROUP:(h + 1) * GROUP], axis=1) for h in range(KV_HEADS)],
                    tq).astype(o_ref.dtype)

    @pl.when(q0 >= window)
    def _():
        attend([(pl.multiple_of(q0 + t * tq - window, LANES), -wsub) for t in range(tiles)])

    @pl.when(q0 < window)
    def _():
        attend([(t * tq - window, -wsub) if t * tq >= window else (0, -(t * tq // LANES))
                for t in range(tiles)])


def _band_attention(q, kv, k_col, v_col, window, sinks):
    B, S, _ = q.shape
    tq = min(BAND_Q, S)
    tiles = BAND_TILES
    assert tiles * tq >= window and S % (tiles * tq) == 0
    in_specs = [pl.BlockSpec((1, tiles * tq, Q_WIDTH), lambda b, i: (b, i, 0)),
                pl.BlockSpec((1, S, KV_WIDTH), lambda b, i: (b, 0, k_col)),
                pl.BlockSpec((1, S, KV_WIDTH), lambda b, i: (b, 0, v_col))]
    args = [q, kv, kv]
    if sinks is not None:
        in_specs = [pl.BlockSpec(memory_space=pltpu.SMEM)] + in_specs
        args = [sinks] + args
    return pl.pallas_call(
        functools.partial(_band_kernel, tq=tq, tiles=tiles, window=window, sinks=sinks is not None),
        grid=(B, S // (tiles * tq)),
        in_specs=in_specs,
        out_specs=pl.BlockSpec((1, tiles * tq, Q_WIDTH), lambda b, i: (b, i, 0)),
        out_shape=jax.ShapeDtypeStruct((B, S, Q_WIDTH), MXU_DTYPE),
        name="band_sink" if sinks is not None else "band",
        compiler_params=_params("parallel", "parallel"),
    )(*args)


def _mix_kernel(x_ref, oc_ref, os_ref, ow_ref, osw_ref, gl_ref, ge_ref, w_ref, gn_ref, gs_ref, gp_ref, h_ref):
    sig = 1.0 / (1.0 + jnp.exp(-gl_ref[0]))
    hi = sig.astype(MXU_DTYPE)
    lo = (sig - hi.astype(jnp.float32)).astype(MXU_DTYPE)
    gates = jnp.dot(jnp.concatenate([hi, lo], axis=1), ge_ref[...],
                    preferred_element_type=jnp.float32)
    f32 = jnp.float32
    o_nsa = (gates[:, 0:Q_WIDTH] * oc_ref[0].astype(f32) + gates[:, Q_WIDTH:2 * Q_WIDTH] * os_ref[0].astype(f32)
             + gates[:, 2 * Q_WIDTH:3 * Q_WIDTH] * ow_ref[0].astype(f32))
    cat = jnp.concatenate([_rms(o_nsa, gn_ref[...]), _rms(osw_ref[0].astype(f32), gs_ref[...])],
                          axis=1).astype(MXU_DTYPE)
    mixed = jnp.dot(cat, w_ref[...], preferred_element_type=jnp.float32)
    h_ref[0] = x_ref[0] + _rms(mixed, gp_ref[...])


def _mix(x, o_cmp, o_slc, o_win, o_swa, gl, gate_expand, w_out, g_nsa, g_swa, g_post):
    B, S, D = x.shape
    ts = min(MIX_ROWS, S)
    row = lambda width: pl.BlockSpec((1, ts, width), lambda b, i: (b, i, 0))
    full = lambda a: pl.BlockSpec(a.shape, lambda b, i: (0,) * a.ndim)
    return pl.pallas_call(
        _mix_kernel,
        grid=(B, S // ts),
        in_specs=[row(D), row(Q_WIDTH), row(Q_WIDTH), row(Q_WIDTH), row(Q_WIDTH), row(LANES),
                  full(gate_expand), full(w_out), full(g_nsa), full(g_swa), full(g_post)],
        out_specs=row(D),
        out_shape=jax.ShapeDtypeStruct((B, S, D), jnp.float32),
        name="mix",
        compiler_params=_params("parallel", "parallel"),
    )(x, o_cmp, o_slc, o_win, o_swa, gl, gate_expand, w_out, g_nsa, g_swa, g_post)


def _ffn_kernel(h_ref, halo_ref, gpre_ref, wup_ref, cw_ref, cb_ref, wdn_ref, gpost_ref, y_ref, act_scr,
                *, ts, d_ff, chunk):
    rows = ts // FFN_PARTS
    sub = lax.broadcasted_iota(jnp.int32, (SUBLANES, chunk), 0)
    live = (pl.program_id(1) > 0).astype(jnp.float32)
    halo_hn = _rms(halo_ref[0], gpre_ref[...]) * live
    lhs = []
    for part in range(FFN_PARTS):
        hn = _rms(h_ref[0, part * rows:(part + 1) * rows, :], gpre_ref[...])
        lhs.append(jnp.concatenate([halo_hn, pltpu.einshape("(sa)d->(as)d", hn, s=SUBLANES)],
                                   axis=0).astype(MXU_DTYPE))
        halo_hn = hn[rows - HALO:]

    def conv(hn, col):
        u = jnp.dot(hn, wup_ref[:, col:col + chunk], preferred_element_type=jnp.float32)
        halo, u = u[:HALO], u[HALO:]
        t1 = jnp.where(sub == 0, halo[HALO - 1:HALO], pltpu.roll(u[rows - SUBLANES:], 1, 0))
        t2 = jnp.where(sub == 0, halo[HALO - 2:HALO - 1], pltpu.roll(u[rows - 2 * SUBLANES:rows - SUBLANES], 1, 0))
        tap1 = jnp.concatenate([t1, u[:rows - SUBLANES]], axis=0)
        tap2 = jnp.concatenate([t2, t1, u[:rows - 2 * SUBLANES]], axis=0)
        w = cw_ref[:, col:col + chunk]
        return w[0:1] * tap2 + w[1:2] * tap1 + w[2:3] * u + cb_ref[:, col:col + chunk]

    for part in range(FFN_PARTS):
        for col in range(0, d_ff, chunk):
            act = _gelu_tanh(conv(lhs[part], col)) * conv(lhs[part], d_ff + col)
            act_scr[part * rows:(part + 1) * rows, col:col + chunk] = act.astype(MXU_DTYPE)
    ys = [jnp.dot(act_scr[part * rows:(part + 1) * rows, :], wdn_ref[...], preferred_element_type=jnp.float32)
          for part in range(FFN_PARTS)]
    for part in range(FFN_PARTS):
        y = pltpu.einshape("(as)d->(sa)d", ys[part], s=SUBLANES)
        y_ref[0, part * rows:(part + 1) * rows, :] = (h_ref[0, part * rows:(part + 1) * rows, :]
                                                      + _rms(y, gpost_ref[...]))


def _ffn(h, g_pre, w_up, conv_w, conv_b, w_down, g_post):
    B, S, D = h.shape
    ts = min(FFN_ROWS, S)
    d_ff = w_down.shape[0]
    full = lambda a: pl.BlockSpec(a.shape, lambda b, i: (0,) * a.ndim, pipeline_mode=pl.Buffered(1))
    return pl.pallas_call(
        functools.partial(_ffn_kernel, ts=ts, d_ff=d_ff, chunk=FFN_CHUNK),
        grid=(B, S // ts),
        in_specs=[pl.BlockSpec((1, ts, D), lambda b, i: (b, i, 0)),
                  pl.BlockSpec((1, HALO, D), lambda b, i: (b, jnp.maximum(i * (ts // HALO) - 1, 0), 0)),
                  full(g_pre), full(w_up), full(conv_w), full(conv_b), full(w_down), full(g_post)],
        out_specs=pl.BlockSpec((1, ts, D), lambda b, i: (b, i, 0)),
        out_shape=jax.ShapeDtypeStruct((B, S, D), jnp.float32),
        scratch_shapes=[pltpu.VMEM((ts, d_ff), MXU_DTYPE)],
        name="ffn",
        compiler_params=_params("parallel", "parallel"),
    )(h, h, g_pre, w_up, conv_w, conv_b, w_down, g_post)


def _rope_tables(S):
    half = HEAD_DIM // 2
    inv = ROPE_THETA ** (-jnp.arange(half, dtype=jnp.float32) / half)
    ang = jnp.arange(S).astype(jnp.float32)[:, None] * inv[None, :]
    cos, sin = jnp.cos(ang), jnp.sin(ang)
    reps = LANES // HEAD_DIM
    return jnp.tile(jnp.concatenate([cos, cos], axis=1), (1, reps)), jnp.tile(jnp.concatenate([-sin, sin], axis=1), (1, reps))


def _layer(x, w_in, w_out, attn_pre_norm, attn_post_norm, nsa_out_norm, swa_out_norm, cmp_pos, cmp_w1, cmp_w2,
           swa_sinks, ffn_pre_norm, ffn_post_norm, w_up, conv_w, conv_b, w_down):
    B, S, D = x.shape
    f32 = jnp.float32
    perm = _q_perm()
    n_gate = KV_HEADS * GROUP * NSA_BRANCHES
    sizes = [Q_WIDTH] + [KV_WIDTH] * 6 + [n_gate] + [Q_WIDTH, KV_WIDTH, KV_WIDTH]
    q_n, k_c, v_c, k_s, v_s, k_w, v_w, g_n, q_w, k_sw, v_sw = np.cumsum([0] + sizes)[:-1]
    slab = np.arange(KV_WIDTH)
    cols = np.concatenate([q_n + perm, q_w + perm] + [o + slab for o in (k_c, k_s, k_w, k_sw, v_c, v_s, v_w, v_sw)]
                          + [g_n + np.arange(n_gate), np.zeros(LANES - n_gate, np.int64)])
    live = np.arange(cols.size) < cols.size - (LANES - n_gate)
    w_all = jnp.where(live, jnp.take(w_in, jnp.asarray(cols, jnp.int32), axis=1), 0.0).astype(MXU_DTYPE)
    cos, sin = _rope_tables(S)
    row = lambda g: g.reshape(1, -1).astype(f32)

    qn, qw, kc, vc, kv, gl = _proj(x, row(attn_pre_norm), w_all, cos, sin)

    pe = jnp.broadcast_to(cmp_pos.reshape(2, 2, CMP_STRIDE, 1, HEAD_DIM), (2, 2, CMP_STRIDE, KV_HEADS, HEAD_DIM))
    pe = pe.reshape(2, 2, 1, CMP_STRIDE * KV_WIDTH)
    w1 = cmp_w1.astype(MXU_DTYPE).reshape(2, 2, CMP_STRIDE, HEAD_DIM, CMP_HIDDEN)
    w1x = jnp.stack([jnp.concatenate([w1, jnp.zeros_like(w1)], axis=-1),
                     jnp.concatenate([jnp.zeros_like(w1), w1], axis=-1)], axis=3)
    w1x = w1x.reshape(2, 2, CMP_STRIDE * KV_WIDTH, KV_HEADS * CMP_HIDDEN)
    w2 = cmp_w2.astype(MXU_DTYPE)
    w2x = jnp.concatenate([jnp.concatenate([w2, jnp.zeros_like(w2)], axis=-1),
                           jnp.concatenate([jnp.zeros_like(w2), w2], axis=-1)], axis=1)
    ck, cv = _compress(kc, vc, pe, w1x, w2x)

    o_cmp, sel = _cmp_attention(qn, ck, cv)
    o_slc = _slc_attention(qn, kv, sel)
    o_win = _band_attention(qn, kv, 1, 4, NSA_WINDOW, None)
    o_swa = _band_attention(qw, kv, 2, 5, SWA_WINDOW, swa_sinks.astype(f32))

    n = np.arange(NSA_BRANCHES * Q_WIDTH)
    br, g, h = n // Q_WIDTH, (n % Q_WIDTH) // LANES, (n % LANES) // HEAD_DIM
    expand = np.zeros((LANES, NSA_BRANCHES * Q_WIDTH), np.float32)
    expand[(h * GROUP + g) * NSA_BRANCHES + br, n] = 1.0
    expand = np.concatenate([expand, expand], axis=0)
    w_out_p = jnp.take(w_out, jnp.asarray(np.concatenate([perm, Q_WIDTH + perm]), jnp.int32),
                       axis=0).astype(MXU_DTYPE)
    hmid = _mix(x, o_cmp, o_slc, o_win, o_swa, gl, jnp.asarray(expand, MXU_DTYPE), w_out_p,
                row(nsa_out_norm[perm]), row(swa_out_norm[perm]), row(attn_post_norm))

    return _ffn(hmid, row(ffn_pre_norm), w_up.astype(MXU_DTYPE), conv_w.astype(f32), row(conv_b),
                w_down.astype(MXU_DTYPE), row(ffn_post_norm))


def kernel(x, w_in, w_out, attn_pre_norm, attn_post_norm, nsa_out_norm, swa_out_norm, cmp_pos, cmp_w1, cmp_w2,
           swa_sinks, ffn_pre_norm, ffn_post_norm, w_up, conv_w, conv_b, w_down):
    h = x
    for l in range(w_in.shape[0]):
        h = _layer(h, w_in[l], w_out[l], attn_pre_norm[l], attn_post_norm[l], nsa_out_norm[l], swa_out_norm[l],
                   cmp_pos[l], cmp_w1[l], cmp_w2[l], swa_sinks[l], ffn_pre_norm[l], ffn_post_norm[l],
                   w_up[l], conv_w[l], conv_b[l], w_down[l])
    return h
```
